```python
import math, functools
import jax, jax.numpy as jnp
from jax import lax
import numpy as np

D_MODEL = 2048
BATCH = 2
SEQ = 4096
DEPTH = 1
DEC_BATCH = 8
DEC_SEQ = 8
PAST_LEN = 16384
PAGE_SIZE = 128

ATT_WIDTH = D_MODEL // 2
SSM_WIDTH = D_MODEL - ATT_WIDTH
HEAD_DIM = 128
N_ATT_HEADS = ATT_WIDTH // HEAD_DIM
SSM_GROUP = 16
N_SSM_GROUPS = SSM_WIDTH // SSM_GROUP
SSM_STATE = 64
D_FF = 4 * D_MODEL
Q_BLOCK = 128
ALPHA = (2 * DEPTH) ** 0.25
BETA = (8 * DEPTH) ** -0.25
LN_EPS = 1e-5
N_MOD = 6
IN_COLS = 3 * ATT_WIDTH + N_ATT_HEADS + SSM_WIDTH
SCALE = HEAD_DIM ** -0.5
FORGET_BIAS_LO = 2.0
FORGET_BIAS_HI = 10.0

kernel_name = "fox_s5_parallel_heads_deepnorm_adaln_step"


def _layer_norm(x, g=None, b=None):
    xf = x.astype(jnp.float32)
    mu = jnp.mean(xf, axis=-1, keepdims=True)
    var = jnp.mean(jnp.square(xf - mu), axis=-1, keepdims=True)
    y = (xf - mu) * lax.rsqrt(var + LN_EPS)
    if g is not None:
        y = y * g.astype(jnp.float32) + b.astype(jnp.float32)
    return y.astype(x.dtype)


def _attend_prompt(q, k, v, logf):
    bsz, s_len = q.shape[:2]
    n_blk = s_len // Q_BLOCK
    f_cum = jnp.transpose(lax.cumsum(logf, axis=1), (0, 2, 1))
    q_blk = jnp.transpose(q.reshape(bsz, n_blk, Q_BLOCK, N_ATT_HEADS, HEAD_DIM), (1, 0, 2, 3, 4))
    f_blk = jnp.transpose(f_cum.reshape(bsz, N_ATT_HEADS, n_blk, Q_BLOCK), (2, 0, 1, 3))
    k_pos = jnp.arange(s_len)

    def block(args):
        qi, fi, i = args
        s = (jnp.einsum('bqhd,bkhd->bhqk', qi, k).astype(jnp.float32) * SCALE
             + fi[..., :, None] - f_cum[:, :, None, :])
        q_pos = i * Q_BLOCK + jnp.arange(Q_BLOCK)
        s = jnp.where(k_pos[None, :] <= q_pos[:, None], s, -jnp.inf)
        p = jax.nn.softmax(s, axis=-1).astype(v.dtype)
        return jnp.einsum('bhqk,bkhd->bqhd', p, v)

    out = lax.map(block, (q_blk, f_blk, jnp.arange(n_blk)))
    return jnp.transpose(out, (1, 0, 2, 3, 4)).reshape(bsz, s_len, ATT_WIDTH)


def _attend_sample(q, k, v, logf, cache_k, cache_v, cache_logf, page_table):
    nb, n_q = q.shape[:2]
    k_past = cache_k[page_table].reshape(nb, -1, N_ATT_HEADS, HEAD_DIM)
    v_past = cache_v[page_table].reshape(nb, -1, N_ATT_HEADS, HEAD_DIM)
    lf_past = cache_logf[page_table].reshape(nb, -1, N_ATT_HEADS).astype(jnp.float32)
    past = k_past.shape[1]
    f_new = jnp.transpose(lax.cumsum(logf, axis=1), (0, 2, 1))
    decay_past = jnp.transpose(lax.cumsum(lf_past, axis=1, reverse=True) - lf_past, (0, 2, 1))
    s_past = (jnp.einsum('bqhd,bkhd->bhqk', q, k_past).astype(jnp.float32) * SCALE
              + decay_past[:, :, None, :] + f_new[:, :, :, None])
    s_new = (jnp.einsum('bqhd,bkhd->bhqk', q, k).astype(jnp.float32) * SCALE
             + f_new[:, :, :, None] - f_new[:, :, None, :])
    causal = jnp.arange(n_q)[None, :] <= jnp.arange(n_q)[:, None]
    s_new = jnp.where(causal, s_new, -jnp.inf)
    p = jax.nn.softmax(jnp.concatenate([s_past, s_new], axis=-1), axis=-1).astype(v.dtype)
    out = (jnp.einsum('bhqk,bkhd->bqhd', p[..., :past], v_past)
           + jnp.einsum('bhqk,bkhd->bqhd', p[..., past:], v))
    return out.reshape(nb, n_q, ATT_WIDTH)


def _s5(u, h0, a_re, a_im, log_dt, b_re, b_im, c_re, c_im, d_skip, w_glu, b_glu):
    f32 = jnp.float32
    bsz, l_len = u.shape[:2]
    u = u.astype(f32).reshape(bsz, l_len, N_SSM_GROUPS, SSM_GROUP)
    a_re, a_im = a_re.astype(f32), a_im.astype(f32)
    dt = jnp.exp(log_dt.astype(f32))[:, None]
    mag = jnp.exp(a_re * dt)
    lam_re, lam_im = mag * jnp.cos(a_im * dt), mag * jnp.sin(a_im * dt)
    den = jnp.square(a_re) + jnp.square(a_im)
    n_re = lam_re - 1.0
    k_re = (n_re * a_re + lam_im * a_im) / den
    k_im = (lam_im * a_re - n_re * a_im) / den
    b_re, b_im = b_re.astype(f32), b_im.astype(f32)
    bb_re = k_re[..., None] * b_re - k_im[..., None] * b_im
    bb_im = k_re[..., None] * b_im + k_im[..., None] * b_re
    bu_re = jnp.einsum('blgc,gpc->blgp', u, bb_re)
    bu_im = jnp.einsum('blgc,gpc->blgp', u, bb_im)
    ar = jnp.broadcast_to(lam_re, bu_re.shape)
    ai = jnp.broadcast_to(lam_im, bu_re.shape)

    def combine(e1, e2):
        a1r, a1i, b1r, b1i = e1
        a2r, a2i, b2r, b2i = e2
        return (a1r * a2r - a1i * a2i, a1r * a2i + a1i * a2r,
                a2r * b1r - a2i * b1i + b2r, a2r * b1i + a2i * b1r + b2i)

    _, _, h_re, h_im = lax.associative_scan(combine, (ar, ai, bu_re, bu_im), axis=1)
    if h0 is not None:
        h0_re, h0_im = h0[0].astype(f32)[:, None], h0[1].astype(f32)[:, None]
        steps = jnp.arange(1, l_len + 1, dtype=f32)[:, None, None]
        pm = jnp.exp(a_re * dt * steps)
        p_re, p_im = pm * jnp.cos(a_im * dt * steps), pm * jnp.sin(a_im * dt * steps)
        h_re = h_re + p_re * h0_re - p_im * h0_im
        h_im = h_im + p_re * h0_im + p_im * h0_re
    y = (jnp.einsum('gcp,blgp->blgc', c_re.astype(f32), h_re)
         - jnp.einsum('gcp,blgp->blgc', c_im.astype(f32), h_im)
         + d_skip.astype(f32) * u)
    z = jax.nn.gelu(y)
    zz = jnp.einsum('blgc,gce->blge', z, w_glu.astype(f32)) + b_glu.astype(f32)
    out = zz[..., :SSM_GROUP] * jax.nn.sigmoid(zz[..., SSM_GROUP:])
    return out.reshape(bsz, l_len, SSM_WIDTH), h_re[:, -1], h_im[:, -1]


def _trunk_layer(x, c, attend, h0, w_ada, b_ada, w_in, b_f, w_o, a_re, a_im, log_dt,
                 b_re, b_im, c_re, c_im, d_skip, w_glu, b_glu, ln1_g, ln1_b,
                 w_up, w_down, ln2_g, ln2_b):
    bsz, l_len, _ = x.shape
    mod = (jax.nn.silu(c) @ w_ada + b_ada).reshape(bsz, N_MOD, 1, D_MODEL)
    shift1, scale1, gate1, shift2, scale2, gate2 = (mod[:, i] for i in range(N_MOD))
    h = _layer_norm(x) * (1 + scale1) + shift1
    proj = h @ w_in
    cuts = [ATT_WIDTH, 2 * ATT_WIDTH, 3 * ATT_WIDTH, 3 * ATT_WIDTH + N_ATT_HEADS]
    q, k, v, f_logit, u = jnp.split(proj, cuts, axis=-1)
    q = q.reshape(bsz, l_len, N_ATT_HEADS, HEAD_DIM)
    k = k.reshape(bsz, l_len, N_ATT_HEADS, HEAD_DIM)
    v = v.reshape(bsz, l_len, N_ATT_HEADS, HEAD_DIM)
    logf = jax.nn.log_sigmoid(f_logit.astype(jnp.float32) + b_f.astype(jnp.float32))
    att = attend(q, k, v, logf)
    ssm, h_re, h_im = _s5(u, h0, a_re, a_im, log_dt, b_re, b_im, c_re, c_im, d_skip, w_glu, b_glu)
    mix = jnp.concatenate([att, ssm.astype(x.dtype)], axis=-1) @ w_o
    x = _layer_norm(ALPHA * x + (1 + gate1) * mix, ln1_g, ln1_b)
    h2 = _layer_norm(x) * (1 + scale2) + shift2
    m = jnp.square(jax.nn.relu(h2 @ w_up)) @ w_down
    x = _layer_norm(ALPHA * x + (1 + gate2) * m, ln2_g, ln2_b)
    return x, k, v, logf.astype(x.dtype), h_re, h_im


def setup_inputs(seed: int = 0) -> dict:
    key = jax.random.key(seed)
    ks = iter(jax.random.split(key, 40))
    nrm = lambda shape, s=1.0: jax.random.normal(next(ks), shape, jnp.float32) * s
    n_pages = PAST_LEN // PAGE_SIZE
    n_used = DEC_BATCH * n_pages
    n_pool = n_used + max(1, n_used // 4)
    page_table = jax.random.permutation(next(ks), n_pool)[:n_used].reshape(DEC_BATCH, n_pages).astype(jnp.int32)
    L = DEPTH
    head_bias = jnp.linspace(FORGET_BIAS_LO, FORGET_BIAS_HI, N_ATT_HEADS, dtype=jnp.float32)
    return {
        "x_prompt": nrm((BATCH, SEQ, D_MODEL)),
        "x_sample": nrm((DEC_BATCH, DEC_SEQ, D_MODEL)),
        "c_prompt": nrm((BATCH, D_MODEL)),
        "c_sample": nrm((DEC_BATCH, D_MODEL)),
        "cache_k": nrm((L, n_pool, PAGE_SIZE, N_ATT_HEADS, HEAD_DIM)),
        "cache_v": nrm((L, n_pool, PAGE_SIZE, N_ATT_HEADS, HEAD_DIM)),
        "cache_logf": jax.nn.log_sigmoid(head_bias + nrm((L, n_pool, PAGE_SIZE, N_ATT_HEADS), 0.5)),
        "state_ssm_re": nrm((L, DEC_BATCH, N_SSM_GROUPS, SSM_STATE), 0.1),
        "state_ssm_im": nrm((L, DEC_BATCH, N_SSM_GROUPS, SSM_STATE), 0.1),
        "page_table": page_table,
        "w_ada": nrm((L, D_MODEL, N_MOD * D_MODEL), 0.5 * D_MODEL ** -0.5),
        "b_ada": nrm((L, N_MOD * D_MODEL), 0.02),
        "w_in": nrm((L, D_MODEL, IN_COLS), D_MODEL ** -0.5),
        "b_f": head_bias + nrm((L, N_ATT_HEADS), 0.5),
        "w_o": nrm((L, D_MODEL, D_MODEL), BETA * D_MODEL ** -0.5),
        "a_re": -0.5 * jnp.exp(nrm((L, N_SSM_GROUPS, SSM_STATE), 0.05)),
        "a_im": jnp.pi * jnp.arange(SSM_STATE, dtype=jnp.float32) + nrm((L, N_SSM_GROUPS, SSM_STATE), 0.01),
        "log_dt": jax.random.uniform(next(ks), (L, N_SSM_GROUPS), jnp.float32, math.log(1e-3), math.log(1e-1)),
        "b_re": nrm((L, N_SSM_GROUPS, SSM_STATE, SSM_GROUP), (2 * SSM_GROUP) ** -0.5),
        "b_im": nrm((L, N_SSM_GROUPS, SSM_STATE, SSM_GROUP), (2 * SSM_GROUP) ** -0.5),
        "c_re": nrm((L, N_SSM_GROUPS, SSM_GROUP, SSM_STATE), (2 * SSM_STATE) ** -0.5),
        "c_im": nrm((L, N_SSM_GROUPS, SSM_GROUP, SSM_STATE), (2 * SSM_STATE) ** -0.5),
        "d_skip": nrm((L, N_SSM_GROUPS, SSM_GROUP)),
        "w_glu": nrm((L, N_SSM_GROUPS, SSM_GROUP, 2 * SSM_GROUP), SSM_GROUP ** -0.5),
        "b_glu": nrm((L, N_SSM_GROUPS, 2 * SSM_GROUP), 0.02),
        "ln1_g": 1.0 + nrm((L, D_MODEL), 0.02),
        "ln1_b": nrm((L, D_MODEL), 0.02),
        "w_up": nrm((L, D_MODEL, D_FF), D_MODEL ** -0.5),
        "w_down": nrm((L, D_FF, D_MODEL), BETA * D_FF ** -0.5),
        "ln2_g": 1.0 + nrm((L, D_MODEL), 0.02),
        "ln2_b": nrm((L, D_MODEL), 0.02),
    }


def reference(x_prompt, x_sample, c_prompt, c_sample, cache_k, cache_v, cache_logf,
              state_ssm_re, state_ssm_im, page_table, w_ada, b_ada, w_in, b_f, w_o,
              a_re, a_im, log_dt, b_re, b_im, c_re, c_im, d_skip, w_glu, b_glu,
              ln1_g, ln1_b, w_up, w_down, ln2_g, ln2_b):
    yp, ys = x_prompt, x_sample
    kp_l, vp_l, fp_l, hpr_l, hpi_l = [], [], [], [], []
    ks_l, vs_l, fs_l, hsr_l, hsi_l = [], [], [], [], []
    for l in range(DEPTH):
        weights = (w_ada[l], b_ada[l], w_in[l], b_f[l], w_o[l], a_re[l], a_im[l], log_dt[l],
                   b_re[l], b_im[l], c_re[l], c_im[l], d_skip[l], w_glu[l], b_glu[l],
                   ln1_g[l], ln1_b[l], w_up[l], w_down[l], ln2_g[l], ln2_b[l])
        yp, kp, vp, fp, hpr, hpi = _trunk_layer(yp, c_prompt, _attend_prompt, None, *weights)
        attend_s = functools.partial(_attend_sample, cache_k=cache_k[l], cache_v=cache_v[l],
                                     cache_logf=cache_logf[l], page_table=page_table)
        ys, kss, vss, fss, hsr, hsi = _trunk_layer(ys, c_sample, attend_s,
                                                   (state_ssm_re[l], state_ssm_im[l]), *weights)
        kp_l.append(kp); vp_l.append(vp); fp_l.append(fp); hpr_l.append(hpr); hpi_l.append(hpi)
        ks_l.append(kss); vs_l.append(vss); fs_l.append(fss); hsr_l.append(hsr); hsi_l.append(hsi)
    return (yp, ys,
            jnp.stack(kp_l), jnp.stack(vp_l), jnp.stack(fp_l), jnp.stack(hpr_l), jnp.stack(hpi_l),
            jnp.stack(ks_l), jnp.stack(vs_l), jnp.stack(fs_l), jnp.stack(hsr_l), jnp.stack(hsi_l))
```

```python
import functools

import jax
import jax.numpy as jnp
from jax import lax
from jax.experimental import pallas as pl
from jax.experimental.pallas import tpu as pltpu

F32 = jnp.float32
BF16 = jnp.bfloat16

D_MODEL = 2048
ATT_WIDTH = 1024
SSM_WIDTH = 1024
HEAD_DIM = 128
N_HEADS = 8
SSM_GROUP = 16
N_GROUPS = 64
SSM_STATE = 64
D_FF = 8192
N_MOD = 6
PAGE = 128
DEPTH = 1
ALPHA = (2 * DEPTH) ** 0.25
LN_EPS = 1e-5
SCALE = HEAD_DIM ** -0.5

GROUPS_PER_BLOCK = 8
N_GBLK = N_GROUPS // GROUPS_PER_BLOCK
ST_LANES = GROUPS_PER_BLOCK * SSM_STATE
CH_LANES = GROUPS_PER_BLOCK * SSM_GROUP

VMEM_LIMIT = 48 * 1024 * 1024


def _cparams(sem):
    return pltpu.CompilerParams(dimension_semantics=sem, vmem_limit_bytes=VMEM_LIMIT)


def _ln(x):
    mu = jnp.mean(x, axis=-1, keepdims=True)
    xc = x - mu
    var = jnp.mean(xc * xc, axis=-1, keepdims=True)
    return xc * lax.rsqrt(var + LN_EPS)


def _log_sigmoid(x):
    return jnp.minimum(x, 0.0) - jnp.log1p(jnp.exp(-jnp.abs(x)))


def _ada_kernel(c_ref, w_ref, b_ref, o_ref):
    c = c_ref[...]
    s = c * jax.nn.sigmoid(c)
    o_ref[...] = jnp.dot(s, w_ref[...], preferred_element_type=F32) + b_ref[...]


def _ada(c_all, w_ada, b_ada):
    rows = c_all.shape[0]
    n = w_ada.shape[1]
    tn = 1024
    return pl.pallas_call(
        _ada_kernel,
        grid=(n // tn,),
        in_specs=[pl.BlockSpec((rows, D_MODEL), lambda j: (0, 0)),
                  pl.BlockSpec((D_MODEL, tn), lambda j: (0, j)),
                  pl.BlockSpec((1, tn), lambda j: (0, j))],
        out_specs=pl.BlockSpec((rows, tn), lambda j: (0, j)),
        out_shape=jax.ShapeDtypeStruct((rows, n), F32),
        compiler_params=_cparams(("arbitrary",)),
        name="ada",
    )(c_all, w_ada, b_ada.reshape(1, n))


def _s5prep_kernel(are_ref, aim_ref, ldt_ref, arx_ref, aix_ref, ldx_ref, bre_ref, bim_ref,
                   lre_ref, lim_ref, bbre_ref, bbim_ref):
    def lam(a_re, a_im, log_dt):
        dt = jnp.exp(log_dt)
        mag = jnp.exp(a_re * dt)
        return mag * jnp.cos(a_im * dt), mag * jnp.sin(a_im * dt)

    l_re, l_im = lam(are_ref[...], aim_ref[...], ldt_ref[...])
    lre_ref[...] = l_re
    lim_ref[...] = l_im
    a_re, a_im = arx_ref[...], aix_ref[...]
    x_re, x_im = lam(a_re, a_im, ldx_ref[...])
    den = a_re * a_re + a_im * a_im
    n_re = x_re - 1.0
    k_re = (n_re * a_re + x_im * a_im) / den
    k_im = (x_im * a_re - n_re * a_im) / den
    b_re, b_im = bre_ref[...], bim_ref[...]
    bbre_ref[...] = k_re * b_re - k_im * b_im
    bbim_ref[...] = k_re * b_im + k_im * b_re


def _s5prep(a_re, a_im, log_dt, b_re, b_im):
    g, p, c = b_re.shape
    ldt = jnp.broadcast_to(log_dt[:, None], (g, p))
    ex = lambda a: jnp.broadcast_to(a[:, :, None], (g, p, c)).reshape(g, p * c)
    small = jax.ShapeDtypeStruct((g, p), F32)
    big = jax.ShapeDtypeStruct((g, p * c), F32)
    l_re, l_im, bb_re, bb_im = pl.pallas_call(
        _s5prep_kernel, out_shape=(small, small, big, big), name="s5prep",
    )(a_re, a_im, ldt, ex(a_re), ex(a_im), ex(ldt), b_re.reshape(g, p * c), b_im.reshape(g, p * c))
    return l_re, l_im, bb_re.reshape(g, p, c), bb_im.reshape(g, p, c)


def _blockdiag(w):
    g, a, b = w.shape
    w = w.reshape(N_GBLK, GROUPS_PER_BLOCK, a, b)
    eye = jnp.eye(GROUPS_PER_BLOCK, dtype=w.dtype)
    return jnp.einsum('xgab,gh->xgahb', w, eye).reshape(N_GBLK, GROUPS_PER_BLOCK * a, GROUPS_PER_BLOCK * b)


def _s5_weights(l_re, l_im, bb_re, bb_im, c_re, c_im, d_skip, w_glu, b_glu):
    wb = jnp.concatenate([_blockdiag(jnp.swapaxes(bb_re, 1, 2)),
                          _blockdiag(jnp.swapaxes(bb_im, 1, 2))], axis=2)
    wc = jnp.concatenate([_blockdiag(jnp.swapaxes(c_re, 1, 2)),
                          _blockdiag(-jnp.swapaxes(c_im, 1, 2))], axis=1)
    wg = jnp.concatenate([_blockdiag(w_glu[:, :, :SSM_GROUP]),
                          _blockdiag(w_glu[:, :, SSM_GROUP:])], axis=2)
    bg = jnp.concatenate([b_glu[:, :SSM_GROUP].reshape(N_GBLK, 1, CH_LANES),
                          b_glu[:, SSM_GROUP:].reshape(N_GBLK, 1, CH_LANES)], axis=2)
    dsk = d_skip.reshape(N_GBLK, 1, CH_LANES)
    lam = jnp.concatenate([l_re.reshape(N_GBLK, 1, ST_LANES), l_im.reshape(N_GBLK, 1, ST_LANES)], axis=2)
    return wb, wc, wg, bg, dsk, lam


def _inproj_kernel(x_ref, sh_ref, sc_ref, w_ref, wf_ref, wft_ref, bf_ref, bfr_ref,
                   q_ref, kf_ref, kb_ref, vf_ref, vb_ref, u_ref, lfc_ref, lfr_ref, h_scr):
    j = pl.program_id(2)

    @pl.when(j == 0)
    def _():
        h = _ln(x_ref[0]) * (1.0 + sc_ref[0]) + sh_ref[0]
        hb = h.astype(BF16)
        h_scr[...] = hb
        fl = jnp.dot(hb, wf_ref[...], preferred_element_type=F32) + bf_ref[...]
        lfc_ref[0] = _log_sigmoid(fl)[:, :N_HEADS]
        flr = lax.dot_general(wft_ref[...], hb, (((1,), (1,)), ((), ())),
                              preferred_element_type=F32) + bfr_ref[...]
        lfr_ref[0] = _log_sigmoid(flr)

    acc = jnp.dot(h_scr[...], w_ref[...], preferred_element_type=F32)

    @pl.when(j == 0)
    def _():
        q_ref[0] = (acc * SCALE).astype(BF16)

    @pl.when(j == 1)
    def _():
        kf_ref[0] = acc
        kb_ref[0] = acc.astype(BF16)

    @pl.when(j == 2)
    def _():
        vf_ref[0] = acc
        vb_ref[0] = acc.astype(BF16)

    @pl.when(j == 3)
    def _():
        u_ref[0] = acc


def _inproj(x, shift, scale, w_main, w_f, w_ft, b_f, tm):
    nb, l, _ = x.shape
    r = shift.shape[1]
    mod_map = (lambda b, i, j: (b, 0, 0)) if r == 1 else (lambda b, i, j: (b, i, 0))
    row_map = lambda b, i, j: (b, i, 0)
    const2 = lambda b, i, j: (0, 0)
    wide = lambda dt: jax.ShapeDtypeStruct((nb, l, ATT_WIDTH), dt)
    return pl.pallas_call(
        _inproj_kernel,
        grid=(nb, l // tm, 4),
        in_specs=[pl.BlockSpec((1, tm, D_MODEL), row_map),
                  pl.BlockSpec((1, r, D_MODEL), mod_map),
                  pl.BlockSpec((1, r, D_MODEL), mod_map),
                  pl.BlockSpec((D_MODEL, ATT_WIDTH), lambda b, i, j: (0, j)),
                  pl.BlockSpec((D_MODEL, 128), const2),
                  pl.BlockSpec((N_HEADS, D_MODEL), const2),
                  pl.BlockSpec((1, 128), const2),
                  pl.BlockSpec((N_HEADS, 1), const2)],
        out_specs=[pl.BlockSpec((1, tm, ATT_WIDTH), row_map)] * 6
                  + [pl.BlockSpec((1, tm, N_HEADS), row_map),
                     pl.BlockSpec((1, N_HEADS, tm), lambda b, i, j: (b, 0, i))],
        out_shape=(wide(BF16), wide(F32), wide(BF16), wide(F32), wide(BF16), wide(F32),
                   jax.ShapeDtypeStruct((nb, l, N_HEADS), F32),
                   jax.ShapeDtypeStruct((nb, N_HEADS, l), F32)),
        scratch_shapes=[pltpu.VMEM((tm, D_MODEL), BF16)],
        compiler_params=_cparams(("arbitrary", "arbitrary", "arbitrary")),
        name="inproj",
    )(x, shift, scale, w_main, w_f, w_ft, jnp.pad(b_f, (0, 128 - N_HEADS)).reshape(1, 128),
      b_f.reshape(N_HEADS, 1))


CUM_BLK = 512


def _cumsum_kernel(lfc_ref, lfr_ref, fc_ref, fr_ref):
    s_len = lfc_ref.shape[1]
    r = lax.broadcasted_iota(jnp.int32, (CUM_BLK, CUM_BLK), 0)
    c = lax.broadcasted_iota(jnp.int32, (CUM_BLK, CUM_BLK), 1)
    tri_l = (c <= r).astype(F32)
    tri_u = (r <= c).astype(F32)
    carry_c = jnp.zeros((1, N_HEADS), F32)
    carry_r = jnp.zeros((N_HEADS, 1), F32)
    for blk in range(s_len // CUM_BLK):
        rows = slice(blk * CUM_BLK, (blk + 1) * CUM_BLK)
        fc = jnp.dot(tri_l, lfc_ref[0, rows, :], precision=lax.Precision.HIGHEST,
                     preferred_element_type=F32) + carry_c
        fc_ref[0, rows, :] = fc
        carry_c = fc[CUM_BLK - 1:CUM_BLK, :]
        fr = jnp.dot(lfr_ref[0, :, rows], tri_u, precision=lax.Precision.HIGHEST,
                     preferred_element_type=F32) + carry_r
        fr_ref[0, :, rows] = fr
        carry_r = fr[:, CUM_BLK - 1:CUM_BLK]


def _cumsum(lfc, lfr):
    nb, s_len, _ = lfc.shape
    cspec = pl.BlockSpec((1, s_len, N_HEADS), lambda b: (b, 0, 0))
    rspec = pl.BlockSpec((1, N_HEADS, s_len), lambda b: (b, 0, 0))
    return pl.pallas_call(
        _cumsum_kernel, grid=(nb,), in_specs=[cspec, rspec], out_specs=[cspec, rspec],
        out_shape=(jax.ShapeDtypeStruct(lfc.shape, F32), jax.ShapeDtypeStruct(lfr.shape, F32)),
        compiler_params=_cparams(("arbitrary",)), name="cumsum",
    )(lfc, lfr)


ATT_BLK = 256


def _attn_kernel(q_ref, k_ref, v_ref, fc_ref, fr_ref, o_ref):
    h = pl.program_id(1)
    qi = pl.program_id(2)
    t = ATT_BLK
    q = q_ref[0]
    fc = fc_ref[0]
    lane = lax.broadcasted_iota(jnp.int32, fc.shape, 1)
    fq = jnp.sum(jnp.where(lane == h, fc, 0.0), axis=1, keepdims=True)

    def step(j, carry, masked):
        m, l, acc = carry
        start = pl.multiple_of(j * t, t)
        kj = k_ref[0, pl.ds(start, t), :]
        vj = v_ref[0, pl.ds(start, t), :]
        s = lax.dot_general(q, kj, (((1,), (1,)), ((), ())), preferred_element_type=F32)
        fk = fr_ref[0, pl.ds(h, 1), pl.ds(start, t)]
        s = s + fq - fk
        if masked:
            row = lax.broadcasted_iota(jnp.int32, (t, t), 0)
            col = lax.broadcasted_iota(jnp.int32, (t, t), 1)
            s = jnp.where(col <= row, s, -jnp.inf)
        m_new = jnp.maximum(m, jnp.max(s, axis=1, keepdims=True))
        alpha = jnp.exp(m - m_new)
        p = jnp.exp(s - m_new)
        l = alpha * l + jnp.sum(p, axis=1, keepdims=True)
        acc = alpha * acc + jnp.dot(p.astype(BF16), vj, preferred_element_type=F32)
        return m_new, l, acc

    init = (jnp.full((t, 1), -jnp.inf, F32), jnp.zeros((t, 1), F32), jnp.zeros((t, HEAD_DIM), F32))
    carry = lax.fori_loop(0, qi, functools.partial(step, masked=False), init)
    m, l, acc = step(qi, carry, True)
    o_ref[0] = (acc / l).astype(BF16)


def _attn(q, k, v, fc, fr):
    nb, s_len, _ = q.shape
    t = ATT_BLK
    qspec = pl.BlockSpec((1, t, HEAD_DIM), lambda b, h, i: (b, i, h))
    kvspec = pl.BlockSpec((1, s_len, HEAD_DIM), lambda b, h, i: (b, 0, h))
    return pl.pallas_call(
        _attn_kernel,
        grid=(nb, N_HEADS, s_len // t),
        in_specs=[qspec, kvspec, kvspec,
                  pl.BlockSpec((1, t, N_HEADS), lambda b, h, i: (b, i, 0)),
                  pl.BlockSpec((1, N_HEADS, s_len), lambda b, h, i: (b, 0, 0))],
        out_specs=qspec,
        out_shape=jax.ShapeDtypeStruct((nb, s_len, ATT_WIDTH), BF16),
        compiler_params=_cparams(("arbitrary", "arbitrary", "arbitrary")),
        name="attn",
    )(q, k, v, fc, fr)


def _attn_s_kernel(pt_ref, q_ref, kn_ref, vn_ref, lfn_ref, kp_ref, vp_ref, lfp_ref, o_ref,
                   qbd_scr, fq_scr, m_scr, l_scr, acc_scr, carry_scr):
    i = pl.program_id(1)
    n_q = q_ref.shape[1]
    rows = n_q * N_HEADS
    row_head = lax.broadcasted_iota(jnp.int32, (N_HEADS, ATT_WIDTH), 0)
    lane_head = lax.broadcasted_iota(jnp.int32, (N_HEADS, ATT_WIDTH), 1) // HEAD_DIM
    head_mask = row_head == lane_head

    def softmax_update(s, v_bf16, first):
        m_old = m_scr[...]
        m_new = jnp.max(s, axis=1, keepdims=True)
        if not first:
            m_new = jnp.maximum(m_old, m_new)
        p = jnp.exp(s - m_new)
        pv = jnp.dot(p.astype(BF16), v_bf16, preferred_element_type=F32)
        if first:
            l_scr[...] = jnp.sum(p, axis=1, keepdims=True)
            acc_scr[...] = pv
        else:
            alpha = jnp.exp(m_old - m_new)
            l_scr[...] = alpha * l_scr[...] + jnp.sum(p, axis=1, keepdims=True)
            acc_scr[...] = alpha * acc_scr[...] + pv
        m_scr[...] = m_new

    @pl.when(i == 0)
    def _():
        q = q_ref[0]
        qbd = jnp.concatenate(
            [jnp.where(head_mask, jnp.broadcast_to(q[t:t + 1, :], (N_HEADS, ATT_WIDTH)), 0.0)
             for t in range(n_q)], axis=0)
        qbd_scr[...] = qbd.astype(BF16)
        lf = lfn_ref[0]
        lane = lax.broadcasted_iota(jnp.int32, lf.shape, 1)
        f_new = jnp.zeros_like(lf)
        for s_pos in range(n_q):
            col = jnp.sum(jnp.where(lane <= s_pos, lf, 0.0), axis=1, keepdims=True)
            f_new = jnp.where(lane == s_pos, col, f_new)
        fq = jnp.concatenate(
            [jnp.sum(jnp.where(lane == t, f_new, 0.0), axis=1, keepdims=True) for t in range(n_q)],
            axis=0)
        fq_scr[...] = fq
        pad = jnp.zeros((PAGE - n_q, ATT_WIDTH), BF16)
        kn = jnp.concatenate([kn_ref[0].astype(BF16), pad], axis=0)
        vn = jnp.concatenate([vn_ref[0].astype(BF16), pad], axis=0)
        s = lax.dot_general(qbd_scr[...], kn, (((1,), (1,)), ((), ())), preferred_element_type=F32)
        s = s + fq - jnp.concatenate([f_new] * n_q, axis=0)
        r_tok = lax.broadcasted_iota(jnp.int32, (rows, PAGE), 0) // N_HEADS
        c_pos = lax.broadcasted_iota(jnp.int32, (rows, PAGE), 1)
        s = jnp.where(c_pos <= r_tok, s, -jnp.inf)
        softmax_update(s, vn, True)
        carry_scr[...] = jnp.zeros_like(carry_scr)

    lf = lfp_ref[0]
    jj = lax.broadcasted_iota(jnp.int32, (PAGE, PAGE), 0)
    ss = lax.broadcasted_iota(jnp.int32, (PAGE, PAGE), 1)
    after = (jj > ss).astype(F32)
    dec = jnp.dot(lf, after, precision=lax.Precision.HIGHEST, preferred_element_type=F32) + carry_scr[...]
    carry_scr[...] = carry_scr[...] + jnp.sum(lf, axis=1, keepdims=True)
    kp = kp_ref[0].astype(BF16)
    s = lax.dot_general(qbd_scr[...], kp, (((1,), (1,)), ((), ())), preferred_element_type=F32)
    s = s + fq_scr[...] + jnp.concatenate([dec] * n_q, axis=0)
    softmax_update(s, vp_ref[0].astype(BF16), False)

    @pl.when(i == pl.num_programs(1) - 1)
    def _():
        out = acc_scr[...] / l_scr[...]
        o_ref[0] = jnp.concatenate(
            [jnp.sum(jnp.where(head_mask, out[t * N_HEADS:(t + 1) * N_HEADS, :], 0.0), axis=0, keepdims=True)
             for t in range(n_q)], axis=0)


def _attn_s(page_table, q, k_new, v_new, lf_new_row, cache_k, cache_v, cache_lf_row):
    nb, n_q, _ = q.shape
    n_pages = page_table.shape[1]
    rows = n_q * N_HEADS
    seq = lambda b, i, pt: (b, 0, 0)
    page = lambda b, i, pt: (pt[b, n_pages - 1 - i], 0, 0)
    grid_spec = pltpu.PrefetchScalarGridSpec(
        num_scalar_prefetch=1,
        grid=(nb, n_pages),
        in_specs=[pl.BlockSpec((1, n_q, ATT_WIDTH), seq),
                  pl.BlockSpec((1, n_q, ATT_WIDTH), seq),
                  pl.BlockSpec((1, n_q, ATT_WIDTH), seq),
                  pl.BlockSpec((1, N_HEADS, PAGE), seq),
                  pl.BlockSpec((1, PAGE, ATT_WIDTH), page),
                  pl.BlockSpec((1, PAGE, ATT_WIDTH), page),
                  pl.BlockSpec((1, N_HEADS, PAGE), page)],
        out_specs=pl.BlockSpec((1, n_q, ATT_WIDTH), seq),
        scratch_shapes=[pltpu.VMEM((rows, ATT_WIDTH), BF16),
                        pltpu.VMEM((rows, 1), F32),
                        pltpu.VMEM((rows, 1), F32),
                        pltpu.VMEM((rows, 1), F32),
                        pltpu.VMEM((rows, ATT_WIDTH), F32),
                        pltpu.VMEM((N_HEADS, 1), F32)])
    return pl.pallas_call(
        _attn_s_kernel, grid_spec=grid_spec,
        out_shape=jax.ShapeDtypeStruct((nb, n_q, ATT_WIDTH), F32),
        compiler_params=_cparams(("arbitrary", "arbitrary")), name="attn_s",
    )(page_table, q, k_new, v_new, lf_new_row, cache_k, cache_v, cache_lf_row)


N_SEG = 8
S5_CHUNK = 512


def _cmul(ar, ai, br, bi):
    return ar * br - ai * bi, ar * bi + ai * br


def _s5_tail(h, u, wc, dsk, wg, bg):
    y = jnp.dot(h.astype(BF16), wc, preferred_element_type=F32) + dsk * u
    z = jax.nn.gelu(y)
    zz = jnp.dot(z.astype(BF16), wg, preferred_element_type=F32) + bg
    return zz[:, :CH_LANES] * jax.nn.sigmoid(zz[:, CH_LANES:])


def _s5_kernel(u_ref, wb_ref, wc_ref, wg_ref, bg_ref, dsk_ref, lam_ref, y_ref, hfin_ref, hbuf):
    s_len = u_ref.shape[1]
    seg = s_len // N_SEG
    n_blk = 2 * ST_LANES // 128
    wb = wb_ref[0]
    for c0 in range(0, s_len, S5_CHUNK):
        bu = jnp.dot(u_ref[0, c0:c0 + S5_CHUNK, :].astype(BF16), wb, preferred_element_type=F32)
        for kb in range(n_blk):
            hbuf[kb, c0:c0 + S5_CHUNK, :] = bu[:, kb * 128:(kb + 1) * 128]
    lam = lam_ref[0]
    l_re = jnp.broadcast_to(lam[:, :ST_LANES], (N_SEG, ST_LANES))
    l_im = jnp.broadcast_to(lam[:, ST_LANES:], (N_SEG, ST_LANES))

    def advance(t, h_re, h_im):
        row = jnp.concatenate([hbuf[kb, pl.ds(t, N_SEG, stride=seg), :] for kb in range(n_blk)], axis=1)
        n_re, n_im = _cmul(l_re, l_im, h_re, h_im)
        return n_re + row[:, :ST_LANES], n_im + row[:, ST_LANES:]

    def pass1(t, carry):
        return advance(t, *carry)

    zero = jnp.zeros((N_SEG, ST_LANES), F32)
    f_re, f_im = lax.fori_loop(0, seg, pass1, (zero, zero), unroll=4)
    p_re, p_im = lam[:, :ST_LANES], lam[:, ST_LANES:]
    n_sq = seg.bit_length() - 1
    assert 1 << n_sq == seg
    for _ in range(n_sq):
        p_re, p_im = _cmul(p_re, p_im, p_re, p_im)
    c_re, c_im = [jnp.zeros((1, ST_LANES), F32)], [jnp.zeros((1, ST_LANES), F32)]
    for i in range(1, N_SEG):
        d_re, d_im = _cmul(p_re, p_im, c_re[-1], c_im[-1])
        c_re.append(f_re[i - 1:i, :] + d_re)
        c_im.append(f_im[i - 1:i, :] + d_im)

    def pass2(t, carry):
        h_re, h_im = advance(t, *carry)
        h_all = jnp.concatenate([h_re, h_im], axis=1)
        for kb in range(n_blk):
            hbuf[kb, pl.ds(t, N_SEG, stride=seg), :] = h_all[:, kb * 128:(kb + 1) * 128]
        return h_re, h_im

    h_re, h_im = lax.fori_loop(0, seg, pass2,
                               (jnp.concatenate(c_re, axis=0), jnp.concatenate(c_im, axis=0)), unroll=4)
    hfin_ref[0, 0] = jnp.concatenate([h_re[N_SEG - 1:, :], h_im[N_SEG - 1:, :]], axis=1)
    wc, wg, bg, dsk = wc_ref[0], wg_ref[0], bg_ref[0], dsk_ref[0]
    for c0 in range(0, s_len, S5_CHUNK):
        rows = slice(c0, c0 + S5_CHUNK)
        h = jnp.concatenate([hbuf[kb, rows, :] for kb in range(n_blk)], axis=1)
        y_ref[0, rows, :] = _s5_tail(h, u_ref[0, rows, :], wc, dsk, wg, bg).astype(BF16)


def _s5(u, wb, wc, wg, bg, dsk, lam):
    nb, s_len, _ = u.shape
    blk = lambda shape: pl.BlockSpec((1,) + shape, lambda b, g: (g, 0, 0))
    y, hfin = pl.pallas_call(
        _s5_kernel,
        grid=(nb, N_GBLK),
        in_specs=[pl.BlockSpec((1, s_len, CH_LANES), lambda b, g: (b, 0, g)),
                  blk((CH_LANES, 2 * ST_LANES)), blk((2 * ST_LANES, CH_LANES)),
                  blk((CH_LANES, 2 * CH_LANES)), blk((1, 2 * CH_LANES)), blk((1, CH_LANES)),
                  blk((1, 2 * ST_LANES))],
        out_specs=[pl.BlockSpec((1, s_len, CH_LANES), lambda b, g: (b, 0, g)),
                   pl.BlockSpec((1, 1, 1, 2 * ST_LANES), lambda b, g: (b, g, 0, 0))],
        out_shape=(jax.ShapeDtypeStruct((nb, s_len, SSM_WIDTH), BF16),
                   jax.ShapeDtypeStruct((nb, N_GBLK, 1, 2 * ST_LANES), F32)),
        scratch_shapes=[pltpu.VMEM((2 * ST_LANES // 128, s_len, 128), F32)],
        compiler_params=_cparams(("arbitrary", "arbitrary")),
        name="s5",
    )(u, wb.astype(BF16), wc.astype(BF16), wg.astype(BF16), bg, dsk, lam)
    hfin = hfin.reshape(nb, N_GBLK, 2, GROUPS_PER_BLOCK, SSM_STATE)
    return y, hfin[:, :, 0].reshape(nb, N_GROUPS, SSM_STATE), hfin[:, :, 1].reshape(nb, N_GROUPS, SSM_STATE)


def _s5s_kernel(u_ref, h0_ref, wb_ref, wc_ref, wg_ref, bg_ref, dsk_ref, lam_ref, y_ref, hfin_ref, *, nb):
    hi = lax.Precision.HIGHEST
    u = u_ref[...]
    n_t = u.shape[0] // nb
    bu = jnp.dot(u, wb_ref[0], precision=hi, preferred_element_type=F32)
    lam = lam_ref[0]
    l_re = jnp.broadcast_to(lam[:, :ST_LANES], (nb, ST_LANES))
    l_im = jnp.broadcast_to(lam[:, ST_LANES:], (nb, ST_LANES))
    h0 = h0_ref[0]
    h_re, h_im = h0[:, :ST_LANES], h0[:, ST_LANES:]
    hs = []
    for t in range(n_t):
        n_re, n_im = _cmul(l_re, l_im, h_re, h_im)
        h_re = n_re + bu[t * nb:(t + 1) * nb, :ST_LANES]
        h_im = n_im + bu[t * nb:(t + 1) * nb, ST_LANES:]
        hs.append(jnp.concatenate([h_re, h_im], axis=1))
    hfin_ref[0] = hs[-1]
    h = jnp.concatenate(hs, axis=0)
    y = jnp.dot(h, wc_ref[0], precision=hi, preferred_element_type=F32) + dsk_ref[0] * u
    z = jax.nn.gelu(y)
    zz = jnp.dot(z, wg_ref[0], precision=hi, preferred_element_type=F32) + bg_ref[0]
    y_ref[...] = (zz[:, :CH_LANES] * jax.nn.sigmoid(zz[:, CH_LANES:])).astype(BF16)


def _s5s(u_tm, h0, wb, wc, wg, bg, dsk, lam, nb):
    rows = u_tm.shape[0]
    blk = lambda shape: pl.BlockSpec((1,) + shape, lambda g: (g, 0, 0))
    return pl.pallas_call(
        functools.partial(_s5s_kernel, nb=nb),
        grid=(N_GBLK,),
        in_specs=[pl.BlockSpec((rows, CH_LANES), lambda g: (0, g)),
                  blk((nb, 2 * ST_LANES)),
                  blk((CH_LANES, 2 * ST_LANES)), blk((2 * ST_LANES, CH_LANES)),
                  blk((CH_LANES, 2 * CH_LANES)), blk((1, 2 * CH_LANES)), blk((1, CH_LANES)),
                  blk((1, 2 * ST_LANES))],
        out_specs=[pl.BlockSpec((rows, CH_LANES), lambda g: (0, g)), blk((nb, 2 * ST_LANES))],
        out_shape=(jax.ShapeDtypeStruct((rows, SSM_WIDTH), BF16),
                   jax.ShapeDtypeStruct((N_GBLK, nb, 2 * ST_LANES), F32)),
        compiler_params=_cparams(("arbitrary",)),
        name="s5s",
    )(u_tm, h0, wb, wc, wg, bg, dsk, lam)


def _outproj_kernel(att_ref, ssm_ref, x_ref, g1_ref, sc2_ref, sh2_ref, lng_ref, lnb_ref, wo_ref,
                    x1_ref, h2_ref):
    mix = (jnp.dot(att_ref[0], wo_ref[:ATT_WIDTH, :], preferred_element_type=F32)
           + jnp.dot(ssm_ref[0], wo_ref[ATT_WIDTH:, :], preferred_element_type=F32))
    x1 = _ln(ALPHA * x_ref[0] + (1.0 + g1_ref[0]) * mix) * lng_ref[...] + lnb_ref[...]
    x1_ref[0] = x1
    h2_ref[0] = (_ln(x1) * (1.0 + sc2_ref[0]) + sh2_ref[0]).astype(BF16)


def _outproj(att, ssm, x, gate1, scale2, shift2, ln_g, ln_b, w_o, tm):
    nb, l, _ = x.shape
    r = gate1.shape[1]
    mod_map = (lambda b, i: (b, 0, 0)) if r == 1 else (lambda b, i: (b, i, 0))
    row_map = lambda b, i: (b, i, 0)
    const2 = lambda b, i: (0, 0)
    half = pl.BlockSpec((1, tm, ATT_WIDTH), row_map)
    full = pl.BlockSpec((1, tm, D_MODEL), row_map)
    mod = pl.BlockSpec((1, r, D_MODEL), mod_map)
    vec = pl.BlockSpec((1, D_MODEL), const2)
    return pl.pallas_call(
        _outproj_kernel,
        grid=(nb, l // tm),
        in_specs=[half, half, full, mod, mod, mod, vec, vec,
                  pl.BlockSpec((D_MODEL, D_MODEL), const2)],
        out_specs=[full, full],
        out_shape=(jax.ShapeDtypeStruct((nb, l, D_MODEL), F32),
                   jax.ShapeDtypeStruct((nb, l, D_MODEL), BF16)),
        compiler_params=_cparams(("arbitrary", "arbitrary")),
        name="outproj",
    )(att, ssm, x, gate1, scale2, shift2, ln_g.reshape(1, D_MODEL), ln_b.reshape(1, D_MODEL), w_o)


FFN_TF = 512


def _ffn_kernel(h2_ref, x1_ref, g2_ref, lng_ref, lnb_ref, wu_ref, wd_ref, y_ref, acc_scr):
    f = pl.program_id(2)
    up = jnp.dot(h2_ref[0], wu_ref[...], preferred_element_type=F32)
    act = jnp.square(jnp.maximum(up, 0.0)).astype(BF16)
    part = jnp.dot(act, wd_ref[...], preferred_element_type=F32)

    @pl.when(f == 0)
    def _():
        acc_scr[...] = part

    @pl.when(f > 0)
    def _():
        acc_scr[...] += part

    @pl.when(f == pl.num_programs(2) - 1)
    def _():
        y_ref[0] = _ln(ALPHA * x1_ref[0] + (1.0 + g2_ref[0]) * acc_scr[...]) * lng_ref[...] + lnb_ref[...]


def _ffn(h2, x1, gate2, ln_g, ln_b, w_up, w_down, tm):
    nb, l, _ = x1.shape
    r = gate2.shape[1]
    mod_map = (lambda b, i, f: (b, 0, 0)) if r == 1 else (lambda b, i, f: (b, i, 0))
    row_map = lambda b, i, f: (b, i, 0)
    const2 = lambda b, i, f: (0, 0)
    full = pl.BlockSpec((1, tm, D_MODEL), row_map)
    vec = pl.BlockSpec((1, D_MODEL), const2)
    return pl.pallas_call(
        _ffn_kernel,
        grid=(nb, l // tm, D_FF // FFN_TF),
        in_specs=[full, full, pl.BlockSpec((1, r, D_MODEL), mod_map), vec, vec,
                  pl.BlockSpec((D_MODEL, FFN_TF), lambda b, i, f: (0, f)),
                  pl.BlockSpec((FFN_TF, D_MODEL), lambda b, i, f: (f, 0))],
        out_specs=full,
        out_shape=jax.ShapeDtypeStruct((nb, l, D_MODEL), F32),
        scratch_shapes=[pltpu.VMEM((tm, D_MODEL), F32)],
        compiler_params=_cparams(("arbitrary", "arbitrary", "arbitrary")),
        name="ffn",
    )(h2, x1, gate2, ln_g.reshape(1, D_MODEL), ln_b.reshape(1, D_MODEL), w_up, w_down)


def kernel(x_prompt, x_sample, c_prompt, c_sample, cache_k, cache_v, cache_logf, state_ssm_re,
           state_ssm_im, page_table, w_ada, b_ada, w_in, b_f, w_o, a_re, a_im, log_dt, b_re, b_im,
           c_re, c_im, d_skip, w_glu, b_glu, ln1_g, ln1_b, w_up, w_down, ln2_g, ln2_b):
    assert w_ada.shape[0] == DEPTH == 1
    nbp, s_len, _ = x_prompt.shape
    nbs, n_q, _ = x_sample.shape
    n_seq = nbp + nbs

    c_all = jnp.concatenate([c_prompt, c_sample, jnp.zeros((16 - n_seq, D_MODEL), F32)], axis=0)
    mod = _ada(c_all, w_ada[0], b_ada[0]).reshape(16, N_MOD, D_MODEL)
    mod_p = [mod[:nbp, i][:, None, :] for i in range(N_MOD)]
    mod_s = [jnp.repeat(mod[nbp:n_seq, i], n_q, axis=0)[None] for i in range(N_MOD)]

    a = ATT_WIDTH
    w_in0 = w_in[0]
    w_main = jnp.concatenate([w_in0[:, :3 * a], w_in0[:, 3 * a + N_HEADS:]], axis=1).astype(BF16)
    w_fcols = w_in0[:, 3 * a:3 * a + N_HEADS]
    w_f = jnp.pad(w_fcols, ((0, 0), (0, 128 - N_HEADS))).astype(BF16)
    w_ft = w_fcols.T.astype(BF16)
    w_o_b = w_o[0].astype(BF16)
    w_up_b = w_up[0].astype(BF16)
    w_down_b = w_down[0].astype(BF16)

    l_re, l_im, bb_re, bb_im = _s5prep(a_re[0], a_im[0], log_dt[0], b_re[0], b_im[0])
    wb, wc, wg, bg, dsk, lam = _s5_weights(l_re, l_im, bb_re, bb_im, c_re[0], c_im[0], d_skip[0],
                                           w_glu[0], b_glu[0])

    q, k, k_b, v, v_b, u, lfc, lfr = _inproj(x_prompt, mod_p[0], mod_p[1], w_main, w_f, w_ft, b_f[0], tm=512)
    fc, fr = _cumsum(lfc, lfr)
    att = _attn(q, k_b, v_b, fc, fr)
    ssm, hp_re, hp_im = _s5(u, wb, wc, wg, bg, dsk, lam)
    x1, h2 = _outproj(att, ssm, x_prompt, mod_p[2], mod_p[4], mod_p[3], ln1_g[0], ln1_b[0], w_o_b, tm=512)
    y_p = _ffn(h2, x1, mod_p[5], ln2_g[0], ln2_b[0], w_up_b, w_down_b, tm=512)

    rows = nbs * n_q
    xs = x_sample.reshape(1, rows, D_MODEL)
    qs, ks, _, vs, _, us, lfcs, lfrs = _inproj(xs, mod_s[0], mod_s[1], w_main, w_f, w_ft, b_f[0], tm=rows)
    lf_new_row = jnp.pad(lfrs.reshape(N_HEADS, nbs, n_q).transpose(1, 0, 2),
                         ((0, 0), (0, 0), (0, PAGE - n_q)))
    n_pool = cache_k.shape[1]
    att_s = _attn_s(page_table,
                    qs.reshape(nbs, n_q, a).astype(F32), ks.reshape(nbs, n_q, a), vs.reshape(nbs, n_q, a),
                    lf_new_row,
                    cache_k[0].reshape(n_pool, PAGE, a), cache_v[0].reshape(n_pool, PAGE, a),
                    jnp.swapaxes(cache_logf[0], 1, 2))
    u_tm = us.reshape(nbs, n_q, SSM_WIDTH).transpose(1, 0, 2).reshape(rows, SSM_WIDTH)
    h0 = jnp.concatenate([state_ssm_re[0].reshape(nbs, N_GBLK, ST_LANES),
                          state_ssm_im[0].reshape(nbs, N_GBLK, ST_LANES)], axis=2).transpose(1, 0, 2)
    ssm_tm, hs_fin = _s5s(u_tm, h0, wb, wc, wg, bg, dsk, lam, nbs)
    ssm_s = ssm_tm.reshape(n_q, nbs, SSM_WIDTH).transpose(1, 0, 2).reshape(1, rows, SSM_WIDTH)
    hs_fin = hs_fin.transpose(1, 0, 2).reshape(nbs, N_GBLK, 2, GROUPS_PER_BLOCK, SSM_STATE)
    hs_re = hs_fin[:, :, 0].reshape(nbs, N_GROUPS, SSM_STATE)
    hs_im = hs_fin[:, :, 1].reshape(nbs, N_GROUPS, SSM_STATE)
    x1s, h2s = _outproj(att_s.reshape(1, rows, a).astype(BF16), ssm_s, xs, mod_s[2], mod_s[4], mod_s[3],
                        ln1_g[0], ln1_b[0], w_o_b, tm=rows)
    y_s = _ffn(h2s, x1s, mod_s[5], ln2_g[0], ln2_b[0], w_up_b, w_down_b, tm=rows)

    hd = (N_HEADS, HEAD_DIM)
    return (y_p, y_s.reshape(nbs, n_q, D_MODEL),
            k.reshape(1, nbp, s_len, *hd), v.reshape(1, nbp, s_len, *hd), lfc[None],
            hp_re[None], hp_im[None],
            ks.reshape(1, nbs, n_q, *hd), vs.reshape(1, nbs, n_q, *hd), lfcs.reshape(1, nbs, n_q, N_HEADS),
            hs_re[None], hs_im[None])
```

```python
import functools
import math

import jax
import jax.numpy as jnp
from jax import lax
from jax.experimental import pallas as pl
from jax.experimental.pallas import tpu as pltpu

F32 = jnp.float32
BF16 = jnp.bfloat16

LANE = 128
D_MODEL = 2048
ATT_WIDTH = 1024
SSM_WIDTH = 1024
HEAD_DIM = 128
N_HEADS = 8
SSM_GROUP = 16
N_GROUPS = 64
SSM_STATE = 64
D_FF = 8192
N_MOD = 6
PAGE = 128
DEPTH = 1
ALPHA = (2 * DEPTH) ** 0.25
LN_EPS = 1e-5
LOG2E = math.log2(math.e)
QSCALE = HEAD_DIM ** -0.5 * LOG2E

GROUPS_PER_BLOCK = 8
N_GBLK = N_GROUPS // GROUPS_PER_BLOCK
ST_LANES = GROUPS_PER_BLOCK * SSM_STATE
CH_LANES = GROUPS_PER_BLOCK * SSM_GROUP

VMEM_LIMIT = 48 * 1024 * 1024


def _cparams(sem):
    return pltpu.CompilerParams(dimension_semantics=sem, vmem_limit_bytes=VMEM_LIMIT)


def _ln(x):
    mu = jnp.mean(x, axis=-1, keepdims=True)
    xc = x - mu
    var = jnp.mean(xc * xc, axis=-1, keepdims=True)
    return xc * lax.rsqrt(var + LN_EPS)


def _log_sigmoid(x):
    return jnp.minimum(x, 0.0) - jnp.log1p(jnp.exp(-jnp.abs(x)))


def _nt_dot(a, b):
    return lax.dot_general(a, b, (((1,), (1,)), ((), ())), preferred_element_type=F32)


def _softmax_step(carry, s, v):
    m, l, acc = carry
    m_new = jnp.maximum(m, jnp.max(s, axis=1, keepdims=True))
    alpha = jnp.exp2(m - m_new)
    p = jnp.exp2(s - m_new)
    l = alpha * l + jnp.sum(p, axis=1, keepdims=True)
    acc = alpha * acc + jnp.dot(p.astype(BF16), v, preferred_element_type=F32)
    return m_new, l, acc


def _ada_kernel(c_ref, w_ref, b_ref, o_ref):
    c = c_ref[...]
    s = c * jax.nn.sigmoid(c)
    o_ref[...] = jnp.dot(s, w_ref[...], preferred_element_type=F32) + b_ref[...]


def _ada(c_all, w_ada, b_ada):
    rows = c_all.shape[0]
    n = w_ada.shape[1]
    tn = 1024
    return pl.pallas_call(
        _ada_kernel,
        grid=(n // tn,),
        in_specs=[pl.BlockSpec((rows, D_MODEL), lambda j: (0, 0)),
                  pl.BlockSpec((D_MODEL, tn), lambda j: (0, j)),
                  pl.BlockSpec((1, tn), lambda j: (0, j))],
        out_specs=pl.BlockSpec((rows, tn), lambda j: (0, j)),
        out_shape=jax.ShapeDtypeStruct((rows, n), F32),
        compiler_params=_cparams(("arbitrary",)),
        name="ada",
    )(c_all, w_ada, b_ada.reshape(1, n))


def _s5prep_kernel(are_ref, aim_ref, ldt_ref, arx_ref, aix_ref, ldx_ref, bre_ref, bim_ref,
                   lre_ref, lim_ref, bbre_ref, bbim_ref):
    def lam(a_re, a_im, log_dt):
        dt = jnp.exp(log_dt)
        mag = jnp.exp(a_re * dt)
        return mag * jnp.cos(a_im * dt), mag * jnp.sin(a_im * dt)

    l_re, l_im = lam(are_ref[...], aim_ref[...], ldt_ref[...])
    lre_ref[...] = l_re
    lim_ref[...] = l_im
    a_re, a_im = arx_ref[...], aix_ref[...]
    x_re, x_im = lam(a_re, a_im, ldx_ref[...])
    den = a_re * a_re + a_im * a_im
    n_re = x_re - 1.0
    k_re = (n_re * a_re + x_im * a_im) / den
    k_im = (x_im * a_re - n_re * a_im) / den
    b_re, b_im = bre_ref[...], bim_ref[...]
    bbre_ref[...] = k_re * b_re - k_im * b_im
    bbim_ref[...] = k_re * b_im + k_im * b_re


def _s5prep(a_re, a_im, log_dt, b_re, b_im):
    g, p, c = b_re.shape
    ldt = jnp.broadcast_to(log_dt[:, None], (g, p))
    ex = lambda a: jnp.broadcast_to(a[:, :, None], (g, p, c)).reshape(g, p * c)
    small = jax.ShapeDtypeStruct((g, p), F32)
    big = jax.ShapeDtypeStruct((g, p * c), F32)
    l_re, l_im, bb_re, bb_im = pl.pallas_call(
        _s5prep_kernel, out_shape=(small, small, big, big), name="s5prep",
    )(a_re, a_im, ldt, ex(a_re), ex(a_im), ex(ldt), b_re.reshape(g, p * c), b_im.reshape(g, p * c))
    return l_re, l_im, bb_re.reshape(g, p, c), bb_im.reshape(g, p, c)


def _blockdiag(w):
    g, a, b = w.shape
    w = w.reshape(N_GBLK, GROUPS_PER_BLOCK, a, b)
    eye = jnp.eye(GROUPS_PER_BLOCK, dtype=w.dtype)
    return jnp.einsum('xgab,gh->xgahb', w, eye).reshape(N_GBLK, GROUPS_PER_BLOCK * a, GROUPS_PER_BLOCK * b)


def _s5_weights(l_re, l_im, bb_re, bb_im, c_re, c_im, d_skip, w_glu, b_glu):
    wb = jnp.concatenate([_blockdiag(jnp.swapaxes(bb_re, 1, 2)),
                          _blockdiag(jnp.swapaxes(bb_im, 1, 2))], axis=2)
    wc = jnp.concatenate([_blockdiag(jnp.swapaxes(c_re, 1, 2)),
                          _blockdiag(-jnp.swapaxes(c_im, 1, 2))], axis=1)
    wg = jnp.concatenate([_blockdiag(w_glu[:, :, :SSM_GROUP]),
                          _blockdiag(w_glu[:, :, SSM_GROUP:])], axis=2)
    bg = jnp.concatenate([b_glu[:, :SSM_GROUP].reshape(N_GBLK, 1, CH_LANES),
                          b_glu[:, SSM_GROUP:].reshape(N_GBLK, 1, CH_LANES)], axis=2)
    dsk = d_skip.reshape(N_GBLK, 1, CH_LANES)
    lam = jnp.concatenate([l_re.reshape(N_GBLK, 1, ST_LANES), l_im.reshape(N_GBLK, 1, ST_LANES)], axis=2)
    return wb, wc, wg, bg, dsk, lam


def _inproj_kernel(x_ref, sh_ref, sc_ref, w_ref, wf_ref, wft_ref, bf_ref, bfr_ref,
                   q_ref, kf_ref, kb_ref, vf_ref, vb_ref, u_ref, lfc_ref, lfr_ref, h_scr):
    j = pl.program_id(2)

    @pl.when(j == 0)
    def _():
        h = _ln(x_ref[0]) * (1.0 + sc_ref[0]) + sh_ref[0]
        hb = h.astype(BF16)
        h_scr[...] = hb
        fl = jnp.dot(hb, wf_ref[...], preferred_element_type=F32) + bf_ref[...]
        lfc_ref[0] = _log_sigmoid(fl)[:, :N_HEADS]
        lfr_ref[0] = _log_sigmoid(_nt_dot(wft_ref[...], hb) + bfr_ref[...])

    acc = jnp.dot(h_scr[...], w_ref[...], preferred_element_type=F32)

    @pl.when(j == 0)
    def _():
        q_ref[0] = (acc * QSCALE).astype(BF16)

    @pl.when(j == 1)
    def _():
        kf_ref[0] = acc
        kb_ref[0] = acc.astype(BF16)

    @pl.when(j == 2)
    def _():
        vf_ref[0] = acc
        vb_ref[0] = acc.astype(BF16)

    @pl.when(j == 3)
    def _():
        u_ref[0] = acc


def _inproj(x, shift, scale, w_main, w_f, w_ft, b_f, tm):
    nb, l, _ = x.shape
    r = shift.shape[1]
    mod_map = (lambda b, i, j: (b, 0, 0)) if r == 1 else (lambda b, i, j: (b, i, 0))
    row_map = lambda b, i, j: (b, i, 0)
    const2 = lambda b, i, j: (0, 0)
    wide = lambda dt: jax.ShapeDtypeStruct((nb, l, ATT_WIDTH), dt)
    return pl.pallas_call(
        _inproj_kernel,
        grid=(nb, l // tm, 4),
        in_specs=[pl.BlockSpec((1, tm, D_MODEL), row_map),
                  pl.BlockSpec((1, r, D_MODEL), mod_map),
                  pl.BlockSpec((1, r, D_MODEL), mod_map),
                  pl.BlockSpec((D_MODEL, ATT_WIDTH), lambda b, i, j: (0, j)),
                  pl.BlockSpec((D_MODEL, LANE), const2),
                  pl.BlockSpec((N_HEADS, D_MODEL), const2),
                  pl.BlockSpec((1, LANE), const2),
                  pl.BlockSpec((N_HEADS, 1), const2)],
        out_specs=[pl.BlockSpec((1, tm, ATT_WIDTH), row_map)] * 6
                  + [pl.BlockSpec((1, tm, N_HEADS), row_map),
                     pl.BlockSpec((1, N_HEADS, tm), lambda b, i, j: (b, 0, i))],
        out_shape=(wide(BF16), wide(F32), wide(BF16), wide(F32), wide(BF16), wide(F32),
                   jax.ShapeDtypeStruct((nb, l, N_HEADS), F32),
                   jax.ShapeDtypeStruct((nb, N_HEADS, l), F32)),
        scratch_shapes=[pltpu.VMEM((tm, D_MODEL), BF16)],
        compiler_params=_cparams(("arbitrary", "arbitrary", "arbitrary")),
        name="inproj",
    )(x, shift, scale, w_main, w_f, w_ft, jnp.pad(b_f, (0, LANE - N_HEADS)).reshape(1, LANE),
      b_f.reshape(N_HEADS, 1))


CUM_BLK = 512


def _cumsum_kernel(lfc_ref, lfr_ref, fc_ref, fr_ref):
    s_len = lfc_ref.shape[1]
    r = lax.broadcasted_iota(jnp.int32, (CUM_BLK, CUM_BLK), 0)
    c = lax.broadcasted_iota(jnp.int32, (CUM_BLK, CUM_BLK), 1)
    tri_l = (c <= r).astype(F32)
    tri_u = (r <= c).astype(F32)
    carry_c = jnp.zeros((1, N_HEADS), F32)
    carry_r = jnp.zeros((N_HEADS, 1), F32)
    for blk in range(s_len // CUM_BLK):
        rows = slice(blk * CUM_BLK, (blk + 1) * CUM_BLK)
        fc = jnp.dot(tri_l, lfc_ref[0, rows, :], precision=lax.Precision.HIGHEST,
                     preferred_element_type=F32) + carry_c
        fc_ref[0, rows, :] = fc * LOG2E
        carry_c = fc[CUM_BLK - 1:CUM_BLK, :]
        fr = jnp.dot(lfr_ref[0, :, rows], tri_u, precision=lax.Precision.HIGHEST,
                     preferred_element_type=F32) + carry_r
        fr_ref[0, :, rows] = fr * LOG2E
        carry_r = fr[:, CUM_BLK - 1:CUM_BLK]


def _cumsum(lfc, lfr):
    nb, s_len, _ = lfc.shape
    cspec = pl.BlockSpec((1, s_len, N_HEADS), lambda b: (b, 0, 0))
    rspec = pl.BlockSpec((1, N_HEADS, s_len), lambda b: (b, 0, 0))
    return pl.pallas_call(
        _cumsum_kernel, grid=(nb,), in_specs=[cspec, rspec], out_specs=[cspec, rspec],
        out_shape=(jax.ShapeDtypeStruct(lfc.shape, F32), jax.ShapeDtypeStruct(lfr.shape, F32)),
        compiler_params=_cparams(("arbitrary",)), name="cumsum",
    )(lfc, lfr)


ATT_BLK = 512


def _attn_kernel(q_ref, k_ref, v_ref, fc_ref, fr_ref, o_ref):
    h = pl.program_id(1)
    qi = pl.program_id(2)
    t = ATT_BLK
    q = q_ref[0]
    fc = fc_ref[0]
    lane = lax.broadcasted_iota(jnp.int32, fc.shape, 1)
    fq = jnp.sum(jnp.where(lane == h, fc, 0.0), axis=1, keepdims=True)

    def step(j, carry, masked):
        start = pl.multiple_of(j * t, t)
        kj = k_ref[0, pl.ds(start, t), :]
        vj = v_ref[0, pl.ds(start, t), :]
        fk = fr_ref[0, pl.ds(h, 1), pl.ds(start, t)]
        s = _nt_dot(q, kj) + fq - fk
        if masked:
            row = lax.broadcasted_iota(jnp.int32, (t, t), 0)
            col = lax.broadcasted_iota(jnp.int32, (t, t), 1)
            s = jnp.where(col <= row, s, -jnp.inf)
        return _softmax_step(carry, s, vj)

    init = (jnp.full((t, 1), -jnp.inf, F32), jnp.zeros((t, 1), F32), jnp.zeros((t, HEAD_DIM), F32))
    carry = lax.fori_loop(0, qi, functools.partial(step, masked=False), init)
    m, l, acc = step(qi, carry, True)
    o_ref[0] = (acc / l).astype(BF16)


def _attn(q, k, v, fc, fr):
    nb, s_len, _ = q.shape
    t = ATT_BLK
    qspec = pl.BlockSpec((1, t, HEAD_DIM), lambda b, h, i: (b, i, h))
    kvspec = pl.BlockSpec((1, s_len, HEAD_DIM), lambda b, h, i: (b, 0, h))
    return pl.pallas_call(
        _attn_kernel,
        grid=(nb, N_HEADS, s_len // t),
        in_specs=[qspec, kvspec, kvspec,
                  pl.BlockSpec((1, t, N_HEADS), lambda b, h, i: (b, i, 0)),
                  pl.BlockSpec((1, N_HEADS, s_len), lambda b, h, i: (b, 0, 0))],
        out_specs=qspec,
        out_shape=jax.ShapeDtypeStruct((nb, s_len, ATT_WIDTH), BF16),
        compiler_params=_cparams(("arbitrary", "arbitrary", "arbitrary")),
        name="attn",
    )(q, k, v, fc, fr)


PAGE_LANES = PAGE * N_HEADS


def _decay_kernel(lf_ref, wt_ref):
    x = lf_ref[...]
    n = x.shape[1]
    lane = lax.broadcasted_iota(jnp.int32, x.shape, 1)
    suffix, total = x, x
    k = N_HEADS
    while k < n:
        suffix = suffix + jnp.where(lane < n - k, pltpu.roll(suffix, n - k, axis=1), 0.0)
        total = total + pltpu.roll(total, k, axis=1)
        k *= 2
    wt_ref[:, 0, :] = suffix - x
    wt_ref[:, 1, :] = total


def _decay(lf_flat):
    n_pool, n = lf_flat.shape
    rb = 256
    return pl.pallas_call(
        _decay_kernel,
        grid=(n_pool // rb,),
        in_specs=[pl.BlockSpec((rb, n), lambda i: (i, 0))],
        out_specs=pl.BlockSpec((rb, 2, n), lambda i: (i, 0, 0)),
        out_shape=jax.ShapeDtypeStruct((n_pool, 2, n), F32),
        compiler_params=_cparams(("arbitrary",)),
        name="decay",
    )(lf_flat)


PAGES_PER_STEP = 8


def _attn_s_kernel(pt_ref, q_ref, kn_ref, vn_ref, lfn_ref, *refs):
    n_pg = PAGES_PER_STEP
    kp_refs, vp_refs, wt_refs = refs[:n_pg], refs[n_pg:2 * n_pg], refs[2 * n_pg:3 * n_pg]
    o_ref, bias_scr, m_scr, l_scr, acc_scr, carry_scr = refs[3 * n_pg:]
    i = pl.program_id(1)
    rows = q_ref.shape[1]
    q = q_ref[0]

    @pl.when(i == 0)
    def _():
        f = jnp.broadcast_to(lfn_ref[0], (N_HEADS, LANE))
        lane8 = lax.broadcasted_iota(jnp.int32, f.shape, 1)
        k = N_HEADS
        while k < rows:
            f = f + jnp.where(lane8 >= k, pltpu.roll(f, k, axis=1), 0.0)
            k *= 2
        f_row = f[0:1, :] * LOG2E
        r = lax.broadcasted_iota(jnp.int32, (rows, LANE), 0)
        c = lax.broadcasted_iota(jnp.int32, (rows, LANE), 1)
        fq = jnp.sum(jnp.where(r == c, jnp.broadcast_to(f_row, (rows, LANE)), 0.0), axis=1, keepdims=True)
        rr = lax.broadcasted_iota(jnp.int32, (rows, PAGE_LANES), 0)
        cc = lax.broadcasted_iota(jnp.int32, (rows, PAGE_LANES), 1)
        same_head = jnp.bitwise_and(rr, N_HEADS - 1) == jnp.bitwise_and(cc, N_HEADS - 1)
        bias_scr[...] = jnp.where(same_head, fq, -jnp.inf)
        pad = jnp.zeros((LANE - rows, HEAD_DIM), BF16)
        kn = jnp.concatenate([kn_ref[0].astype(BF16), pad], axis=0)
        vn = jnp.concatenate([vn_ref[0].astype(BF16), pad], axis=0)
        tok_r = jnp.right_shift(r, 3)
        tok_c = jnp.right_shift(c, 3)
        head_ok = jnp.bitwise_and(r, N_HEADS - 1) == jnp.bitwise_and(c, N_HEADS - 1)
        s = _nt_dot(q, kn) + fq - f_row
        s = jnp.where(head_ok, jnp.where(tok_c <= tok_r, s, -jnp.inf), -jnp.inf)
        init = (jnp.full((rows, 1), -jnp.inf, F32), jnp.zeros((rows, 1), F32),
                jnp.zeros((rows, HEAD_DIM), F32))
        m, l, acc = _softmax_step(init, s, vn)
        m_scr[...] = m
        l_scr[...] = l
        acc_scr[...] = acc
        carry_scr[...] = jnp.zeros_like(carry_scr)

    carry = carry_scr[...]
    bias = bias_scr[...]
    scores, values = [], []
    for pg in range(n_pg):
        wt = wt_refs[pg][0]
        dec = (wt[0:1, :] + carry) * LOG2E
        carry = carry + wt[1:2, :]
        k = kp_refs[pg][...].reshape(PAGE_LANES, HEAD_DIM).astype(BF16)
        scores.append(_nt_dot(q, k) + bias + dec)
        values.append(vp_refs[pg][...].reshape(PAGE_LANES, HEAD_DIM).astype(BF16))
    state = _softmax_step((m_scr[...], l_scr[...], acc_scr[...]),
                          jnp.concatenate(scores, axis=1), jnp.concatenate(values, axis=0))
    carry_scr[...] = carry
    m_scr[...], l_scr[...], acc_scr[...] = state

    @pl.when(i == pl.num_programs(1) - 1)
    def _():
        o_ref[0] = state[2] / state[1]


def _attn_s(page_table, q, k_new, v_new, lf_new, cache_k, cache_v, decay_wt):
    nb, rows, _ = q.shape
    n_pages = page_table.shape[1]
    n_pg = PAGES_PER_STEP
    seq = lambda b, i, pt: (b, 0, 0)

    def page_spec(block, n_lead, pg):
        def idx(b, i, pt):
            return (0,) * n_lead + (pt[b, n_pages - 1 - (i * n_pg + pg)],) + (0,) * (len(block) - n_lead - 1)
        return pl.BlockSpec(block, idx)

    kv_block = (None, None, PAGE, N_HEADS, HEAD_DIM)
    grid_spec = pltpu.PrefetchScalarGridSpec(
        num_scalar_prefetch=1,
        grid=(nb, n_pages // n_pg),
        in_specs=[pl.BlockSpec((1, rows, HEAD_DIM), seq),
                  pl.BlockSpec((1, rows, HEAD_DIM), seq),
                  pl.BlockSpec((1, rows, HEAD_DIM), seq),
                  pl.BlockSpec((1, 1, LANE), seq)]
                 + [page_spec(kv_block, 1, pg) for pg in range(n_pg)]
                 + [page_spec(kv_block, 1, pg) for pg in range(n_pg)]
                 + [page_spec((1, 2, PAGE_LANES), 0, pg) for pg in range(n_pg)],
        out_specs=pl.BlockSpec((1, rows, HEAD_DIM), seq),
        scratch_shapes=[pltpu.VMEM((rows, PAGE_LANES), F32),
                        pltpu.VMEM((rows, 1), F32),
                        pltpu.VMEM((rows, 1), F32),
                        pltpu.VMEM((rows, HEAD_DIM), F32),
                        pltpu.VMEM((1, PAGE_LANES), F32)])
    return pl.pallas_call(
        _attn_s_kernel, grid_spec=grid_spec,
        out_shape=jax.ShapeDtypeStruct((nb, rows, HEAD_DIM), F32),
        compiler_params=_cparams(("arbitrary", "arbitrary")), name="attn_s",
    )(page_table, q, k_new, v_new, lf_new, *([cache_k] * n_pg), *([cache_v] * n_pg), *([decay_wt] * n_pg))


N_SEG = 8
S5_TB = 64
S5_PITCH = 520


def _cmul(ar, ai, br, bi):
    return ar * br - ai * bi, ar * bi + ai * br


def _s5_tail(h, u, wc, dsk, wg, bg):
    y = jnp.dot(h.astype(BF16), wc, preferred_element_type=F32) + dsk * u
    z = jax.nn.gelu(y)
    zz = jnp.dot(z.astype(BF16), wg, preferred_element_type=F32) + bg
    return zz[:, :CH_LANES] * jax.nn.sigmoid(zz[:, CH_LANES:])


def _s5_kernel(u_ref, wb_ref, wc_ref, wg_ref, bg_ref, dsk_ref, lam_ref, y_ref, hfin_ref, ubuf, ybuf):
    s_len = u_ref.shape[1]
    seg = s_len // N_SEG
    n_tb = seg // S5_TB
    assert seg <= S5_PITCH and seg % S5_TB == 0
    for i in range(N_SEG):
        ubuf[i * S5_PITCH:i * S5_PITCH + seg, :] = u_ref[0, i * seg:(i + 1) * seg, :]
    wb = wb_ref[0]
    lam = lam_ref[0]
    l_re = jnp.broadcast_to(lam[:, :ST_LANES], (N_SEG, ST_LANES))
    l_im = jnp.broadcast_to(lam[:, ST_LANES:], (N_SEG, ST_LANES))

    def load_block(tb):
        t0 = tb * S5_TB
        u = jnp.concatenate([ubuf[pl.ds(t0 + j, N_SEG, stride=S5_PITCH), :] for j in range(S5_TB)], axis=0)
        return u, jnp.dot(u.astype(BF16), wb, preferred_element_type=F32)

    def scan_block(bu, h_re, h_im, keep):
        hs = []
        for j in range(S5_TB):
            n_re, n_im = _cmul(l_re, l_im, h_re, h_im)
            h_re = n_re + bu[j * N_SEG:(j + 1) * N_SEG, :ST_LANES]
            h_im = n_im + bu[j * N_SEG:(j + 1) * N_SEG, ST_LANES:]
            if keep:
                hs.append(jnp.concatenate([h_re, h_im], axis=1))
        return h_re, h_im, hs

    def pass1(tb, carry):
        _, bu = load_block(tb)
        h_re, h_im, _ = scan_block(bu, *carry, keep=False)
        return h_re, h_im

    zero = jnp.zeros((N_SEG, ST_LANES), F32)
    f_re, f_im = lax.fori_loop(0, n_tb, pass1, (zero, zero))
    p_re, p_im = lam[:, :ST_LANES], lam[:, ST_LANES:]
    n_sq = seg.bit_length() - 1
    assert 1 << n_sq == seg
    for _ in range(n_sq):
        p_re, p_im = _cmul(p_re, p_im, p_re, p_im)
    c_re, c_im = [jnp.zeros((1, ST_LANES), F32)], [jnp.zeros((1, ST_LANES), F32)]
    for i in range(1, N_SEG):
        d_re, d_im = _cmul(p_re, p_im, c_re[-1], c_im[-1])
        c_re.append(f_re[i - 1:i, :] + d_re)
        c_im.append(f_im[i - 1:i, :] + d_im)
    wc, wg, bg, dsk = wc_ref[0], wg_ref[0], bg_ref[0], dsk_ref[0]

    def pass2(tb, carry):
        u, bu = load_block(tb)
        h_re, h_im, hs = scan_block(bu, *carry, keep=True)
        y = _s5_tail(jnp.concatenate(hs, axis=0), u, wc, dsk, wg, bg)
        t0 = tb * S5_TB
        for j in range(S5_TB):
            ybuf[pl.ds(t0 + j, N_SEG, stride=S5_PITCH), :] = y[j * N_SEG:(j + 1) * N_SEG, :]
        return h_re, h_im

    h_re, h_im = lax.fori_loop(0, n_tb, pass2,
                               (jnp.concatenate(c_re, axis=0), jnp.concatenate(c_im, axis=0)))
    hfin_ref[0, 0] = jnp.concatenate([h_re[N_SEG - 1:, :], h_im[N_SEG - 1:, :]], axis=1)
    for i in range(N_SEG):
        y_ref[0, i * seg:(i + 1) * seg, :] = ybuf[i * S5_PITCH:i * S5_PITCH + seg, :].astype(BF16)


def _s5(u, wb, wc, wg, bg, dsk, lam):
    nb, s_len, _ = u.shape
    blk = lambda shape: pl.BlockSpec((1,) + shape, lambda b, g: (g, 0, 0))
    y, hfin = pl.pallas_call(
        _s5_kernel,
        grid=(nb, N_GBLK),
        in_specs=[pl.BlockSpec((1, s_len, CH_LANES), lambda b, g: (b, 0, g)),
                  blk((CH_LANES, 2 * ST_LANES)), blk((2 * ST_LANES, CH_LANES)),
                  blk((CH_LANES, 2 * CH_LANES)), blk((1, 2 * CH_LANES)), blk((1, CH_LANES)),
                  blk((1, 2 * ST_LANES))],
        out_specs=[pl.BlockSpec((1, s_len, CH_LANES), lambda b, g: (b, 0, g)),
                   pl.BlockSpec((1, 1, 1, 2 * ST_LANES), lambda b, g: (b, g, 0, 0))],
        out_shape=(jax.ShapeDtypeStruct((nb, s_len, SSM_WIDTH), BF16),
                   jax.ShapeDtypeStruct((nb, N_GBLK, 1, 2 * ST_LANES), F32)),
        scratch_shapes=[pltpu.VMEM((N_SEG * S5_PITCH, CH_LANES), F32),
                        pltpu.VMEM((N_SEG * S5_PITCH, CH_LANES), F32)],
        compiler_params=_cparams(("arbitrary", "arbitrary")),
        name="s5",
    )(u, wb.astype(BF16), wc.astype(BF16), wg.astype(BF16), bg, dsk, lam)
    hfin = hfin.reshape(nb, N_GBLK, 2, GROUPS_PER_BLOCK, SSM_STATE)
    return y, hfin[:, :, 0].reshape(nb, N_GROUPS, SSM_STATE), hfin[:, :, 1].reshape(nb, N_GROUPS, SSM_STATE)


def _s5s_kernel(u_ref, h0_ref, wb_ref, wc_ref, wg_ref, bg_ref, dsk_ref, lam_ref, y_ref, hfin_ref, *, nb):
    hi = lax.Precision.HIGHEST
    u = u_ref[...]
    n_t = u.shape[0] // nb
    bu = jnp.dot(u, wb_ref[0], precision=hi, preferred_element_type=F32)
    lam = lam_ref[0]
    l_re = jnp.broadcast_to(lam[:, :ST_LANES], (nb, ST_LANES))
    l_im = jnp.broadcast_to(lam[:, ST_LANES:], (nb, ST_LANES))
    h0 = h0_ref[0]
    h_re, h_im = h0[:, :ST_LANES], h0[:, ST_LANES:]
    hs = []
    for t in range(n_t):
        n_re, n_im = _cmul(l_re, l_im, h_re, h_im)
        h_re = n_re + bu[t * nb:(t + 1) * nb, :ST_LANES]
        h_im = n_im + bu[t * nb:(t + 1) * nb, ST_LANES:]
        hs.append(jnp.concatenate([h_re, h_im], axis=1))
    hfin_ref[0] = hs[-1]
    h = jnp.concatenate(hs, axis=0)
    y = jnp.dot(h, wc_ref[0], precision=hi, preferred_element_type=F32) + dsk_ref[0] * u
    z = jax.nn.gelu(y)
    zz = jnp.dot(z, wg_ref[0], precision=hi, preferred_element_type=F32) + bg_ref[0]
    y_ref[...] = (zz[:, :CH_LANES] * jax.nn.sigmoid(zz[:, CH_LANES:])).astype(BF16)


def _s5s(u_tm, h0, wb, wc, wg, bg, dsk, lam, nb):
    rows = u_tm.shape[0]
    blk = lambda shape: pl.BlockSpec((1,) + shape, lambda g: (g, 0, 0))
    return pl.pallas_call(
        functools.partial(_s5s_kernel, nb=nb),
        grid=(N_GBLK,),
        in_specs=[pl.BlockSpec((rows, CH_LANES), lambda g: (0, g)),
                  blk((nb, 2 * ST_LANES)),
                  blk((CH_LANES, 2 * ST_LANES)), blk((2 * ST_LANES, CH_LANES)),
                  blk((CH_LANES, 2 * CH_LANES)), blk((1, 2 * CH_LANES)), blk((1, CH_LANES)),
                  blk((1, 2 * ST_LANES))],
        out_specs=[pl.BlockSpec((rows, CH_LANES), lambda g: (0, g)), blk((nb, 2 * ST_LANES))],
        out_shape=(jax.ShapeDtypeStruct((rows, SSM_WIDTH), BF16),
                   jax.ShapeDtypeStruct((N_GBLK, nb, 2 * ST_LANES), F32)),
        compiler_params=_cparams(("arbitrary",)),
        name="s5s",
    )(u_tm, h0, wb, wc, wg, bg, dsk, lam)


def _outproj_kernel(att_ref, ssm_ref, x_ref, g1_ref, sc2_ref, sh2_ref, lng_ref, lnb_ref, wo_ref,
                    x1_ref, h2_ref):
    mix = (jnp.dot(att_ref[0], wo_ref[:ATT_WIDTH, :], preferred_element_type=F32)
           + jnp.dot(ssm_ref[0], wo_ref[ATT_WIDTH:, :], preferred_element_type=F32))
    x1 = _ln(ALPHA * x_ref[0] + (1.0 + g1_ref[0]) * mix) * lng_ref[...] + lnb_ref[...]
    x1_ref[0] = x1
    h2_ref[0] = (_ln(x1) * (1.0 + sc2_ref[0]) + sh2_ref[0]).astype(BF16)


def _outproj(att, ssm, x, gate1, scale2, shift2, ln_g, ln_b, w_o, tm):
    nb, l, _ = x.shape
    r = gate1.shape[1]
    mod_map = (lambda b, i: (b, 0, 0)) if r == 1 else (lambda b, i: (b, i, 0))
    row_map = lambda b, i: (b, i, 0)
    const2 = lambda b, i: (0, 0)
    half = pl.BlockSpec((1, tm, ATT_WIDTH), row_map)
    full = pl.BlockSpec((1, tm, D_MODEL), row_map)
    mod = pl.BlockSpec((1, r, D_MODEL), mod_map)
    vec = pl.BlockSpec((1, D_MODEL), const2)
    return pl.pallas_call(
        _outproj_kernel,
        grid=(nb, l // tm),
        in_specs=[half, half, full, mod, mod, mod, vec, vec,
                  pl.BlockSpec((D_MODEL, D_MODEL), const2)],
        out_specs=[full, full],
        out_shape=(jax.ShapeDtypeStruct((nb, l, D_MODEL), F32),
                   jax.ShapeDtypeStruct((nb, l, D_MODEL), BF16)),
        compiler_params=_cparams(("arbitrary", "arbitrary")),
        name="outproj",
    )(att, ssm, x, gate1, scale2, shift2, ln_g.reshape(1, D_MODEL), ln_b.reshape(1, D_MODEL), w_o)


FFN_TF = 512


def _ffn_kernel(h2_ref, x1_ref, g2_ref, lng_ref, lnb_ref, wu_ref, wd_ref, y_ref, acc_scr):
    f = pl.program_id(2)
    up = jnp.dot(h2_ref[0], wu_ref[...], preferred_element_type=F32)
    act = jnp.square(jnp.maximum(up, 0.0)).astype(BF16)
    part = jnp.dot(act, wd_ref[...], preferred_element_type=F32)

    @pl.when(f == 0)
    def _():
        acc_scr[...] = part

    @pl.when(f > 0)
    def _():
        acc_scr[...] += part

    @pl.when(f == pl.num_programs(2) - 1)
    def _():
        y_ref[0] = _ln(ALPHA * x1_ref[0] + (1.0 + g2_ref[0]) * acc_scr[...]) * lng_ref[...] + lnb_ref[...]


def _ffn(h2, x1, gate2, ln_g, ln_b, w_up, w_down, tm):
    nb, l, _ = x1.shape
    r = gate2.shape[1]
    mod_map = (lambda b, i, f: (b, 0, 0)) if r == 1 else (lambda b, i, f: (b, i, 0))
    row_map = lambda b, i, f: (b, i, 0)
    const2 = lambda b, i, f: (0, 0)
    full = pl.BlockSpec((1, tm, D_MODEL), row_map)
    vec = pl.BlockSpec((1, D_MODEL), const2)
    return pl.pallas_call(
        _ffn_kernel,
        grid=(nb, l // tm, D_FF // FFN_TF),
        in_specs=[full, full, pl.BlockSpec((1, r, D_MODEL), mod_map), vec, vec,
                  pl.BlockSpec((D_MODEL, FFN_TF), lambda b, i, f: (0, f)),
                  pl.BlockSpec((FFN_TF, D_MODEL), lambda b, i, f: (f, 0))],
        out_specs=full,
        out_shape=jax.ShapeDtypeStruct((nb, l, D_MODEL), F32),
        scratch_shapes=[pltpu.VMEM((tm, D_MODEL), F32)],
        compiler_params=_cparams(("arbitrary", "arbitrary", "arbitrary")),
        name="ffn",
    )(h2, x1, gate2, ln_g.reshape(1, D_MODEL), ln_b.reshape(1, D_MODEL), w_up, w_down)


def kernel(x_prompt, x_sample, c_prompt, c_sample, cache_k, cache_v, cache_logf, state_ssm_re,
           state_ssm_im, page_table, w_ada, b_ada, w_in, b_f, w_o, a_re, a_im, log_dt, b_re, b_im,
           c_re, c_im, d_skip, w_glu, b_glu, ln1_g, ln1_b, w_up, w_down, ln2_g, ln2_b):
    assert w_ada.shape[0] == DEPTH == 1
    nbp, s_len, _ = x_prompt.shape
    nbs, n_q, _ = x_sample.shape
    n_seq = nbp + nbs

    c_all = jnp.concatenate([c_prompt, c_sample, jnp.zeros((16 - n_seq, D_MODEL), F32)], axis=0)
    mod = _ada(c_all, w_ada[0], b_ada[0]).reshape(16, N_MOD, D_MODEL)
    mod_p = [mod[:nbp, i][:, None, :] for i in range(N_MOD)]
    mod_s = [jnp.repeat(mod[nbp:n_seq, i], n_q, axis=0)[None] for i in range(N_MOD)]

    a = ATT_WIDTH
    w_in0 = w_in[0]
    w_main = jnp.concatenate([w_in0[:, :3 * a], w_in0[:, 3 * a + N_HEADS:]], axis=1).astype(BF16)
    w_fcols = w_in0[:, 3 * a:3 * a + N_HEADS]
    w_f = jnp.pad(w_fcols, ((0, 0), (0, LANE - N_HEADS))).astype(BF16)
    w_ft = w_fcols.T.astype(BF16)
    w_o_b = w_o[0].astype(BF16)
    w_up_b = w_up[0].astype(BF16)
    w_down_b = w_down[0].astype(BF16)

    l_re, l_im, bb_re, bb_im = _s5prep(a_re[0], a_im[0], log_dt[0], b_re[0], b_im[0])
    wb, wc, wg, bg, dsk, lam = _s5_weights(l_re, l_im, bb_re, bb_im, c_re[0], c_im[0], d_skip[0],
                                           w_glu[0], b_glu[0])

    q, k, k_b, v, v_b, u, lfc, lfr = _inproj(x_prompt, mod_p[0], mod_p[1], w_main, w_f, w_ft, b_f[0], tm=512)
    fc, fr = _cumsum(lfc, lfr)
    att = _attn(q, k_b, v_b, fc, fr)
    ssm, hp_re, hp_im = _s5(u, wb, wc, wg, bg, dsk, lam)
    x1, h2 = _outproj(att, ssm, x_prompt, mod_p[2], mod_p[4], mod_p[3], ln1_g[0], ln1_b[0], w_o_b, tm=512)
    y_p = _ffn(h2, x1, mod_p[5], ln2_g[0], ln2_b[0], w_up_b, w_down_b, tm=512)

    rows = nbs * n_q
    xs = x_sample.reshape(1, rows, D_MODEL)
    qs, ks, _, vs, _, us, lfcs, _ = _inproj(xs, mod_s[0], mod_s[1], w_main, w_f, w_ft, b_f[0], tm=rows)
    n_pool = cache_k.shape[1]
    decay_wt = _decay(cache_logf[0].reshape(n_pool, PAGE_LANES))
    per_head = lambda t: t.reshape(nbs, n_q * N_HEADS, HEAD_DIM)
    lf_new = jnp.pad(lfcs.reshape(nbs, 1, n_q * N_HEADS), ((0, 0), (0, 0), (0, LANE - n_q * N_HEADS)))
    att_s = _attn_s(page_table, per_head(qs), per_head(ks), per_head(vs), lf_new, cache_k, cache_v, decay_wt)
    u_tm = us.reshape(nbs, n_q, SSM_WIDTH).transpose(1, 0, 2).reshape(rows, SSM_WIDTH)
    h0 = jnp.concatenate([state_ssm_re[0].reshape(nbs, N_GBLK, ST_LANES),
                          state_ssm_im[0].reshape(nbs, N_GBLK, ST_LANES)], axis=2).transpose(1, 0, 2)
    ssm_tm, hs_fin = _s5s(u_tm, h0, wb, wc, wg, bg, dsk, lam, nbs)
    ssm_s = ssm_tm.reshape(n_q, nbs, SSM_WIDTH).transpose(1, 0, 2).reshape(1, rows, SSM_WIDTH)
    hs_fin = hs_fin.transpose(1, 0, 2).reshape(nbs, N_GBLK, 2, GROUPS_PER_BLOCK, SSM_STATE)
    hs_re = hs_fin[:, :, 0].reshape(nbs, N_GROUPS, SSM_STATE)
    hs_im = hs_fin[:, :, 1].reshape(nbs, N_GROUPS, SSM_STATE)
    x1s, h2s = _outproj(att_s.reshape(1, rows, a).astype(BF16), ssm_s, xs, mod_s[2], mod_s[4], mod_s[3],
                        ln1_g[0], ln1_b[0], w_o_b, tm=rows)
    y_s = _ffn(h2s, x1s, mod_s[5], ln2_g[0], ln2_b[0], w_up_b, w_down_b, tm=rows)

    hd = (N_HEADS, HEAD_DIM)
    return (y_p, y_s.reshape(nbs, n_q, D_MODEL),
            k.reshape(1, nbp, s_len, *hd), v.reshape(1, nbp, s_len, *hd), lfc[None],
            hp_re[None], hp_im[None],
            ks.reshape(1, nbs, n_q, *hd), vs.reshape(1, nbs, n_q, *hd), lfcs.reshape(1, nbs, n_q, N_HEADS),
            hs_re[None], hs_im[None])
```

```python
import functools
import math

import jax
import jax.numpy as jnp
from jax import lax
from jax.experimental import pallas as pl
from jax.experimental.pallas import tpu as pltpu

F32 = jnp.float32
BF16 = jnp.bfloat16

LANE = 128
D_MODEL = 2048
ATT_WIDTH = 1024
SSM_WIDTH = 1024
HEAD_DIM = 128
N_HEADS = 8
SSM_GROUP = 16
N_GROUPS = 64
SSM_STATE = 64
D_FF = 8192
N_MOD = 6
PAGE = 128
DEPTH = 1
ALPHA = (2 * DEPTH) ** 0.25
LN_EPS = 1e-5
LOG2E = math.log2(math.e)
QSCALE = HEAD_DIM ** -0.5 * LOG2E

GROUPS_PER_BLOCK = 8
N_GBLK = N_GROUPS // GROUPS_PER_BLOCK
ST_LANES = GROUPS_PER_BLOCK * SSM_STATE
CH_LANES = GROUPS_PER_BLOCK * SSM_GROUP

VMEM_LIMIT = 48 * 1024 * 1024


def _cparams(sem):
    return pltpu.CompilerParams(dimension_semantics=sem, vmem_limit_bytes=VMEM_LIMIT)


def _ln(x):
    mu = jnp.mean(x, axis=-1, keepdims=True)
    xc = x - mu
    var = jnp.mean(xc * xc, axis=-1, keepdims=True)
    return xc * lax.rsqrt(var + LN_EPS)


def _log_sigmoid(x):
    return jnp.minimum(x, 0.0) - jnp.log1p(jnp.exp(-jnp.abs(x)))


def _nt_dot(a, b):
    return lax.dot_general(a, b, (((1,), (1,)), ((), ())), preferred_element_type=F32)


def _softmax_step(carry, s, v):
    m, l, acc = carry
    m_new = jnp.maximum(m, jnp.max(s, axis=1, keepdims=True))
    alpha = jnp.exp2(m - m_new)
    p = jnp.exp2(s - m_new)
    l = alpha * l + jnp.sum(p, axis=1, keepdims=True)
    acc = alpha * acc + jnp.dot(p.astype(BF16), v, preferred_element_type=F32)
    return m_new, l, acc


def _ada_kernel(c_ref, w_ref, b_ref, o_ref):
    c = c_ref[...]
    s = c * jax.nn.sigmoid(c)
    o_ref[...] = jnp.dot(s, w_ref[...], preferred_element_type=F32) + b_ref[...]


def _ada(c_all, w_ada, b_ada):
    rows = c_all.shape[0]
    n = w_ada.shape[1]
    tn = 1024
    return pl.pallas_call(
        _ada_kernel,
        grid=(n // tn,),
        in_specs=[pl.BlockSpec((rows, D_MODEL), lambda j: (0, 0)),
                  pl.BlockSpec((D_MODEL, tn), lambda j: (0, j)),
                  pl.BlockSpec((1, tn), lambda j: (0, j))],
        out_specs=pl.BlockSpec((rows, tn), lambda j: (0, j)),
        out_shape=jax.ShapeDtypeStruct((rows, n), F32),
        compiler_params=_cparams(("arbitrary",)),
        name="ada",
    )(c_all, w_ada, b_ada.reshape(1, n))


def _s5prep_kernel(are_ref, aim_ref, ldt_ref, arx_ref, aix_ref, ldx_ref, bre_ref, bim_ref,
                   lre_ref, lim_ref, bbre_ref, bbim_ref):
    def lam(a_re, a_im, log_dt):
        dt = jnp.exp(log_dt)
        mag = jnp.exp(a_re * dt)
        return mag * jnp.cos(a_im * dt), mag * jnp.sin(a_im * dt)

    l_re, l_im = lam(are_ref[...], aim_ref[...], ldt_ref[...])
    lre_ref[...] = l_re
    lim_ref[...] = l_im
    a_re, a_im = arx_ref[...], aix_ref[...]
    x_re, x_im = lam(a_re, a_im, ldx_ref[...])
    den = a_re * a_re + a_im * a_im
    n_re = x_re - 1.0
    k_re = (n_re * a_re + x_im * a_im) / den
    k_im = (x_im * a_re - n_re * a_im) / den
    b_re, b_im = bre_ref[...], bim_ref[...]
    bbre_ref[...] = k_re * b_re - k_im * b_im
    bbim_ref[...] = k_re * b_im + k_im * b_re


def _s5prep(a_re, a_im, log_dt, b_re, b_im):
    g, p, c = b_re.shape
    ldt = jnp.broadcast_to(log_dt[:, None], (g, p))
    ex = lambda a: jnp.broadcast_to(a[:, :, None], (g, p, c)).reshape(g, p * c)
    small = jax.ShapeDtypeStruct((g, p), F32)
    big = jax.ShapeDtypeStruct((g, p * c), F32)
    l_re, l_im, bb_re, bb_im = pl.pallas_call(
        _s5prep_kernel, out_shape=(small, small, big, big), name="s5prep",
    )(a_re, a_im, ldt, ex(a_re), ex(a_im), ex(ldt), b_re.reshape(g, p * c), b_im.reshape(g, p * c))
    return l_re, l_im, bb_re.reshape(g, p, c), bb_im.reshape(g, p, c)


def _blockdiag(w):
    g, a, b = w.shape
    w = w.reshape(N_GBLK, GROUPS_PER_BLOCK, a, b)
    eye = jnp.eye(GROUPS_PER_BLOCK, dtype=w.dtype)
    return jnp.einsum('xgab,gh->xgahb', w, eye).reshape(N_GBLK, GROUPS_PER_BLOCK * a, GROUPS_PER_BLOCK * b)


def _s5_weights(l_re, l_im, bb_re, bb_im, c_re, c_im, d_skip, w_glu, b_glu):
    wb = jnp.concatenate([_blockdiag(jnp.swapaxes(bb_re, 1, 2)),
                          _blockdiag(jnp.swapaxes(bb_im, 1, 2))], axis=2)
    wc = jnp.concatenate([_blockdiag(jnp.swapaxes(c_re, 1, 2)),
                          _blockdiag(-jnp.swapaxes(c_im, 1, 2))], axis=1)
    wg = jnp.concatenate([_blockdiag(w_glu[:, :, :SSM_GROUP]),
                          _blockdiag(w_glu[:, :, SSM_GROUP:])], axis=2)
    bg = jnp.concatenate([b_glu[:, :SSM_GROUP].reshape(N_GBLK, 1, CH_LANES),
                          b_glu[:, SSM_GROUP:].reshape(N_GBLK, 1, CH_LANES)], axis=2)
    dsk = d_skip.reshape(N_GBLK, 1, CH_LANES)
    lam = jnp.concatenate([l_re.reshape(N_GBLK, 1, ST_LANES), l_im.reshape(N_GBLK, 1, ST_LANES)], axis=2)
    return wb, wc, wg, bg, dsk, lam


def _inproj_kernel(x_ref, sh_ref, sc_ref, w_ref, wu_ref, wf_ref, wft_ref, bf_ref, bfr_ref,
                   q_ref, kf_ref, kb_ref, vf_ref, vb_ref, u_ref, lfc_ref, lfr_ref, h_scr):
    j = pl.program_id(2)

    @pl.when(j == 0)
    def _():
        h = _ln(x_ref[0]) * (1.0 + sc_ref[0]) + sh_ref[0]
        hb = h.astype(BF16)
        h_scr[...] = hb
        fl = jnp.dot(hb, wf_ref[...], preferred_element_type=F32) + bf_ref[...]
        lfc_ref[0] = _log_sigmoid(fl)[:, :N_HEADS]
        lfr_ref[0] = _log_sigmoid(_nt_dot(wft_ref[...], hb) + bfr_ref[...])

    def proj(w):
        return jnp.dot(h_scr[...], w[...], preferred_element_type=F32)

    @pl.when(j == 0)
    def _():
        q_ref[0] = (proj(w_ref) * QSCALE).astype(BF16)

    @pl.when(j == 1)
    def _():
        acc = proj(w_ref)
        kf_ref[0] = acc
        kb_ref[0] = acc.astype(BF16)

    @pl.when(j == 2)
    def _():
        acc = proj(w_ref)
        vf_ref[0] = acc
        vb_ref[0] = acc.astype(BF16)

    @pl.when(j == 3)
    def _():
        u_ref[0] = proj(wu_ref)


def _inproj(x, shift, scale, w_qkv, w_u, w_f, w_ft, b_f, tm):
    nb, l, _ = x.shape
    r = shift.shape[1]
    mod_map = (lambda b, i, j: (b, 0, 0)) if r == 1 else (lambda b, i, j: (b, i, 0))
    row_map = lambda b, i, j: (b, i, 0)
    const2 = lambda b, i, j: (0, 0)
    wide = lambda dt: jax.ShapeDtypeStruct((nb, l, ATT_WIDTH), dt)
    return pl.pallas_call(
        _inproj_kernel,
        grid=(nb, l // tm, 4),
        in_specs=[pl.BlockSpec((1, tm, D_MODEL), row_map),
                  pl.BlockSpec((1, r, D_MODEL), mod_map),
                  pl.BlockSpec((1, r, D_MODEL), mod_map),
                  pl.BlockSpec((D_MODEL, ATT_WIDTH), lambda b, i, j: (0, jnp.minimum(j, 2))),
                  pl.BlockSpec((D_MODEL, SSM_WIDTH), const2),
                  pl.BlockSpec((D_MODEL, LANE), const2),
                  pl.BlockSpec((N_HEADS, D_MODEL), const2),
                  pl.BlockSpec((1, LANE), const2),
                  pl.BlockSpec((N_HEADS, 1), const2)],
        out_specs=[pl.BlockSpec((1, tm, ATT_WIDTH), row_map)] * 6
                  + [pl.BlockSpec((1, tm, N_HEADS), row_map),
                     pl.BlockSpec((1, N_HEADS, tm), lambda b, i, j: (b, 0, i))],
        out_shape=(wide(BF16), wide(F32), wide(BF16), wide(F32), wide(BF16), wide(F32),
                   jax.ShapeDtypeStruct((nb, l, N_HEADS), F32),
                   jax.ShapeDtypeStruct((nb, N_HEADS, l), F32)),
        scratch_shapes=[pltpu.VMEM((tm, D_MODEL), BF16)],
        compiler_params=_cparams(("arbitrary", "arbitrary", "arbitrary")),
        name="inproj",
    )(x, shift, scale, w_qkv, w_u, w_f, w_ft, jnp.pad(b_f, (0, LANE - N_HEADS)).reshape(1, LANE),
      b_f.reshape(N_HEADS, 1))


CUM_BLK = 512


def _cumsum_kernel(lfr_ref, fr_ref):
    s_len = lfr_ref.shape[2]
    r = lax.broadcasted_iota(jnp.int32, (CUM_BLK, CUM_BLK), 0)
    c = lax.broadcasted_iota(jnp.int32, (CUM_BLK, CUM_BLK), 1)
    tri_u = (r <= c).astype(F32)
    carry = jnp.zeros((N_HEADS, 1), F32)
    for blk in range(s_len // CUM_BLK):
        cols = slice(blk * CUM_BLK, (blk + 1) * CUM_BLK)
        fr = jnp.dot(lfr_ref[0, :, cols], tri_u, precision=lax.Precision.HIGHEST,
                     preferred_element_type=F32) + carry
        fr_ref[0, :, cols] = fr * LOG2E
        carry = fr[:, CUM_BLK - 1:CUM_BLK]


def _cumsum(lfr):
    nb, _, s_len = lfr.shape
    rspec = pl.BlockSpec((1, N_HEADS, s_len), lambda b: (b, 0, 0))
    return pl.pallas_call(
        _cumsum_kernel, grid=(nb,), in_specs=[rspec], out_specs=rspec,
        out_shape=jax.ShapeDtypeStruct(lfr.shape, F32),
        compiler_params=_cparams(("arbitrary",)), name="cumsum",
    )(lfr)


ATT_BLK = 512


def _attn_kernel(q_ref, k_ref, v_ref, fr_ref, o_ref, s_scr):
    h = pl.program_id(1)
    qi = pl.program_id(2)
    t = ATT_BLK
    q = q_ref[0]

    def scores(j):
        start = pl.multiple_of(j * t, t)
        fk = fr_ref[0, pl.ds(h, 1), pl.ds(start, t)]
        return _nt_dot(q, k_ref[0, pl.ds(start, t), :]) - fk

    def values(j):
        return v_ref[0, pl.ds(pl.multiple_of(j * t, t), t), :]

    def stage(j, carry, slot):
        s_scr[1 - slot] = scores(j + 1)
        return _softmax_step(carry, s_scr[slot], values(j))

    def pair(jj, carry):
        return stage(2 * jj + 1, stage(2 * jj, carry, 0), 1)

    def diagonal(carry, slot):
        row = lax.broadcasted_iota(jnp.int32, (t, t), 0)
        col = lax.broadcasted_iota(jnp.int32, (t, t), 1)
        return _softmax_step(carry, jnp.where(col <= row, s_scr[slot], -jnp.inf), values(qi))

    s_scr[0] = scores(0)
    init = (jnp.full((t, 1), -jnp.inf, F32), jnp.zeros((t, 1), F32), jnp.zeros((t, HEAD_DIM), F32))
    carry = lax.fori_loop(0, qi // 2, pair, init)
    m, l, acc = lax.cond(qi % 2 == 1,
                         lambda c: diagonal(stage(qi - 1, c, 0), 1),
                         lambda c: diagonal(c, 0), carry)
    o_ref[0] = (acc / l).astype(BF16)


def _attn(q, k, v, fr):
    nb, s_len, _ = q.shape
    t = ATT_BLK
    qspec = pl.BlockSpec((1, t, HEAD_DIM), lambda b, h, i: (b, i, h))
    kvspec = pl.BlockSpec((1, s_len, HEAD_DIM), lambda b, h, i: (b, 0, h))
    return pl.pallas_call(
        _attn_kernel,
        grid=(nb, N_HEADS, s_len // t),
        in_specs=[qspec, kvspec, kvspec,
                  pl.BlockSpec((1, N_HEADS, s_len), lambda b, h, i: (b, 0, 0))],
        out_specs=qspec,
        out_shape=jax.ShapeDtypeStruct((nb, s_len, ATT_WIDTH), BF16),
        scratch_shapes=[pltpu.VMEM((2, t, t), F32)],
        compiler_params=_cparams(("arbitrary", "arbitrary", "arbitrary")),
        name="attn",
    )(q, k, v, fr)


PAGE_LANES = PAGE * N_HEADS


def _decay_kernel(lf_ref, wt_ref):
    x = lf_ref[...]
    n = x.shape[1]
    lane = lax.broadcasted_iota(jnp.int32, x.shape, 1)
    suffix, total = x, x
    k = N_HEADS
    while k < n:
        suffix = suffix + jnp.where(lane < n - k, pltpu.roll(suffix, n - k, axis=1), 0.0)
        total = total + pltpu.roll(total, k, axis=1)
        k *= 2
    wt_ref[:, 0, :] = suffix - x
    wt_ref[:, 1, :] = total


def _decay(lf_flat):
    n_pool, n = lf_flat.shape
    rb = 256
    return pl.pallas_call(
        _decay_kernel,
        grid=(n_pool // rb,),
        in_specs=[pl.BlockSpec((rb, n), lambda i: (i, 0))],
        out_specs=pl.BlockSpec((rb, 2, n), lambda i: (i, 0, 0)),
        out_shape=jax.ShapeDtypeStruct((n_pool, 2, n), F32),
        compiler_params=_cparams(("arbitrary",)),
        name="decay",
    )(lf_flat)


PAGES_PER_STEP = 8


def _attn_s_kernel(pt_ref, q_ref, kn_ref, vn_ref, lfn_ref, *refs):
    n_pg = PAGES_PER_STEP
    kp_refs, vp_refs, wt_refs = refs[:n_pg], refs[n_pg:2 * n_pg], refs[2 * n_pg:3 * n_pg]
    o_ref, bias_scr, m_scr, l_scr, acc_scr, carry_scr = refs[3 * n_pg:]
    i = pl.program_id(1)
    rows = q_ref.shape[1]
    q = q_ref[0]

    @pl.when(i == 0)
    def _():
        f = jnp.broadcast_to(lfn_ref[0], (N_HEADS, LANE))
        lane8 = lax.broadcasted_iota(jnp.int32, f.shape, 1)
        k = N_HEADS
        while k < rows:
            f = f + jnp.where(lane8 >= k, pltpu.roll(f, k, axis=1), 0.0)
            k *= 2
        f_row = f[0:1, :] * LOG2E
        r = lax.broadcasted_iota(jnp.int32, (rows, LANE), 0)
        c = lax.broadcasted_iota(jnp.int32, (rows, LANE), 1)
        rr = lax.broadcasted_iota(jnp.int32, (rows, PAGE_LANES), 0)
        cc = lax.broadcasted_iota(jnp.int32, (rows, PAGE_LANES), 1)
        same_head = jnp.bitwise_and(rr, N_HEADS - 1) == jnp.bitwise_and(cc, N_HEADS - 1)
        bias_scr[...] = jnp.where(same_head, 0.0, -jnp.inf)
        pad = jnp.zeros((LANE - rows, HEAD_DIM), BF16)
        kn = jnp.concatenate([kn_ref[0].astype(BF16), pad], axis=0)
        vn = jnp.concatenate([vn_ref[0].astype(BF16), pad], axis=0)
        tok_r = jnp.right_shift(r, 3)
        tok_c = jnp.right_shift(c, 3)
        head_ok = jnp.bitwise_and(r, N_HEADS - 1) == jnp.bitwise_and(c, N_HEADS - 1)
        s = _nt_dot(q, kn) - f_row
        s = jnp.where(head_ok, jnp.where(tok_c <= tok_r, s, -jnp.inf), -jnp.inf)
        init = (jnp.full((rows, 1), -jnp.inf, F32), jnp.zeros((rows, 1), F32),
                jnp.zeros((rows, HEAD_DIM), F32))
        m, l, acc = _softmax_step(init, s, vn)
        m_scr[...] = m
        l_scr[...] = l
        acc_scr[...] = acc
        carry_scr[...] = jnp.zeros_like(carry_scr)

    carry = carry_scr[...]
    bias = bias_scr[...]
    scores, values = [], []
    for pg in range(n_pg):
        wt = wt_refs[pg][0]
        dec = (wt[0:1, :] + carry) * LOG2E
        carry = carry + wt[1:2, :]
        k = kp_refs[pg][...].reshape(PAGE_LANES, HEAD_DIM).astype(BF16)
        scores.append(_nt_dot(q, k) + bias + dec)
        values.append(vp_refs[pg][...].reshape(PAGE_LANES, HEAD_DIM).astype(BF16))
    state = _softmax_step((m_scr[...], l_scr[...], acc_scr[...]),
                          jnp.concatenate(scores, axis=1), jnp.concatenate(values, axis=0))
    carry_scr[...] = carry
    m_scr[...], l_scr[...], acc_scr[...] = state

    @pl.when(i == pl.num_programs(1) - 1)
    def _():
        o_ref[0] = state[2] / state[1]


def _attn_s(page_table, q, k_new, v_new, lf_new, cache_k, cache_v, decay_wt):
    nb, rows, _ = q.shape
    n_pages = page_table.shape[1]
    n_pg = PAGES_PER_STEP
    seq = lambda b, i, pt: (b, 0, 0)

    def page_spec(block, n_lead, pg):
        def idx(b, i, pt):
            return (0,) * n_lead + (pt[b, n_pages - 1 - (i * n_pg + pg)],) + (0,) * (len(block) - n_lead - 1)
        return pl.BlockSpec(block, idx)

    kv_block = (None, None, PAGE, N_HEADS, HEAD_DIM)
    grid_spec = pltpu.PrefetchScalarGridSpec(
        num_scalar_prefetch=1,
        grid=(nb, n_pages // n_pg),
        in_specs=[pl.BlockSpec((1, rows, HEAD_DIM), seq),
                  pl.BlockSpec((1, rows, HEAD_DIM), seq),
                  pl.BlockSpec((1, rows, HEAD_DIM), seq),
                  pl.BlockSpec((1, 1, LANE), seq)]
                 + [page_spec(kv_block, 1, pg) for pg in range(n_pg)]
                 + [page_spec(kv_block, 1, pg) for pg in range(n_pg)]
                 + [page_spec((1, 2, PAGE_LANES), 0, pg) for pg in range(n_pg)],
        out_specs=pl.BlockSpec((1, rows, HEAD_DIM), seq),
        scratch_shapes=[pltpu.VMEM((rows, PAGE_LANES), F32),
                        pltpu.VMEM((rows, 1), F32),
                        pltpu.VMEM((rows, 1), F32),
                        pltpu.VMEM((rows, HEAD_DIM), F32),
                        pltpu.VMEM((1, PAGE_LANES), F32)])
    return pl.pallas_call(
        _attn_s_kernel, grid_spec=grid_spec,
        out_shape=jax.ShapeDtypeStruct((nb, rows, HEAD_DIM), F32),
        compiler_params=_cparams(("arbitrary", "arbitrary")), name="attn_s",
    )(page_table, q, k_new, v_new, lf_new, *([cache_k] * n_pg), *([cache_v] * n_pg), *([decay_wt] * n_pg))


N_SEG = 8
S5_TB = 64
S5_PITCH = 520


def _cmul(ar, ai, br, bi):
    return ar * br - ai * bi, ar * bi + ai * br


def _s5_tail(h, u, wc, dsk, wg, bg):
    y = jnp.dot(h.astype(BF16), wc, preferred_element_type=F32) + dsk * u
    z = jax.nn.gelu(y)
    zz = jnp.dot(z.astype(BF16), wg, preferred_element_type=F32) + bg
    return zz[:, :CH_LANES] * jax.nn.sigmoid(zz[:, CH_LANES:])


def _s5_kernel(u_ref, wb_ref, wc_ref, wg_ref, bg_ref, dsk_ref, lam_ref, y_ref, hfin_ref, ubuf, ybuf):
    s_len = u_ref.shape[1]
    seg = s_len // N_SEG
    n_tb = seg // S5_TB
    assert seg <= S5_PITCH and seg % S5_TB == 0
    for i in range(N_SEG):
        ubuf[i * S5_PITCH:i * S5_PITCH + seg, :] = u_ref[0, i * seg:(i + 1) * seg, :]
    wb = wb_ref[0]
    lam = lam_ref[0]
    l_re = jnp.broadcast_to(lam[:, :ST_LANES], (N_SEG, ST_LANES))
    l_im = jnp.broadcast_to(lam[:, ST_LANES:], (N_SEG, ST_LANES))

    def load_block(tb):
        t0 = tb * S5_TB
        u = jnp.concatenate([ubuf[pl.ds(t0 + j, N_SEG, stride=S5_PITCH), :] for j in range(S5_TB)], axis=0)
        return u, jnp.dot(u.astype(BF16), wb, preferred_element_type=F32)

    def scan_block(bu, h_re, h_im, keep):
        hs = []
        for j in range(S5_TB):
            n_re, n_im = _cmul(l_re, l_im, h_re, h_im)
            h_re = n_re + bu[j * N_SEG:(j + 1) * N_SEG, :ST_LANES]
            h_im = n_im + bu[j * N_SEG:(j + 1) * N_SEG, ST_LANES:]
            if keep:
                hs.append(jnp.concatenate([h_re, h_im], axis=1))
        return h_re, h_im, hs

    def pass1(tb, carry):
        _, bu = load_block(tb)
        h_re, h_im, _ = scan_block(bu, *carry, keep=False)
        return h_re, h_im

    zero = jnp.zeros((N_SEG, ST_LANES), F32)
    f_re, f_im = lax.fori_loop(0, n_tb, pass1, (zero, zero))
    p_re, p_im = lam[:, :ST_LANES], lam[:, ST_LANES:]
    n_sq = seg.bit_length() - 1
    assert 1 << n_sq == seg
    for _ in range(n_sq):
        p_re, p_im = _cmul(p_re, p_im, p_re, p_im)
    c_re, c_im = [jnp.zeros((1, ST_LANES), F32)], [jnp.zeros((1, ST_LANES), F32)]
    for i in range(1, N_SEG):
        d_re, d_im = _cmul(p_re, p_im, c_re[-1], c_im[-1])
        c_re.append(f_re[i - 1:i, :] + d_re)
        c_im.append(f_im[i - 1:i, :] + d_im)
    wc, wg, bg, dsk = wc_ref[0], wg_ref[0], bg_ref[0], dsk_ref[0]

    def pass2(tb, carry):
        u, bu = load_block(tb)
        h_re, h_im, hs = scan_block(bu, *carry, keep=True)
        y = _s5_tail(jnp.concatenate(hs, axis=0), u, wc, dsk, wg, bg)
        t0 = tb * S5_TB
        for j in range(S5_TB):
            ybuf[pl.ds(t0 + j, N_SEG, stride=S5_PITCH), :] = y[j * N_SEG:(j + 1) * N_SEG, :]
        return h_re, h_im

    h_re, h_im = lax.fori_loop(0, n_tb, pass2,
                               (jnp.concatenate(c_re, axis=0), jnp.concatenate(c_im, axis=0)))
    hfin_ref[0, 0] = jnp.concatenate([h_re[N_SEG - 1:, :], h_im[N_SEG - 1:, :]], axis=1)
    for i in range(N_SEG):
        y_ref[0, i * seg:(i + 1) * seg, :] = ybuf[i * S5_PITCH:i * S5_PITCH + seg, :].astype(BF16)


def _s5(u, wb, wc, wg, bg, dsk, lam):
    nb, s_len, _ = u.shape
    blk = lambda shape: pl.BlockSpec((1,) + shape, lambda b, g: (g, 0, 0))
    y, hfin = pl.pallas_call(
        _s5_kernel,
        grid=(nb, N_GBLK),
        in_specs=[pl.BlockSpec((1, s_len, CH_LANES), lambda b, g: (b, 0, g)),
                  blk((CH_LANES, 2 * ST_LANES)), blk((2 * ST_LANES, CH_LANES)),
                  blk((CH_LANES, 2 * CH_LANES)), blk((1, 2 * CH_LANES)), blk((1, CH_LANES)),
                  blk((1, 2 * ST_LANES))],
        out_specs=[pl.BlockSpec((1, s_len, CH_LANES), lambda b, g: (b, 0, g)),
                   pl.BlockSpec((1, 1, 1, 2 * ST_LANES), lambda b, g: (b, g, 0, 0))],
        out_shape=(jax.ShapeDtypeStruct((nb, s_len, SSM_WIDTH), BF16),
                   jax.ShapeDtypeStruct((nb, N_GBLK, 1, 2 * ST_LANES), F32)),
        scratch_shapes=[pltpu.VMEM((N_SEG * S5_PITCH, CH_LANES), F32),
                        pltpu.VMEM((N_SEG * S5_PITCH, CH_LANES), F32)],
        compiler_params=_cparams(("arbitrary", "arbitrary")),
        name="s5",
    )(u, wb.astype(BF16), wc.astype(BF16), wg.astype(BF16), bg, dsk, lam)
    hfin = hfin.reshape(nb, N_GBLK, 2, GROUPS_PER_BLOCK, SSM_STATE)
    return y, hfin[:, :, 0].reshape(nb, N_GROUPS, SSM_STATE), hfin[:, :, 1].reshape(nb, N_GROUPS, SSM_STATE)


def _s5s_kernel(u_ref, h0_ref, wb_ref, wc_ref, wg_ref, bg_ref, dsk_ref, lam_ref, y_ref, hfin_ref, *, nb):
    hi = lax.Precision.HIGHEST
    u = u_ref[...]
    n_t = u.shape[0] // nb
    bu = jnp.dot(u, wb_ref[0], precision=hi, preferred_element_type=F32)
    lam = lam_ref[0]
    l_re = jnp.broadcast_to(lam[:, :ST_LANES], (nb, ST_LANES))
    l_im = jnp.broadcast_to(lam[:, ST_LANES:], (nb, ST_LANES))
    h0 = h0_ref[0]
    h_re, h_im = h0[:, :ST_LANES], h0[:, ST_LANES:]
    hs = []
    for t in range(n_t):
        n_re, n_im = _cmul(l_re, l_im, h_re, h_im)
        h_re = n_re + bu[t * nb:(t + 1) * nb, :ST_LANES]
        h_im = n_im + bu[t * nb:(t + 1) * nb, ST_LANES:]
        hs.append(jnp.concatenate([h_re, h_im], axis=1))
    hfin_ref[0] = hs[-1]
    h = jnp.concatenate(hs, axis=0)
    y = jnp.dot(h, wc_ref[0], precision=hi, preferred_element_type=F32) + dsk_ref[0] * u
    z = jax.nn.gelu(y)
    zz = jnp.dot(z, wg_ref[0], precision=hi, preferred_element_type=F32) + bg_ref[0]
    y_ref[...] = (zz[:, :CH_LANES] * jax.nn.sigmoid(zz[:, CH_LANES:])).astype(BF16)


def _s5s(u_tm, h0, wb, wc, wg, bg, dsk, lam, nb):
    rows = u_tm.shape[0]
    blk = lambda shape: pl.BlockSpec((1,) + shape, lambda g: (g, 0, 0))
    return pl.pallas_call(
        functools.partial(_s5s_kernel, nb=nb),
        grid=(N_GBLK,),
        in_specs=[pl.BlockSpec((rows, CH_LANES), lambda g: (0, g)),
                  blk((nb, 2 * ST_LANES)),
                  blk((CH_LANES, 2 * ST_LANES)), blk((2 * ST_LANES, CH_LANES)),
                  blk((CH_LANES, 2 * CH_LANES)), blk((1, 2 * CH_LANES)), blk((1, CH_LANES)),
                  blk((1, 2 * ST_LANES))],
        out_specs=[pl.BlockSpec((rows, CH_LANES), lambda g: (0, g)), blk((nb, 2 * ST_LANES))],
        out_shape=(jax.ShapeDtypeStruct((rows, SSM_WIDTH), BF16),
                   jax.ShapeDtypeStruct((N_GBLK, nb, 2 * ST_LANES), F32)),
        compiler_params=_cparams(("arbitrary",)),
        name="s5s",
    )(u_tm, h0, wb, wc, wg, bg, dsk, lam)


def _outproj_kernel(att_ref, ssm_ref, x_ref, g1_ref, sc2_ref, sh2_ref, lng_ref, lnb_ref, wo_ref,
                    x1_ref, h2_ref):
    tm = x_ref.shape[1]
    n_sub = 2 if tm >= 512 else 1
    sub = tm // n_sub

    def mod_rows(ref, rows):
        return ref[0] if ref.shape[1] == 1 else ref[0, rows, :]

    for c in range(n_sub):
        rows = slice(c * sub, (c + 1) * sub)
        mix = (jnp.dot(att_ref[0, rows, :], wo_ref[:ATT_WIDTH, :], preferred_element_type=F32)
               + jnp.dot(ssm_ref[0, rows, :], wo_ref[ATT_WIDTH:, :], preferred_element_type=F32))
        x1 = (_ln(ALPHA * x_ref[0, rows, :] + (1.0 + mod_rows(g1_ref, rows)) * mix) * lng_ref[...]
              + lnb_ref[...])
        x1_ref[0, rows, :] = x1
        h2_ref[0, rows, :] = (_ln(x1) * (1.0 + mod_rows(sc2_ref, rows)) + mod_rows(sh2_ref, rows)).astype(BF16)


def _outproj(att, ssm, x, gate1, scale2, shift2, ln_g, ln_b, w_o, tm):
    nb, l, _ = x.shape
    r = gate1.shape[1]
    mod_map = (lambda b, i: (b, 0, 0)) if r == 1 else (lambda b, i: (b, i, 0))
    row_map = lambda b, i: (b, i, 0)
    const2 = lambda b, i: (0, 0)
    half = pl.BlockSpec((1, tm, ATT_WIDTH), row_map)
    full = pl.BlockSpec((1, tm, D_MODEL), row_map)
    mod = pl.BlockSpec((1, r, D_MODEL), mod_map)
    vec = pl.BlockSpec((1, D_MODEL), const2)
    return pl.pallas_call(
        _outproj_kernel,
        grid=(nb, l // tm),
        in_specs=[half, half, full, mod, mod, mod, vec, vec,
                  pl.BlockSpec((D_MODEL, D_MODEL), const2)],
        out_specs=[full, full],
        out_shape=(jax.ShapeDtypeStruct((nb, l, D_MODEL), F32),
                   jax.ShapeDtypeStruct((nb, l, D_MODEL), BF16)),
        compiler_params=_cparams(("arbitrary", "arbitrary")),
        name="outproj",
    )(att, ssm, x, gate1, scale2, shift2, ln_g.reshape(1, D_MODEL), ln_b.reshape(1, D_MODEL), w_o)


FFN_TF = 512


def _ffn_kernel(h2_ref, x1_ref, g2_ref, lng_ref, lnb_ref, wu_ref, wd_ref, y_ref, acc_scr):
    f = pl.program_id(2)
    @pl.when(f == 0)
    def _():
        acc_scr[...] = jnp.zeros_like(acc_scr)

    up = jnp.dot(h2_ref[0], wu_ref[...], preferred_element_type=F32)
    act = jnp.square(jnp.maximum(up, 0.0)).astype(BF16)
    acc_scr[...] += jnp.dot(act, wd_ref[...], preferred_element_type=F32)

    @pl.when(f == pl.num_programs(2) - 1)
    def _():
        y_ref[0] = _ln(ALPHA * x1_ref[0] + (1.0 + g2_ref[0]) * acc_scr[...]) * lng_ref[...] + lnb_ref[...]


def _ffn(h2, x1, gate2, ln_g, ln_b, w_up, w_down, tm):
    nb, l, _ = x1.shape
    r = gate2.shape[1]
    mod_map = (lambda b, i, f: (b, 0, 0)) if r == 1 else (lambda b, i, f: (b, i, 0))
    row_map = lambda b, i, f: (b, i, 0)
    const2 = lambda b, i, f: (0, 0)
    full = pl.BlockSpec((1, tm, D_MODEL), row_map)
    vec = pl.BlockSpec((1, D_MODEL), const2)
    return pl.pallas_call(
        _ffn_kernel,
        grid=(nb, l // tm, D_FF // FFN_TF),
        in_specs=[full, full, pl.BlockSpec((1, r, D_MODEL), mod_map), vec, vec,
                  pl.BlockSpec((D_MODEL, FFN_TF), lambda b, i, f: (0, f)),
                  pl.BlockSpec((FFN_TF, D_MODEL), lambda b, i, f: (f, 0))],
        out_specs=full,
        out_shape=jax.ShapeDtypeStruct((nb, l, D_MODEL), F32),
        scratch_shapes=[pltpu.VMEM((tm, D_MODEL), F32)],
        compiler_params=_cparams(("arbitrary", "arbitrary", "arbitrary")),
        name="ffn",
    )(h2, x1, gate2, ln_g.reshape(1, D_MODEL), ln_b.reshape(1, D_MODEL), w_up, w_down)


def kernel(x_prompt, x_sample, c_prompt, c_sample, cache_k, cache_v, cache_logf, state_ssm_re,
           state_ssm_im, page_table, w_ada, b_ada, w_in, b_f, w_o, a_re, a_im, log_dt, b_re, b_im,
           c_re, c_im, d_skip, w_glu, b_glu, ln1_g, ln1_b, w_up, w_down, ln2_g, ln2_b):
    assert w_ada.shape[0] == DEPTH == 1
    nbp, s_len, _ = x_prompt.shape
    nbs, n_q, _ = x_sample.shape
    n_seq = nbp + nbs

    c_all = jnp.concatenate([c_prompt, c_sample, jnp.zeros((16 - n_seq, D_MODEL), F32)], axis=0)
    mod = _ada(c_all, w_ada[0], b_ada[0]).reshape(16, N_MOD, D_MODEL)
    mod_p = [mod[:nbp, i][:, None, :] for i in range(N_MOD)]
    mod_s = [jnp.repeat(mod[nbp:n_seq, i], n_q, axis=0)[None] for i in range(N_MOD)]

    a = ATT_WIDTH
    w_in0 = w_in[0]
    w_qkv = w_in0[:, :3 * a].astype(BF16)
    w_u = w_in0[:, 3 * a + N_HEADS:].astype(BF16)
    w_fcols = w_in0[:, 3 * a:3 * a + N_HEADS]
    w_f = jnp.pad(w_fcols, ((0, 0), (0, LANE - N_HEADS))).astype(BF16)
    w_ft = w_fcols.T.astype(BF16)
    w_o_b = w_o[0].astype(BF16)
    w_up_b = w_up[0].astype(BF16)
    w_down_b = w_down[0].astype(BF16)

    l_re, l_im, bb_re, bb_im = _s5prep(a_re[0], a_im[0], log_dt[0], b_re[0], b_im[0])
    wb, wc, wg, bg, dsk, lam = _s5_weights(l_re, l_im, bb_re, bb_im, c_re[0], c_im[0], d_skip[0],
                                           w_glu[0], b_glu[0])

    q, k, k_b, v, v_b, u, lfc, lfr = _inproj(x_prompt, mod_p[0], mod_p[1], w_qkv, w_u, w_f, w_ft, b_f[0], tm=512)
    att = _attn(q, k_b, v_b, _cumsum(lfr))
    ssm, hp_re, hp_im = _s5(u, wb, wc, wg, bg, dsk, lam)
    x1, h2 = _outproj(att, ssm, x_prompt, mod_p[2], mod_p[4], mod_p[3], ln1_g[0], ln1_b[0], w_o_b, tm=512)
    y_p = _ffn(h2, x1, mod_p[5], ln2_g[0], ln2_b[0], w_up_b, w_down_b, tm=512)

    rows = nbs * n_q
    xs = x_sample.reshape(1, rows, D_MODEL)
    qs, ks, _, vs, _, us, lfcs, _ = _inproj(xs, mod_s[0], mod_s[1], w_qkv, w_u, w_f, w_ft, b_f[0], tm=rows)
    n_pool = cache_k.shape[1]
    decay_wt = _decay(cache_logf[0].reshape(n_pool, PAGE_LANES))
    per_head = lambda t: t.reshape(nbs, n_q * N_HEADS, HEAD_DIM)
    lf_new = jnp.pad(lfcs.reshape(nbs, 1, n_q * N_HEADS), ((0, 0), (0, 0), (0, LANE - n_q * N_HEADS)))
    att_s = _attn_s(page_table, per_head(qs), per_head(ks), per_head(vs), lf_new, cache_k, cache_v, decay_wt)
    u_tm = us.reshape(nbs, n_q, SSM_WIDTH).transpose(1, 0, 2).reshape(rows, SSM_WIDTH)
    h0 = jnp.concatenate([state_ssm_re[0].reshape(nbs, N_GBLK, ST_LANES),
                          state_ssm_im[0].reshape(nbs, N_GBLK, ST_LANES)], axis=2).transpose(1, 0, 2)
    ssm_tm, hs_fin = _s5s(u_tm, h0, wb, wc, wg, bg, dsk, lam, nbs)
    ssm_s = ssm_tm.reshape(n_q, nbs, SSM_WIDTH).transpose(1, 0, 2).reshape(1, rows, SSM_WIDTH)
    hs_fin = hs_fin.transpose(1, 0, 2).reshape(nbs, N_GBLK, 2, GROUPS_PER_BLOCK, SSM_STATE)
    hs_re = hs_fin[:, :, 0].reshape(nbs, N_GROUPS, SSM_STATE)
    hs_im = hs_fin[:, :, 1].reshape(nbs, N_GROUPS, SSM_STATE)
    x1s, h2s = _outproj(att_s.reshape(1, rows, a).astype(BF16), ssm_s, xs, mod_s[2], mod_s[4], mod_s[3],
                        ln1_g[0], ln1_b[0], w_o_b, tm=rows)
    y_s = _ffn(h2s, x1s, mod_s[5], ln2_g[0], ln2_b[0], w_up_b, w_down_b, tm=rows)

    hd = (N_HEADS, HEAD_DIM)
    return (y_p, y_s.reshape(nbs, n_q, D_MODEL),
            k.reshape(1, nbp, s_len, *hd), v.reshape(1, nbp, s_len, *hd), lfc[None],
            hp_re[None], hp_im[None],
            ks.reshape(1, nbs, n_q, *hd), vs.reshape(1, nbs, n_q, *hd), lfcs.reshape(1, nbs, n_q, N_HEADS),
            hs_re[None], hs_im[None])
```

```python
import functools
import math

import jax
import jax.numpy as jnp
from jax import lax
from jax.experimental import pallas as pl
from jax.experimental.pallas import tpu as pltpu

F32 = jnp.float32
BF16 = jnp.bfloat16

LANE = 128
D_MODEL = 2048
ATT_WIDTH = 1024
SSM_WIDTH = 1024
HEAD_DIM = 128
N_HEADS = 8
SSM_GROUP = 16
N_GROUPS = 64
SSM_STATE = 64
D_FF = 8192
N_MOD = 6
PAGE = 128
DEPTH = 1
ALPHA = (2 * DEPTH) ** 0.25
LN_EPS = 1e-5
LOG2E = math.log2(math.e)
QSCALE = HEAD_DIM ** -0.5 * LOG2E

GROUPS_PER_BLOCK = 8
N_GBLK = N_GROUPS // GROUPS_PER_BLOCK
ST_LANES = GROUPS_PER_BLOCK * SSM_STATE
CH_LANES = GROUPS_PER_BLOCK * SSM_GROUP

VMEM_LIMIT = 56 * 1024 * 1024


def _cparams(sem):
    return pltpu.CompilerParams(dimension_semantics=sem, vmem_limit_bytes=VMEM_LIMIT)


def _ln(x):
    mu = jnp.mean(x, axis=-1, keepdims=True)
    xc = x - mu
    var = jnp.mean(xc * xc, axis=-1, keepdims=True)
    return xc * lax.rsqrt(var + LN_EPS)


def _log_sigmoid(x):
    return jnp.minimum(x, 0.0) - jnp.log1p(jnp.exp(-jnp.abs(x)))


def _nt_dot(a, b):
    return lax.dot_general(a, b, (((1,), (1,)), ((), ())), preferred_element_type=F32)


def _softmax_weights(m, l, s):
    m_new = jnp.maximum(m, jnp.max(s, axis=1, keepdims=True))
    alpha = jnp.exp2(m - m_new)
    p = jnp.exp2(s - m_new)
    return m_new, alpha, p.astype(BF16), alpha * l + jnp.sum(p, axis=1, keepdims=True)


def _softmax_step(carry, s, v):
    m, l, acc = carry
    m_new, alpha, p, l_new = _softmax_weights(m, l, s)
    return m_new, l_new, alpha * acc + jnp.dot(p, v, preferred_element_type=F32)


def _ada_kernel(c_ref, w_ref, b_ref, o_ref):
    c = c_ref[...]
    s = c * jax.nn.sigmoid(c)
    o_ref[...] = jnp.dot(s, w_ref[...], preferred_element_type=F32) + b_ref[...]


def _ada(c_all, w_ada, b_ada):
    rows = c_all.shape[0]
    n = w_ada.shape[1]
    tn = 1024
    return pl.pallas_call(
        _ada_kernel,
        grid=(n // tn,),
        in_specs=[pl.BlockSpec((rows, D_MODEL), lambda j: (0, 0)),
                  pl.BlockSpec((D_MODEL, tn), lambda j: (0, j)),
                  pl.BlockSpec((1, tn), lambda j: (0, j))],
        out_specs=pl.BlockSpec((rows, tn), lambda j: (0, j)),
        out_shape=jax.ShapeDtypeStruct((rows, n), F32),
        compiler_params=_cparams(("arbitrary",)),
        name="ada",
    )(c_all, w_ada, b_ada.reshape(1, n))


def _s5prep_kernel(are_ref, aim_ref, ldt_ref, arx_ref, aix_ref, ldx_ref, bre_ref, bim_ref,
                   lre_ref, lim_ref, bbre_ref, bbim_ref):
    def lam(a_re, a_im, log_dt):
        dt = jnp.exp(log_dt)
        mag = jnp.exp(a_re * dt)
        return mag * jnp.cos(a_im * dt), mag * jnp.sin(a_im * dt)

    l_re, l_im = lam(are_ref[...], aim_ref[...], ldt_ref[...])
    lre_ref[...] = l_re
    lim_ref[...] = l_im
    a_re, a_im = arx_ref[...], aix_ref[...]
    x_re, x_im = lam(a_re, a_im, ldx_ref[...])
    den = a_re * a_re + a_im * a_im
    n_re = x_re - 1.0
    k_re = (n_re * a_re + x_im * a_im) / den
    k_im = (x_im * a_re - n_re * a_im) / den
    b_re, b_im = bre_ref[...], bim_ref[...]
    bbre_ref[...] = k_re * b_re - k_im * b_im
    bbim_ref[...] = k_re * b_im + k_im * b_re


def _s5prep(a_re, a_im, log_dt, b_re, b_im):
    g, p, c = b_re.shape
    ldt = jnp.broadcast_to(log_dt[:, None], (g, p))
    ex = lambda a: jnp.broadcast_to(a[:, :, None], (g, p, c)).reshape(g, p * c)
    small = jax.ShapeDtypeStruct((g, p), F32)
    big = jax.ShapeDtypeStruct((g, p * c), F32)
    l_re, l_im, bb_re, bb_im = pl.pallas_call(
        _s5prep_kernel, out_shape=(small, small, big, big), name="s5prep",
    )(a_re, a_im, ldt, ex(a_re), ex(a_im), ex(ldt), b_re.reshape(g, p * c), b_im.reshape(g, p * c))
    return l_re, l_im, bb_re.reshape(g, p, c), bb_im.reshape(g, p, c)


def _blockdiag(w):
    g, a, b = w.shape
    w = w.reshape(N_GBLK, GROUPS_PER_BLOCK, a, b)
    eye = jnp.eye(GROUPS_PER_BLOCK, dtype=w.dtype)
    return jnp.einsum('xgab,gh->xgahb', w, eye).reshape(N_GBLK, GROUPS_PER_BLOCK * a, GROUPS_PER_BLOCK * b)


def _s5_weights(l_re, l_im, bb_re, bb_im, c_re, c_im, d_skip, w_glu, b_glu):
    wb = jnp.concatenate([_blockdiag(jnp.swapaxes(bb_re, 1, 2)),
                          _blockdiag(jnp.swapaxes(bb_im, 1, 2))], axis=2)
    wc = jnp.concatenate([_blockdiag(jnp.swapaxes(c_re, 1, 2)),
                          _blockdiag(-jnp.swapaxes(c_im, 1, 2))], axis=1)
    wg = jnp.concatenate([_blockdiag(w_glu[:, :, :SSM_GROUP]),
                          _blockdiag(w_glu[:, :, SSM_GROUP:])], axis=2)
    bg = jnp.concatenate([b_glu[:, :SSM_GROUP].reshape(N_GBLK, 1, CH_LANES),
                          b_glu[:, SSM_GROUP:].reshape(N_GBLK, 1, CH_LANES)], axis=2)
    dsk = d_skip.reshape(N_GBLK, 1, CH_LANES)
    lam = jnp.concatenate([l_re.reshape(N_GBLK, 1, ST_LANES), l_im.reshape(N_GBLK, 1, ST_LANES)], axis=2)
    return wb, wc, wg, bg, dsk, lam


def _inproj_kernel(x_ref, sh_ref, sc_ref, w_ref, wu_ref, wf_ref, wft_ref, bf_ref, bfr_ref,
                   q_ref, kf_ref, kb_ref, vf_ref, vb_ref, u_ref, lfc_ref, lfr_ref, h_scr):
    j = pl.program_id(2)

    @pl.when(j == 0)
    def _():
        h = _ln(x_ref[0]) * (1.0 + sc_ref[0]) + sh_ref[0]
        hb = h.astype(BF16)
        h_scr[...] = hb
        fl = _nt_dot(hb, wf_ref[...]) + bf_ref[...]
        lfc_ref[0] = _log_sigmoid(fl)[:, :N_HEADS]
        lfr_ref[0] = _log_sigmoid(_nt_dot(wft_ref[...], hb) + bfr_ref[...])

    def proj(w):
        return _nt_dot(h_scr[...], w[...])

    @pl.when(j == 0)
    def _():
        q_ref[0] = (proj(w_ref) * QSCALE).astype(BF16)

    @pl.when(j == 1)
    def _():
        acc = proj(w_ref)
        kf_ref[0] = acc
        kb_ref[0] = acc.astype(BF16)

    @pl.when(j == 2)
    def _():
        acc = proj(w_ref)
        vf_ref[0] = acc
        vb_ref[0] = acc.astype(BF16)

    @pl.when(j == 3)
    def _():
        u_ref[0] = proj(wu_ref)


def _inproj(x, shift, scale, w_qkv, w_u, w_f, w_ft, b_f, tm):
    nb, l, _ = x.shape
    r = shift.shape[1]
    mod_map = (lambda b, i, j: (b, 0, 0)) if r == 1 else (lambda b, i, j: (b, i, 0))
    row_map = lambda b, i, j: (b, i, 0)
    const2 = lambda b, i, j: (0, 0)
    wide = lambda dt: jax.ShapeDtypeStruct((nb, l, ATT_WIDTH), dt)
    return pl.pallas_call(
        _inproj_kernel,
        grid=(nb, l // tm, 4),
        in_specs=[pl.BlockSpec((1, tm, D_MODEL), row_map),
                  pl.BlockSpec((1, r, D_MODEL), mod_map),
                  pl.BlockSpec((1, r, D_MODEL), mod_map),
                  pl.BlockSpec((ATT_WIDTH, D_MODEL), lambda b, i, j: (jnp.minimum(j, 2), 0)),
                  pl.BlockSpec((SSM_WIDTH, D_MODEL), const2),
                  pl.BlockSpec((LANE, D_MODEL), const2),
                  pl.BlockSpec((N_HEADS, D_MODEL), const2),
                  pl.BlockSpec((1, LANE), const2),
                  pl.BlockSpec((N_HEADS, 1), const2)],
        out_specs=[pl.BlockSpec((1, tm, ATT_WIDTH), row_map)] * 6
                  + [pl.BlockSpec((1, tm, N_HEADS), row_map),
                     pl.BlockSpec((1, N_HEADS, tm), lambda b, i, j: (b, 0, i))],
        out_shape=(wide(BF16), wide(F32), wide(BF16), wide(F32), wide(BF16), wide(F32),
                   jax.ShapeDtypeStruct((nb, l, N_HEADS), F32),
                   jax.ShapeDtypeStruct((nb, N_HEADS, l), F32)),
        scratch_shapes=[pltpu.VMEM((tm, D_MODEL), BF16)],
        compiler_params=_cparams(("arbitrary", "arbitrary", "arbitrary")),
        name="inproj",
    )(x, shift, scale, w_qkv, w_u, w_f, w_ft, jnp.pad(b_f, (0, LANE - N_HEADS)).reshape(1, LANE),
      b_f.reshape(N_HEADS, 1))


CUM_BLK = 512


def _cumsum_kernel(lfr_ref, fr_ref):
    s_len = lfr_ref.shape[2]
    r = lax.broadcasted_iota(jnp.int32, (CUM_BLK, CUM_BLK), 0)
    c = lax.broadcasted_iota(jnp.int32, (CUM_BLK, CUM_BLK), 1)
    tri_u = (r <= c).astype(F32)
    carry = jnp.zeros((N_HEADS, 1), F32)
    for blk in range(s_len // CUM_BLK):
        cols = slice(blk * CUM_BLK, (blk + 1) * CUM_BLK)
        fr = jnp.dot(lfr_ref[0, :, cols], tri_u, precision=lax.Precision.HIGHEST,
                     preferred_element_type=F32) + carry
        fr_ref[0, :, cols] = fr * LOG2E
        carry = fr[:, CUM_BLK - 1:CUM_BLK]


def _cumsum(lfr):
    nb, _, s_len = lfr.shape
    rspec = pl.BlockSpec((1, N_HEADS, s_len), lambda b: (b, 0, 0))
    return pl.pallas_call(
        _cumsum_kernel, grid=(nb,), in_specs=[rspec], out_specs=rspec,
        out_shape=jax.ShapeDtypeStruct(lfr.shape, F32),
        compiler_params=_cparams(("arbitrary",)), name="cumsum",
    )(lfr)


ATT_BLK = 512


def _attn_kernel(q_ref, k_ref, v_ref, fr_ref, o_ref, s_scr):
    h = pl.program_id(1)
    qi = pl.program_id(2)
    t = ATT_BLK
    q = q_ref[0]

    def scores(j):
        start = pl.multiple_of(j * t, t)
        fk = fr_ref[0, pl.ds(h, 1), pl.ds(start, t)]
        return _nt_dot(q, k_ref[0, pl.ds(start, t), :]) - fk

    def values(j):
        return v_ref[0, pl.ds(pl.multiple_of(j * t, t), t), :]

    def stage(j, carry, slot):
        s_scr[1 - slot] = scores(j + 1)
        return _softmax_step(carry, s_scr[slot], values(j))

    def pair(jj, carry):
        return stage(2 * jj + 1, stage(2 * jj, carry, 0), 1)

    def diagonal(carry, slot):
        row = lax.broadcasted_iota(jnp.int32, (t, t), 0)
        col = lax.broadcasted_iota(jnp.int32, (t, t), 1)
        return _softmax_step(carry, jnp.where(col <= row, s_scr[slot], -jnp.inf), values(qi))

    s_scr[0] = scores(0)
    init = (jnp.full((t, 1), -jnp.inf, F32), jnp.zeros((t, 1), F32), jnp.zeros((t, HEAD_DIM), F32))
    carry = lax.fori_loop(0, qi // 2, pair, init)
    m, l, acc = lax.cond(qi % 2 == 1,
                         lambda c: diagonal(stage(qi - 1, c, 0), 1),
                         lambda c: diagonal(c, 0), carry)
    o_ref[0] = (acc / l).astype(BF16)


def _attn(q, k, v, fr):
    nb, s_len, _ = q.shape
    t = ATT_BLK
    qspec = pl.BlockSpec((1, t, HEAD_DIM), lambda b, h, i: (b, i, h))
    kvspec = pl.BlockSpec((1, s_len, HEAD_DIM), lambda b, h, i: (b, 0, h))
    return pl.pallas_call(
        _attn_kernel,
        grid=(nb, N_HEADS, s_len // t),
        in_specs=[qspec, kvspec, kvspec,
                  pl.BlockSpec((1, N_HEADS, s_len), lambda b, h, i: (b, 0, 0))],
        out_specs=qspec,
        out_shape=jax.ShapeDtypeStruct((nb, s_len, ATT_WIDTH), BF16),
        scratch_shapes=[pltpu.VMEM((2, t, t), F32)],
        compiler_params=_cparams(("arbitrary", "arbitrary", "arbitrary")),
        name="attn",
    )(q, k, v, fr)


PAGE_LANES = PAGE * N_HEADS


def _decay_kernel(lf_ref, wt_ref):
    x = lf_ref[...]
    n = x.shape[1]
    lane = lax.broadcasted_iota(jnp.int32, x.shape, 1)
    suffix, total = x, x
    k = N_HEADS
    while k < n:
        suffix = suffix + jnp.where(lane < n - k, pltpu.roll(suffix, n - k, axis=1), 0.0)
        total = total + pltpu.roll(total, k, axis=1)
        k *= 2
    wt_ref[:, 0, :] = suffix - x
    wt_ref[:, 1, :] = total


def _decay(lf_flat):
    n_pool, n = lf_flat.shape
    rb = 256
    return pl.pallas_call(
        _decay_kernel,
        grid=(n_pool // rb,),
        in_specs=[pl.BlockSpec((rb, n), lambda i: (i, 0))],
        out_specs=pl.BlockSpec((rb, 2, n), lambda i: (i, 0, 0)),
        out_shape=jax.ShapeDtypeStruct((n_pool, 2, n), F32),
        compiler_params=_cparams(("arbitrary",)),
        name="decay",
    )(lf_flat)


def _attn_s_init(q, kn_ref, vn_ref, lfn_ref, bias_scr, m_scr, l_scr, acc_scr, carry_scr):
    rows = q.shape[0]
    f = jnp.broadcast_to(lfn_ref[0], (N_HEADS, LANE))
    lane8 = lax.broadcasted_iota(jnp.int32, f.shape, 1)
    k = N_HEADS
    while k < rows:
        f = f + jnp.where(lane8 >= k, pltpu.roll(f, k, axis=1), 0.0)
        k *= 2
    f_row = f[0:1, :] * LOG2E
    r = lax.broadcasted_iota(jnp.int32, (rows, LANE), 0)
    c = lax.broadcasted_iota(jnp.int32, (rows, LANE), 1)
    rr = lax.broadcasted_iota(jnp.int32, (rows, PAGE_LANES), 0)
    cc = lax.broadcasted_iota(jnp.int32, (rows, PAGE_LANES), 1)
    same_head = jnp.bitwise_and(rr, N_HEADS - 1) == jnp.bitwise_and(cc, N_HEADS - 1)
    bias_scr[...] = jnp.where(same_head, 0.0, -jnp.inf)
    pad = jnp.zeros((LANE - rows, HEAD_DIM), BF16)
    kn = jnp.concatenate([kn_ref[0].astype(BF16), pad], axis=0)
    vn = jnp.concatenate([vn_ref[0].astype(BF16), pad], axis=0)
    tok_r = jnp.right_shift(r, 3)
    tok_c = jnp.right_shift(c, 3)
    head_ok = jnp.bitwise_and(r, N_HEADS - 1) == jnp.bitwise_and(c, N_HEADS - 1)
    s = _nt_dot(q, kn) - f_row
    s = jnp.where(head_ok, jnp.where(tok_c <= tok_r, s, -jnp.inf), -jnp.inf)
    init = (jnp.full((rows, 1), -jnp.inf, F32), jnp.zeros((rows, 1), F32), jnp.zeros((rows, HEAD_DIM), F32))
    m_scr[...], l_scr[...], acc_scr[...] = _softmax_step(init, s, vn)
    carry_scr[...] = jnp.zeros_like(carry_scr)


def _attn_s_scores(q, kp_refs, wt_refs, bias_scr, carry_scr):
    carry = carry_scr[...]
    bias = bias_scr[...]
    scores = []
    for kp_ref, wt_ref in zip(kp_refs, wt_refs):
        wt = wt_ref[0]
        dec = (wt[0:1, :] + carry) * LOG2E
        carry = carry + wt[1:2, :]
        k = kp_ref[...].reshape(PAGE_LANES, HEAD_DIM).astype(BF16)
        scores.append(_nt_dot(q, k) + bias + dec)
    carry_scr[...] = carry
    return jnp.concatenate(scores, axis=1)


def _attn_s_values(vp_refs):
    return jnp.concatenate([vp_ref[...].reshape(PAGE_LANES, HEAD_DIM).astype(BF16) for vp_ref in vp_refs], axis=0)


N_SEG = 8
S5_TB = 64
S5_PITCH = 520


def _cmul(ar, ai, br, bi):
    return ar * br - ai * bi, ar * bi + ai * br


def _s5_tail(h, u, wc, dsk, wg, bg):
    y = jnp.dot(h.astype(BF16), wc, preferred_element_type=F32) + dsk * u
    z = jax.nn.gelu(y)
    zz = jnp.dot(z.astype(BF16), wg, preferred_element_type=F32) + bg
    return zz[:, :CH_LANES] * jax.nn.sigmoid(zz[:, CH_LANES:])


def _s5_kernel(u_ref, wb_ref, wc_ref, wg_ref, bg_ref, dsk_ref, lam_ref, y_ref, hfin_ref, ubuf, ybuf):
    s_len = u_ref.shape[1]
    seg = s_len // N_SEG
    n_tb = seg // S5_TB
    assert seg <= S5_PITCH and seg % S5_TB == 0
    for i in range(N_SEG):
        ubuf[i * S5_PITCH:i * S5_PITCH + seg, :] = u_ref[0, i * seg:(i + 1) * seg, :]
    wb = wb_ref[0]
    lam = lam_ref[0]
    l_re = jnp.broadcast_to(lam[:, :ST_LANES], (N_SEG, ST_LANES))
    l_im = jnp.broadcast_to(lam[:, ST_LANES:], (N_SEG, ST_LANES))

    def load_block(tb):
        t0 = tb * S5_TB
        u = jnp.concatenate([ubuf[pl.ds(t0 + j, N_SEG, stride=S5_PITCH), :] for j in range(S5_TB)], axis=0)
        return u, jnp.dot(u.astype(BF16), wb, preferred_element_type=F32)

    def scan_block(bu, h_re, h_im, keep):
        hs = []
        for j in range(S5_TB):
            n_re, n_im = _cmul(l_re, l_im, h_re, h_im)
            h_re = n_re + bu[j * N_SEG:(j + 1) * N_SEG, :ST_LANES]
            h_im = n_im + bu[j * N_SEG:(j + 1) * N_SEG, ST_LANES:]
            if keep:
                hs.append(jnp.concatenate([h_re, h_im], axis=1))
        return h_re, h_im, hs

    def pass1(tb, carry):
        _, bu = load_block(tb)
        h_re, h_im, _ = scan_block(bu, *carry, keep=False)
        return h_re, h_im

    zero = jnp.zeros((N_SEG, ST_LANES), F32)
    f_re, f_im = lax.fori_loop(0, n_tb, pass1, (zero, zero))
    p_re, p_im = lam[:, :ST_LANES], lam[:, ST_LANES:]
    n_sq = seg.bit_length() - 1
    assert 1 << n_sq == seg
    for _ in range(n_sq):
        p_re, p_im = _cmul(p_re, p_im, p_re, p_im)
    c_re, c_im = [jnp.zeros((1, ST_LANES), F32)], [jnp.zeros((1, ST_LANES), F32)]
    for i in range(1, N_SEG):
        d_re, d_im = _cmul(p_re, p_im, c_re[-1], c_im[-1])
        c_re.append(f_re[i - 1:i, :] + d_re)
        c_im.append(f_im[i - 1:i, :] + d_im)
    wc, wg, bg, dsk = wc_ref[0], wg_ref[0], bg_ref[0], dsk_ref[0]

    def pass2(tb, carry):
        u, bu = load_block(tb)
        h_re, h_im, hs = scan_block(bu, *carry, keep=True)
        y = _s5_tail(jnp.concatenate(hs, axis=0), u, wc, dsk, wg, bg)
        t0 = tb * S5_TB
        for j in range(S5_TB):
            ybuf[pl.ds(t0 + j, N_SEG, stride=S5_PITCH), :] = y[j * N_SEG:(j + 1) * N_SEG, :]
        return h_re, h_im

    h_re, h_im = lax.fori_loop(0, n_tb, pass2,
                               (jnp.concatenate(c_re, axis=0), jnp.concatenate(c_im, axis=0)))
    hfin_ref[0, 0] = jnp.concatenate([h_re[N_SEG - 1:, :], h_im[N_SEG - 1:, :]], axis=1)
    for i in range(N_SEG):
        y_ref[0, i * seg:(i + 1) * seg, :] = ybuf[i * S5_PITCH:i * S5_PITCH + seg, :].astype(BF16)


def _s5(u, wb, wc, wg, bg, dsk, lam):
    nb, s_len, _ = u.shape
    blk = lambda shape: pl.BlockSpec((1,) + shape, lambda b, g: (g, 0, 0))
    y, hfin = pl.pallas_call(
        _s5_kernel,
        grid=(nb, N_GBLK),
        in_specs=[pl.BlockSpec((1, s_len, CH_LANES), lambda b, g: (b, 0, g)),
                  blk((CH_LANES, 2 * ST_LANES)), blk((2 * ST_LANES, CH_LANES)),
                  blk((CH_LANES, 2 * CH_LANES)), blk((1, 2 * CH_LANES)), blk((1, CH_LANES)),
                  blk((1, 2 * ST_LANES))],
        out_specs=[pl.BlockSpec((1, s_len, CH_LANES), lambda b, g: (b, 0, g)),
                   pl.BlockSpec((1, 1, 1, 2 * ST_LANES), lambda b, g: (b, g, 0, 0))],
        out_shape=(jax.ShapeDtypeStruct((nb, s_len, SSM_WIDTH), BF16),
                   jax.ShapeDtypeStruct((nb, N_GBLK, 1, 2 * ST_LANES), F32)),
        scratch_shapes=[pltpu.VMEM((N_SEG * S5_PITCH, CH_LANES), F32),
                        pltpu.VMEM((N_SEG * S5_PITCH, CH_LANES), F32)],
        compiler_params=_cparams(("arbitrary", "arbitrary")),
        name="s5",
    )(u, wb.astype(BF16), wc.astype(BF16), wg.astype(BF16), bg, dsk, lam)
    hfin = hfin.reshape(nb, N_GBLK, 2, GROUPS_PER_BLOCK, SSM_STATE)
    return y, hfin[:, :, 0].reshape(nb, N_GROUPS, SSM_STATE), hfin[:, :, 1].reshape(nb, N_GROUPS, SSM_STATE)


def _s5s_kernel(u_ref, h0_ref, wb_ref, wc_ref, wg_ref, bg_ref, dsk_ref, lam_ref, y_ref, hfin_ref, *, nb):
    hi = lax.Precision.HIGHEST
    u = u_ref[...]
    n_t = u.shape[0] // nb
    bu = jnp.dot(u, wb_ref[0], precision=hi, preferred_element_type=F32)
    lam = lam_ref[0]
    l_re = jnp.broadcast_to(lam[:, :ST_LANES], (nb, ST_LANES))
    l_im = jnp.broadcast_to(lam[:, ST_LANES:], (nb, ST_LANES))
    h0 = h0_ref[0]
    h_re, h_im = h0[:, :ST_LANES], h0[:, ST_LANES:]
    hs = []
    for t in range(n_t):
        n_re, n_im = _cmul(l_re, l_im, h_re, h_im)
        h_re = n_re + bu[t * nb:(t + 1) * nb, :ST_LANES]
        h_im = n_im + bu[t * nb:(t + 1) * nb, ST_LANES:]
        hs.append(jnp.concatenate([h_re, h_im], axis=1))
    hfin_ref[0] = hs[-1]
    h = jnp.concatenate(hs, axis=0)
    y = jnp.dot(h, wc_ref[0], precision=hi, preferred_element_type=F32) + dsk_ref[0] * u
    z = jax.nn.gelu(y)
    zz = jnp.dot(z, wg_ref[0], precision=hi, preferred_element_type=F32) + bg_ref[0]
    y_ref[...] = (zz[:, :CH_LANES] * jax.nn.sigmoid(zz[:, CH_LANES:])).astype(BF16)


def _s5s(u_tm, h0, wb, wc, wg, bg, dsk, lam, nb):
    rows = u_tm.shape[0]
    blk = lambda shape: pl.BlockSpec((1,) + shape, lambda g: (g, 0, 0))
    return pl.pallas_call(
        functools.partial(_s5s_kernel, nb=nb),
        grid=(N_GBLK,),
        in_specs=[pl.BlockSpec((rows, CH_LANES), lambda g: (0, g)),
                  blk((nb, 2 * ST_LANES)),
                  blk((CH_LANES, 2 * ST_LANES)), blk((2 * ST_LANES, CH_LANES)),
                  blk((CH_LANES, 2 * CH_LANES)), blk((1, 2 * CH_LANES)), blk((1, CH_LANES)),
                  blk((1, 2 * ST_LANES))],
        out_specs=[pl.BlockSpec((rows, CH_LANES), lambda g: (0, g)), blk((nb, 2 * ST_LANES))],
        out_shape=(jax.ShapeDtypeStruct((rows, SSM_WIDTH), BF16),
                   jax.ShapeDtypeStruct((N_GBLK, nb, 2 * ST_LANES), F32)),
        compiler_params=_cparams(("arbitrary",)),
        name="s5s",
    )(u_tm, h0, wb, wc, wg, bg, dsk, lam)


def _outproj_kernel(att_ref, ssm_ref, x_ref, g1_ref, sc2_ref, sh2_ref, lng_ref, lnb_ref, wo_ref,
                    x1_ref, h2_ref):
    tm = x_ref.shape[1]
    n_sub = 2 if tm >= 512 else 1
    sub = tm // n_sub

    def mod_rows(ref, rows):
        return ref[0] if ref.shape[1] == 1 else ref[0, rows, :]

    for c in range(n_sub):
        rows = slice(c * sub, (c + 1) * sub)
        mix = (jnp.dot(att_ref[0, rows, :], wo_ref[:ATT_WIDTH, :], preferred_element_type=F32)
               + jnp.dot(ssm_ref[0, rows, :], wo_ref[ATT_WIDTH:, :], preferred_element_type=F32))
        x1 = (_ln(ALPHA * x_ref[0, rows, :] + (1.0 + mod_rows(g1_ref, rows)) * mix) * lng_ref[...]
              + lnb_ref[...])
        x1_ref[0, rows, :] = x1
        h2_ref[0, rows, :] = (_ln(x1) * (1.0 + mod_rows(sc2_ref, rows)) + mod_rows(sh2_ref, rows)).astype(BF16)


def _outproj(att, ssm, x, gate1, scale2, shift2, ln_g, ln_b, w_o, tm):
    nb, l, _ = x.shape
    r = gate1.shape[1]
    mod_map = (lambda b, i: (b, 0, 0)) if r == 1 else (lambda b, i: (b, i, 0))
    row_map = lambda b, i: (b, i, 0)
    const2 = lambda b, i: (0, 0)
    half = pl.BlockSpec((1, tm, ATT_WIDTH), row_map)
    full = pl.BlockSpec((1, tm, D_MODEL), row_map)
    mod = pl.BlockSpec((1, r, D_MODEL), mod_map)
    vec = pl.BlockSpec((1, D_MODEL), const2)
    return pl.pallas_call(
        _outproj_kernel,
        grid=(nb, l // tm),
        in_specs=[half, half, full, mod, mod, mod, vec, vec,
                  pl.BlockSpec((D_MODEL, D_MODEL), const2)],
        out_specs=[full, full],
        out_shape=(jax.ShapeDtypeStruct((nb, l, D_MODEL), F32),
                   jax.ShapeDtypeStruct((nb, l, D_MODEL), BF16)),
        compiler_params=_cparams(("arbitrary", "arbitrary")),
        name="outproj",
    )(att, ssm, x, gate1, scale2, shift2, ln_g.reshape(1, D_MODEL), ln_b.reshape(1, D_MODEL), w_o)


FFN_TF = 1024
FUSED_TF = 512


def _ffn_zero(f, acc_scr):
    @pl.when(f == 0)
    def _():
        acc_scr[...] = jnp.zeros_like(acc_scr)


def _ffn_accumulate(h2_ref, wu_ref, wd_ref, acc_scr):
    up = jnp.dot(h2_ref[0], wu_ref[...], preferred_element_type=F32)
    act = jnp.square(jnp.maximum(up, 0.0)).astype(BF16)
    acc_scr[...] += jnp.dot(act, wd_ref[...], preferred_element_type=F32)


def _ffn_finish(f, x1_ref, g2_ref, lng_ref, lnb_ref, y_ref, acc_scr):
    @pl.when(f == pl.num_programs(2) - 1)
    def _():
        y_ref[0] = _ln(ALPHA * x1_ref[0] + (1.0 + g2_ref[0]) * acc_scr[...]) * lng_ref[...] + lnb_ref[...]


def _ffn_kernel(h2_ref, x1_ref, g2_ref, lng_ref, lnb_ref, wu_ref, wd_ref, y_ref, acc_scr):
    f = pl.program_id(2)
    _ffn_zero(f, acc_scr)
    _ffn_accumulate(h2_ref, wu_ref, wd_ref, acc_scr)
    _ffn_finish(f, x1_ref, g2_ref, lng_ref, lnb_ref, y_ref, acc_scr)


def _ffn_specs(gate2, tm, tf):
    r = gate2.shape[1]
    wrap = lambda fn: (lambda b, i, f, *_: fn(b, i, f))
    row_map = wrap(lambda b, i, f: (b, i, 0))
    full = pl.BlockSpec((1, tm, D_MODEL), row_map)
    vec = pl.BlockSpec((1, D_MODEL), wrap(lambda b, i, f: (0, 0)))
    mod = pl.BlockSpec((1, r, D_MODEL), wrap((lambda b, i, f: (b, 0, 0)) if r == 1 else (lambda b, i, f: (b, i, 0))))
    in_specs = [full, full, mod, vec, vec,
                pl.BlockSpec((D_MODEL, tf), wrap(lambda b, i, f: (0, f))),
                pl.BlockSpec((tf, D_MODEL), wrap(lambda b, i, f: (f, 0)))]
    return in_specs, full


def _ffn(h2, x1, gate2, ln_g, ln_b, w_up, w_down, tm):
    nb, l, _ = x1.shape
    in_specs, out_spec = _ffn_specs(gate2, tm, FFN_TF)
    return pl.pallas_call(
        _ffn_kernel,
        grid=(nb, l // tm, D_FF // FFN_TF),
        in_specs=in_specs,
        out_specs=out_spec,
        out_shape=jax.ShapeDtypeStruct((nb, l, D_MODEL), F32),
        scratch_shapes=[pltpu.VMEM((tm, D_MODEL), F32)],
        compiler_params=_cparams(("arbitrary", "arbitrary", "arbitrary")),
        name="ffn",
    )(h2, x1, gate2, ln_g.reshape(1, D_MODEL), ln_b.reshape(1, D_MODEL), w_up, w_down)


def _ffn_attn_kernel(pt_ref, h2_ref, x1_ref, g2_ref, lng_ref, lnb_ref, wu_ref, wd_ref,
                     q_ref, kn_ref, vn_ref, lfn_ref, *refs, n_pg, steps_per_seq):
    kp_refs, vp_refs, wt_refs = refs[:n_pg], refs[n_pg:2 * n_pg], refs[2 * n_pg:3 * n_pg]
    y_ref, o_ref, acc_scr, bias_scr, m_scr, l_scr, a_scr, carry_scr = refs[3 * n_pg:]
    f = pl.program_id(2)
    t = (pl.program_id(0) * pl.num_programs(1) + pl.program_id(1)) * pl.num_programs(2) + f
    si = lax.rem(t, steps_per_seq)
    q = q_ref[0]

    @pl.when(si == 0)
    def _():
        _attn_s_init(q, kn_ref, vn_ref, lfn_ref, bias_scr, m_scr, l_scr, a_scr, carry_scr)

    _ffn_zero(f, acc_scr)
    s = _attn_s_scores(q, kp_refs, wt_refs, bias_scr, carry_scr)
    up = jnp.dot(h2_ref[0], wu_ref[...], preferred_element_type=F32)
    m_new, alpha, p, l_new = _softmax_weights(m_scr[...], l_scr[...], s)
    act = jnp.square(jnp.maximum(up, 0.0)).astype(BF16)
    acc_scr[...] += jnp.dot(act, wd_ref[...], preferred_element_type=F32)
    a_new = alpha * a_scr[...] + jnp.dot(p, _attn_s_values(vp_refs), preferred_element_type=F32)
    m_scr[...], l_scr[...], a_scr[...] = m_new, l_new, a_new

    @pl.when(si == steps_per_seq - 1)
    def _():
        o_ref[0] = a_new / l_new

    _ffn_finish(f, x1_ref, g2_ref, lng_ref, lnb_ref, y_ref, acc_scr)


def _ffn_attn(h2, x1, gate2, ln_g, ln_b, w_up, w_down, tm,
              page_table, q, k_new, v_new, lf_new, cache_k, cache_v, decay_wt):
    nb, l, _ = x1.shape
    nbs, rows, _ = q.shape
    n_pages = page_table.shape[1]
    n_i, n_f = l // tm, D_FF // FUSED_TF
    n_steps = nb * n_i * n_f
    n_pg = nbs * n_pages // n_steps
    steps_per_seq = n_pages // n_pg
    assert n_pg * n_steps == nbs * n_pages and steps_per_seq * n_pg == n_pages

    def step(b, i, f):
        t = (b * n_i + i) * n_f + f
        return lax.div(t, steps_per_seq), lax.rem(t, steps_per_seq)

    seq = lambda b, i, f, pt: (step(b, i, f)[0], 0, 0)

    def page_spec(block, n_lead, pg):
        def idx(b, i, f, pt):
            sq, si = step(b, i, f)
            return (0,) * n_lead + (pt[sq, n_pages - 1 - (si * n_pg + pg)],) + (0,) * (len(block) - n_lead - 1)
        return pl.BlockSpec(block, idx)

    ffn_specs, y_spec = _ffn_specs(gate2, tm, FUSED_TF)
    kv_block = (None, None, PAGE, N_HEADS, HEAD_DIM)
    tok_spec = pl.BlockSpec((1, rows, HEAD_DIM), seq)
    grid_spec = pltpu.PrefetchScalarGridSpec(
        num_scalar_prefetch=1,
        grid=(nb, n_i, n_f),
        in_specs=ffn_specs + [tok_spec, tok_spec, tok_spec, pl.BlockSpec((1, 1, LANE), seq)]
                 + [page_spec(kv_block, 1, pg) for pg in range(n_pg)]
                 + [page_spec(kv_block, 1, pg) for pg in range(n_pg)]
                 + [page_spec((1, 2, PAGE_LANES), 0, pg) for pg in range(n_pg)],
        out_specs=[y_spec, tok_spec],
        scratch_shapes=[pltpu.VMEM((tm, D_MODEL), F32),
                        pltpu.VMEM((rows, PAGE_LANES), F32),
                        pltpu.VMEM((rows, 1), F32),
                        pltpu.VMEM((rows, 1), F32),
                        pltpu.VMEM((rows, HEAD_DIM), F32),
                        pltpu.VMEM((1, PAGE_LANES), F32)])
    return pl.pallas_call(
        functools.partial(_ffn_attn_kernel, n_pg=n_pg, steps_per_seq=steps_per_seq),
        grid_spec=grid_spec,
        out_shape=(jax.ShapeDtypeStruct((nb, l, D_MODEL), F32),
                   jax.ShapeDtypeStruct((nbs, rows, HEAD_DIM), F32)),
        compiler_params=_cparams(("arbitrary", "arbitrary", "arbitrary")),
        name="ffn_attn",
    )(page_table, h2, x1, gate2, ln_g.reshape(1, D_MODEL), ln_b.reshape(1, D_MODEL), w_up, w_down,
      q, k_new, v_new, lf_new, *([cache_k] * n_pg), *([cache_v] * n_pg), *([decay_wt] * n_pg))


def kernel(x_prompt, x_sample, c_prompt, c_sample, cache_k, cache_v, cache_logf, state_ssm_re,
           state_ssm_im, page_table, w_ada, b_ada, w_in, b_f, w_o, a_re, a_im, log_dt, b_re, b_im,
           c_re, c_im, d_skip, w_glu, b_glu, ln1_g, ln1_b, w_up, w_down, ln2_g, ln2_b):
    assert w_ada.shape[0] == DEPTH == 1
    nbp, s_len, _ = x_prompt.shape
    nbs, n_q, _ = x_sample.shape
    n_seq = nbp + nbs

    c_all = jnp.concatenate([c_prompt, c_sample, jnp.zeros((16 - n_seq, D_MODEL), F32)], axis=0)
    mod = _ada(c_all, w_ada[0], b_ada[0]).reshape(16, N_MOD, D_MODEL)
    mod_p = [mod[:nbp, i][:, None, :] for i in range(N_MOD)]
    mod_s = [jnp.repeat(mod[nbp:n_seq, i], n_q, axis=0)[None] for i in range(N_MOD)]

    a = ATT_WIDTH
    w_in_t = w_in[0].T
    w_qkv = w_in_t[:3 * a].astype(BF16)
    w_u = w_in_t[3 * a + N_HEADS:].astype(BF16)
    w_ft = w_in_t[3 * a:3 * a + N_HEADS].astype(BF16)
    w_f = jnp.pad(w_ft, ((0, LANE - N_HEADS), (0, 0)))
    w_o_b = w_o[0].astype(BF16)
    w_up_b = w_up[0].astype(BF16)
    w_down_b = w_down[0].astype(BF16)

    l_re, l_im, bb_re, bb_im = _s5prep(a_re[0], a_im[0], log_dt[0], b_re[0], b_im[0])
    wb, wc, wg, bg, dsk, lam = _s5_weights(l_re, l_im, bb_re, bb_im, c_re[0], c_im[0], d_skip[0],
                                           w_glu[0], b_glu[0])

    rows = nbs * n_q
    xs = x_sample.reshape(1, rows, D_MODEL)
    qs, ks, _, vs, _, us, lfcs, _ = _inproj(xs, mod_s[0], mod_s[1], w_qkv, w_u, w_f, w_ft, b_f[0], tm=rows)
    n_pool = cache_k.shape[1]
    decay_wt = _decay(cache_logf[0].reshape(n_pool, PAGE_LANES))
    per_head = lambda t: t.reshape(nbs, n_q * N_HEADS, HEAD_DIM)
    lf_new = jnp.pad(lfcs.reshape(nbs, 1, n_q * N_HEADS), ((0, 0), (0, 0), (0, LANE - n_q * N_HEADS)))

    q, k, k_b, v, v_b, u, lfc, lfr = _inproj(x_prompt, mod_p[0], mod_p[1], w_qkv, w_u, w_f, w_ft, b_f[0], tm=512)
    att = _attn(q, k_b, v_b, _cumsum(lfr))
    ssm, hp_re, hp_im = _s5(u, wb, wc, wg, bg, dsk, lam)
    x1, h2 = _outproj(att, ssm, x_prompt, mod_p[2], mod_p[4], mod_p[3], ln1_g[0], ln1_b[0], w_o_b, tm=512)
    y_p, att_s = _ffn_attn(h2, x1, mod_p[5], ln2_g[0], ln2_b[0], w_up_b, w_down_b, 512,
                           page_table, per_head(qs), per_head(ks), per_head(vs), lf_new, cache_k, cache_v,
                           decay_wt)

    u_tm = us.reshape(nbs, n_q, SSM_WIDTH).transpose(1, 0, 2).reshape(rows, SSM_WIDTH)
    h0 = jnp.concatenate([state_ssm_re[0].reshape(nbs, N_GBLK, ST_LANES),
                          state_ssm_im[0].reshape(nbs, N_GBLK, ST_LANES)], axis=2).transpose(1, 0, 2)
    ssm_tm, hs_fin = _s5s(u_tm, h0, wb, wc, wg, bg, dsk, lam, nbs)
    ssm_s = ssm_tm.reshape(n_q, nbs, SSM_WIDTH).transpose(1, 0, 2).reshape(1, rows, SSM_WIDTH)
    hs_fin = hs_fin.transpose(1, 0, 2).reshape(nbs, N_GBLK, 2, GROUPS_PER_BLOCK, SSM_STATE)
    hs_re = hs_fin[:, :, 0].reshape(nbs, N_GROUPS, SSM_STATE)
    hs_im = hs_fin[:, :, 1].reshape(nbs, N_GROUPS, SSM_STATE)
    x1s, h2s = _outproj(att_s.reshape(1, rows, a).astype(BF16), ssm_s, xs, mod_s[2], mod_s[4], mod_s[3],
                        ln1_g[0], ln1_b[0], w_o_b, tm=rows)
    y_s = _ffn(h2s, x1s, mod_s[5], ln2_g[0], ln2_b[0], w_up_b, w_down_b, tm=rows)

    hd = (N_HEADS, HEAD_DIM)
    return (y_p, y_s.reshape(nbs, n_q, D_MODEL),
            k.reshape(1, nbp, s_len, *hd), v.reshape(1, nbp, s_len, *hd), lfc[None],
            hp_re[None], hp_im[None],
            ks.reshape(1, nbs, n_q, *hd), vs.reshape(1, nbs, n_q, *hd), lfcs.reshape(1, nbs, n_q, N_HEADS),
            hs_re[None], hs_im[None])
```

```python
import functools
import math

import jax
import jax.numpy as jnp
from jax import lax
from jax.experimental import pallas as pl
from jax.experimental.pallas import tpu as pltpu

F32 = jnp.float32
BF16 = jnp.bfloat16

LANE = 128
D_MODEL = 2048
ATT_WIDTH = 1024
SSM_WIDTH = 1024
HEAD_DIM = 128
N_HEADS = 8
SSM_GROUP = 16
N_GROUPS = 64
SSM_STATE = 64
D_FF = 8192
N_MOD = 6
PAGE = 128
DEPTH = 1
ALPHA = (2 * DEPTH) ** 0.25
LN_EPS = 1e-5
LOG2E = math.log2(math.e)
QSCALE = HEAD_DIM ** -0.5 * LOG2E

GROUPS_PER_BLOCK = 8
N_GBLK = N_GROUPS // GROUPS_PER_BLOCK
ST_LANES = GROUPS_PER_BLOCK * SSM_STATE
CH_LANES = GROUPS_PER_BLOCK * SSM_GROUP

VMEM_LIMIT = 58 * 1024 * 1024


def _cparams(sem):
    return pltpu.CompilerParams(dimension_semantics=sem, vmem_limit_bytes=VMEM_LIMIT)


def _ln(x):
    mu = jnp.mean(x, axis=-1, keepdims=True)
    xc = x - mu
    var = jnp.mean(xc * xc, axis=-1, keepdims=True)
    return xc * lax.rsqrt(var + LN_EPS)


def _log_sigmoid(x):
    return jnp.minimum(x, 0.0) - jnp.log1p(jnp.exp(-jnp.abs(x)))


def _nt_dot(a, b):
    return lax.dot_general(a, b, (((1,), (1,)), ((), ())), preferred_element_type=F32)


def _softmax_weights(m, l, s):
    m_new = jnp.maximum(m, jnp.max(s, axis=1, keepdims=True))
    alpha = jnp.exp2(m - m_new)
    p = jnp.exp2(s - m_new)
    return m_new, alpha, p.astype(BF16), alpha * l + jnp.sum(p, axis=1, keepdims=True)


def _softmax_step(carry, s, v):
    m, l, acc = carry
    m_new, alpha, p, l_new = _softmax_weights(m, l, s)
    return m_new, l_new, alpha * acc + jnp.dot(p, v, preferred_element_type=F32)


def _ada_kernel(c_ref, w_ref, b_ref, o_ref):
    c = c_ref[...]
    s = c * jax.nn.sigmoid(c)
    o_ref[...] = jnp.dot(s, w_ref[...], preferred_element_type=F32) + b_ref[...]


def _ada(c_all, w_ada, b_ada):
    rows = c_all.shape[0]
    n = w_ada.shape[1]
    tn = 1024
    return pl.pallas_call(
        _ada_kernel,
        grid=(n // tn,),
        in_specs=[pl.BlockSpec((rows, D_MODEL), lambda j: (0, 0)),
                  pl.BlockSpec((D_MODEL, tn), lambda j: (0, j)),
                  pl.BlockSpec((1, tn), lambda j: (0, j))],
        out_specs=pl.BlockSpec((rows, tn), lambda j: (0, j)),
        out_shape=jax.ShapeDtypeStruct((rows, n), F32),
        compiler_params=_cparams(("arbitrary",)),
        name="ada",
    )(c_all, w_ada, b_ada.reshape(1, n))


def _s5prep_kernel(are_ref, aim_ref, ldt_ref, arx_ref, aix_ref, ldx_ref, bre_ref, bim_ref,
                   lre_ref, lim_ref, bbre_ref, bbim_ref):
    def lam(a_re, a_im, log_dt):
        dt = jnp.exp(log_dt)
        mag = jnp.exp(a_re * dt)
        return mag * jnp.cos(a_im * dt), mag * jnp.sin(a_im * dt)

    l_re, l_im = lam(are_ref[...], aim_ref[...], ldt_ref[...])
    lre_ref[...] = l_re
    lim_ref[...] = l_im
    a_re, a_im = arx_ref[...], aix_ref[...]
    x_re, x_im = lam(a_re, a_im, ldx_ref[...])
    den = a_re * a_re + a_im * a_im
    n_re = x_re - 1.0
    k_re = (n_re * a_re + x_im * a_im) / den
    k_im = (x_im * a_re - n_re * a_im) / den
    b_re, b_im = bre_ref[...], bim_ref[...]
    bbre_ref[...] = k_re * b_re - k_im * b_im
    bbim_ref[...] = k_re * b_im + k_im * b_re


def _s5prep(a_re, a_im, log_dt, b_re, b_im):
    g, p, c = b_re.shape
    ldt = jnp.broadcast_to(log_dt[:, None], (g, p))
    ex = lambda a: jnp.broadcast_to(a[:, :, None], (g, p, c)).reshape(g, p * c)
    small = jax.ShapeDtypeStruct((g, p), F32)
    big = jax.ShapeDtypeStruct((g, p * c), F32)
    l_re, l_im, bb_re, bb_im = pl.pallas_call(
        _s5prep_kernel, out_shape=(small, small, big, big), name="s5prep",
    )(a_re, a_im, ldt, ex(a_re), ex(a_im), ex(ldt), b_re.reshape(g, p * c), b_im.reshape(g, p * c))
    return l_re, l_im, bb_re.reshape(g, p, c), bb_im.reshape(g, p, c)


def _blockdiag(w):
    g, a, b = w.shape
    w = w.reshape(N_GBLK, GROUPS_PER_BLOCK, a, b)
    eye = jnp.eye(GROUPS_PER_BLOCK, dtype=w.dtype)
    return jnp.einsum('xgab,gh->xgahb', w, eye).reshape(N_GBLK, GROUPS_PER_BLOCK * a, GROUPS_PER_BLOCK * b)


def _s5_weights(l_re, l_im, bb_re, bb_im, c_re, c_im, d_skip, w_glu, b_glu):
    wb = jnp.concatenate([_blockdiag(jnp.swapaxes(bb_re, 1, 2)),
                          _blockdiag(jnp.swapaxes(bb_im, 1, 2))], axis=2)
    wc = jnp.concatenate([_blockdiag(jnp.swapaxes(c_re, 1, 2)),
                          _blockdiag(-jnp.swapaxes(c_im, 1, 2))], axis=1)
    wg = jnp.concatenate([_blockdiag(w_glu[:, :, :SSM_GROUP]),
                          _blockdiag(w_glu[:, :, SSM_GROUP:])], axis=2)
    bg = jnp.concatenate([b_glu[:, :SSM_GROUP].reshape(N_GBLK, 1, CH_LANES),
                          b_glu[:, SSM_GROUP:].reshape(N_GBLK, 1, CH_LANES)], axis=2)
    dsk = d_skip.reshape(N_GBLK, 1, CH_LANES)
    lam = jnp.concatenate([l_re.reshape(N_GBLK, 1, ST_LANES), l_im.reshape(N_GBLK, 1, ST_LANES)], axis=2)
    return wb, wc, wg, bg, dsk, lam


def _inproj_kernel(x_ref, sh_ref, sc_ref, w_ref, wu_ref, wf_ref, wft_ref, bf_ref, bfr_ref,
                   q_ref, kf_ref, kb_ref, vf_ref, vb_ref, u_ref, lfc_ref, lfr_ref, h_scr):
    j = pl.program_id(2)

    @pl.when(j == 0)
    def _():
        h = _ln(x_ref[0]) * (1.0 + sc_ref[0]) + sh_ref[0]
        hb = h.astype(BF16)
        h_scr[...] = hb
        fl = _nt_dot(hb, wf_ref[...]) + bf_ref[...]
        lfc_ref[0] = _log_sigmoid(fl)[:, :N_HEADS]
        lfr_ref[0] = _log_sigmoid(_nt_dot(wft_ref[...], hb) + bfr_ref[...])

    def proj(w):
        return _nt_dot(h_scr[...], w[...])

    @pl.when(j == 0)
    def _():
        q_ref[0] = (proj(w_ref) * QSCALE).astype(BF16)

    @pl.when(j == 1)
    def _():
        acc = proj(w_ref)
        kf_ref[0] = acc
        kb_ref[0] = acc.astype(BF16)

    @pl.when(j == 2)
    def _():
        acc = proj(w_ref)
        vf_ref[0] = acc
        vb_ref[0] = acc.astype(BF16)

    @pl.when(j == 3)
    def _():
        u_ref[0] = proj(wu_ref)


def _inproj(x, shift, scale, w_qkv, w_u, w_f, w_ft, b_f, tm):
    nb, l, _ = x.shape
    r = shift.shape[1]
    mod_map = (lambda b, i, j: (b, 0, 0)) if r == 1 else (lambda b, i, j: (b, i, 0))
    row_map = lambda b, i, j: (b, i, 0)
    const2 = lambda b, i, j: (0, 0)
    wide = lambda dt: jax.ShapeDtypeStruct((nb, l, ATT_WIDTH), dt)
    return pl.pallas_call(
        _inproj_kernel,
        grid=(nb, l // tm, 4),
        in_specs=[pl.BlockSpec((1, tm, D_MODEL), row_map),
                  pl.BlockSpec((1, r, D_MODEL), mod_map),
                  pl.BlockSpec((1, r, D_MODEL), mod_map),
                  pl.BlockSpec((ATT_WIDTH, D_MODEL), lambda b, i, j: (jnp.minimum(j, 2), 0)),
                  pl.BlockSpec((SSM_WIDTH, D_MODEL), const2),
                  pl.BlockSpec((LANE, D_MODEL), const2),
                  pl.BlockSpec((N_HEADS, D_MODEL), const2),
                  pl.BlockSpec((1, LANE), const2),
                  pl.BlockSpec((N_HEADS, 1), const2)],
        out_specs=[pl.BlockSpec((1, tm, ATT_WIDTH), row_map)] * 6
                  + [pl.BlockSpec((1, tm, N_HEADS), row_map),
                     pl.BlockSpec((1, N_HEADS, tm), lambda b, i, j: (b, 0, i))],
        out_shape=(wide(BF16), wide(F32), wide(BF16), wide(F32), wide(BF16), wide(F32),
                   jax.ShapeDtypeStruct((nb, l, N_HEADS), F32),
                   jax.ShapeDtypeStruct((nb, N_HEADS, l), F32)),
        scratch_shapes=[pltpu.VMEM((tm, D_MODEL), BF16)],
        compiler_params=_cparams(("arbitrary", "arbitrary", "arbitrary")),
        name="inproj",
    )(x, shift, scale, w_qkv, w_u, w_f, w_ft, jnp.pad(b_f, (0, LANE - N_HEADS)).reshape(1, LANE),
      b_f.reshape(N_HEADS, 1))


CUM_BLK = 512


def _cumsum_kernel(lfr_ref, fr_ref):
    s_len = lfr_ref.shape[2]
    r = lax.broadcasted_iota(jnp.int32, (CUM_BLK, CUM_BLK), 0)
    c = lax.broadcasted_iota(jnp.int32, (CUM_BLK, CUM_BLK), 1)
    tri_u = (r <= c).astype(F32)
    carry = jnp.zeros((N_HEADS, 1), F32)
    for blk in range(s_len // CUM_BLK):
        cols = slice(blk * CUM_BLK, (blk + 1) * CUM_BLK)
        fr = jnp.dot(lfr_ref[0, :, cols], tri_u, precision=lax.Precision.HIGHEST,
                     preferred_element_type=F32) + carry
        fr_ref[0, :, cols] = fr * LOG2E
        carry = fr[:, CUM_BLK - 1:CUM_BLK]


def _cumsum(lfr):
    nb, _, s_len = lfr.shape
    rspec = pl.BlockSpec((1, N_HEADS, s_len), lambda b: (b, 0, 0))
    return pl.pallas_call(
        _cumsum_kernel, grid=(nb,), in_specs=[rspec], out_specs=rspec,
        out_shape=jax.ShapeDtypeStruct(lfr.shape, F32),
        compiler_params=_cparams(("arbitrary",)), name="cumsum",
    )(lfr)


ATT_BLK = 512


def _attn_kernel(q_ref, k_ref, v_ref, fr_ref, o_ref, s_scr):
    h = pl.program_id(1)
    qi = pl.program_id(2)
    t = ATT_BLK
    q = q_ref[0]

    def scores(j):
        start = pl.multiple_of(j * t, t)
        fk = fr_ref[0, pl.ds(h, 1), pl.ds(start, t)]
        return _nt_dot(q, k_ref[0, pl.ds(start, t), :]) - fk

    def values(j):
        return v_ref[0, pl.ds(pl.multiple_of(j * t, t), t), :]

    def stage(j, carry, slot):
        s_scr[1 - slot] = scores(j + 1)
        return _softmax_step(carry, s_scr[slot], values(j))

    def pair(jj, carry):
        return stage(2 * jj + 1, stage(2 * jj, carry, 0), 1)

    def diagonal(carry, slot):
        row = lax.broadcasted_iota(jnp.int32, (t, t), 0)
        col = lax.broadcasted_iota(jnp.int32, (t, t), 1)
        return _softmax_step(carry, jnp.where(col <= row, s_scr[slot], -jnp.inf), values(qi))

    s_scr[0] = scores(0)
    init = (jnp.full((t, 1), -jnp.inf, F32), jnp.zeros((t, 1), F32), jnp.zeros((t, HEAD_DIM), F32))
    carry = lax.fori_loop(0, qi // 2, pair, init)
    m, l, acc = lax.cond(qi % 2 == 1,
                         lambda c: diagonal(stage(qi - 1, c, 0), 1),
                         lambda c: diagonal(c, 0), carry)
    o_ref[0] = (acc / l).astype(BF16)


def _attn(q, k, v, fr):
    nb, s_len, _ = q.shape
    t = ATT_BLK
    qspec = pl.BlockSpec((1, t, HEAD_DIM), lambda b, h, i: (b, i, h))
    kvspec = pl.BlockSpec((1, s_len, HEAD_DIM), lambda b, h, i: (b, 0, h))
    return pl.pallas_call(
        _attn_kernel,
        grid=(nb, N_HEADS, s_len // t),
        in_specs=[qspec, kvspec, kvspec,
                  pl.BlockSpec((1, N_HEADS, s_len), lambda b, h, i: (b, 0, 0))],
        out_specs=qspec,
        out_shape=jax.ShapeDtypeStruct((nb, s_len, ATT_WIDTH), BF16),
        scratch_shapes=[pltpu.VMEM((2, t, t), F32)],
        compiler_params=_cparams(("arbitrary", "arbitrary", "arbitrary")),
        name="attn",
    )(q, k, v, fr)


PAGE_LANES = PAGE * N_HEADS


def _decay_kernel(lf_ref, wt_ref):
    x = lf_ref[...]
    n = x.shape[1]
    lane = lax.broadcasted_iota(jnp.int32, x.shape, 1)
    suffix, total = x, x
    k = N_HEADS
    while k < n:
        suffix = suffix + jnp.where(lane < n - k, pltpu.roll(suffix, n - k, axis=1), 0.0)
        total = total + pltpu.roll(total, k, axis=1)
        k *= 2
    wt_ref[:, 0, :] = suffix - x
    wt_ref[:, 1, :] = total


def _decay(lf_flat):
    n_pool, n = lf_flat.shape
    rb = 256
    return pl.pallas_call(
        _decay_kernel,
        grid=(n_pool // rb,),
        in_specs=[pl.BlockSpec((rb, n), lambda i: (i, 0))],
        out_specs=pl.BlockSpec((rb, 2, n), lambda i: (i, 0, 0)),
        out_shape=jax.ShapeDtypeStruct((n_pool, 2, n), F32),
        compiler_params=_cparams(("arbitrary",)),
        name="decay",
    )(lf_flat)


def _attn_s_init(q, kn_ref, vn_ref, lfn_ref, bias_scr, m_scr, l_scr, acc_scr, carry_scr):
    rows = q.shape[0]
    f = jnp.broadcast_to(lfn_ref[0], (N_HEADS, LANE))
    lane8 = lax.broadcasted_iota(jnp.int32, f.shape, 1)
    k = N_HEADS
    while k < rows:
        f = f + jnp.where(lane8 >= k, pltpu.roll(f, k, axis=1), 0.0)
        k *= 2
    f_row = f[0:1, :] * LOG2E
    r = lax.broadcasted_iota(jnp.int32, (rows, LANE), 0)
    c = lax.broadcasted_iota(jnp.int32, (rows, LANE), 1)
    rr = lax.broadcasted_iota(jnp.int32, (rows, PAGE_LANES), 0)
    cc = lax.broadcasted_iota(jnp.int32, (rows, PAGE_LANES), 1)
    same_head = jnp.bitwise_and(rr, N_HEADS - 1) == jnp.bitwise_and(cc, N_HEADS - 1)
    bias_scr[...] = jnp.where(same_head, 0.0, -jnp.inf)
    pad = jnp.zeros((LANE - rows, HEAD_DIM), BF16)
    kn = jnp.concatenate([kn_ref[0].astype(BF16), pad], axis=0)
    vn = jnp.concatenate([vn_ref[0].astype(BF16), pad], axis=0)
    tok_r = jnp.right_shift(r, 3)
    tok_c = jnp.right_shift(c, 3)
    head_ok = jnp.bitwise_and(r, N_HEADS - 1) == jnp.bitwise_and(c, N_HEADS - 1)
    s = _nt_dot(q, kn) - f_row
    s = jnp.where(head_ok, jnp.where(tok_c <= tok_r, s, -jnp.inf), -jnp.inf)
    init = (jnp.full((rows, 1), -jnp.inf, F32), jnp.zeros((rows, 1), F32), jnp.zeros((rows, HEAD_DIM), F32))
    m_scr[...], l_scr[...], acc_scr[...] = _softmax_step(init, s, vn)
    carry_scr[...] = jnp.zeros_like(carry_scr)


def _attn_s_scores(q, kp_refs, wt_refs, bias_scr, carry_scr):
    carry = carry_scr[...]
    bias = bias_scr[...]
    scores = []
    for kp_ref, wt_ref in zip(kp_refs, wt_refs):
        wt = wt_ref[0]
        dec = (wt[0:1, :] + carry) * LOG2E
        carry = carry + wt[1:2, :]
        k = kp_ref[...].reshape(PAGE_LANES, HEAD_DIM).astype(BF16)
        scores.append(_nt_dot(q, k) + bias + dec)
    carry_scr[...] = carry
    return jnp.concatenate(scores, axis=1)


def _attn_s_values(vp_refs):
    return jnp.concatenate([vp_ref[...].reshape(PAGE_LANES, HEAD_DIM).astype(BF16) for vp_ref in vp_refs], axis=0)


N_SEG = 8
S5_TB = 64
S5_PITCH = 520


def _cmul(ar, ai, br, bi):
    return ar * br - ai * bi, ar * bi + ai * br


def _s5_tail(h, u, wc, dsk, wg, bg):
    y = jnp.dot(h.astype(BF16), wc, preferred_element_type=F32) + dsk * u
    z = jax.nn.gelu(y)
    zz = jnp.dot(z.astype(BF16), wg, preferred_element_type=F32) + bg
    return zz[:, :CH_LANES] * jax.nn.sigmoid(zz[:, CH_LANES:])


def _s5_kernel(u_ref, wb_ref, wc_ref, wg_ref, bg_ref, dsk_ref, lam_ref, y_ref, hfin_ref, ubuf, ybuf):
    s_len = u_ref.shape[1]
    seg = s_len // N_SEG
    n_tb = seg // S5_TB
    assert seg <= S5_PITCH and seg % S5_TB == 0
    for i in range(N_SEG):
        ubuf[i * S5_PITCH:i * S5_PITCH + seg, :] = u_ref[0, i * seg:(i + 1) * seg, :]
    wb = wb_ref[0]
    lam = lam_ref[0]
    l_re = jnp.broadcast_to(lam[:, :ST_LANES], (N_SEG, ST_LANES))
    l_im = jnp.broadcast_to(lam[:, ST_LANES:], (N_SEG, ST_LANES))

    def load_block(tb):
        t0 = tb * S5_TB
        u = jnp.concatenate([ubuf[pl.ds(t0 + j, N_SEG, stride=S5_PITCH), :] for j in range(S5_TB)], axis=0)
        return u, jnp.dot(u.astype(BF16), wb, preferred_element_type=F32)

    def scan_block(bu, h_re, h_im, keep):
        hs = []
        for j in range(S5_TB):
            n_re, n_im = _cmul(l_re, l_im, h_re, h_im)
            h_re = n_re + bu[j * N_SEG:(j + 1) * N_SEG, :ST_LANES]
            h_im = n_im + bu[j * N_SEG:(j + 1) * N_SEG, ST_LANES:]
            if keep:
                hs.append(jnp.concatenate([h_re, h_im], axis=1))
        return h_re, h_im, hs

    def pass1(tb, carry):
        _, bu = load_block(tb)
        h_re, h_im, _ = scan_block(bu, *carry, keep=False)
        return h_re, h_im

    zero = jnp.zeros((N_SEG, ST_LANES), F32)
    f_re, f_im = lax.fori_loop(0, n_tb, pass1, (zero, zero))
    p_re, p_im = lam[:, :ST_LANES], lam[:, ST_LANES:]
    n_sq = seg.bit_length() - 1
    assert 1 << n_sq == seg
    for _ in range(n_sq):
        p_re, p_im = _cmul(p_re, p_im, p_re, p_im)
    c_re, c_im = [jnp.zeros((1, ST_LANES), F32)], [jnp.zeros((1, ST_LANES), F32)]
    for i in range(1, N_SEG):
        d_re, d_im = _cmul(p_re, p_im, c_re[-1], c_im[-1])
        c_re.append(f_re[i - 1:i, :] + d_re)
        c_im.append(f_im[i - 1:i, :] + d_im)
    wc, wg, bg, dsk = wc_ref[0], wg_ref[0], bg_ref[0], dsk_ref[0]

    def pass2(tb, carry):
        u, bu = load_block(tb)
        h_re, h_im, hs = scan_block(bu, *carry, keep=True)
        y = _s5_tail(jnp.concatenate(hs, axis=0), u, wc, dsk, wg, bg)
        t0 = tb * S5_TB
        for j in range(S5_TB):
            ybuf[pl.ds(t0 + j, N_SEG, stride=S5_PITCH), :] = y[j * N_SEG:(j + 1) * N_SEG, :]
        return h_re, h_im

    h_re, h_im = lax.fori_loop(0, n_tb, pass2,
                               (jnp.concatenate(c_re, axis=0), jnp.concatenate(c_im, axis=0)))
    hfin_ref[0, 0] = jnp.concatenate([h_re[N_SEG - 1:, :], h_im[N_SEG - 1:, :]], axis=1)
    for i in range(N_SEG):
        y_ref[0, i * seg:(i + 1) * seg, :] = ybuf[i * S5_PITCH:i * S5_PITCH + seg, :].astype(BF16)


def _s5(u, wb, wc, wg, bg, dsk, lam):
    nb, s_len, _ = u.shape
    blk = lambda shape: pl.BlockSpec((1,) + shape, lambda b, g: (g, 0, 0))
    y, hfin = pl.pallas_call(
        _s5_kernel,
        grid=(nb, N_GBLK),
        in_specs=[pl.BlockSpec((1, s_len, CH_LANES), lambda b, g: (b, 0, g)),
                  blk((CH_LANES, 2 * ST_LANES)), blk((2 * ST_LANES, CH_LANES)),
                  blk((CH_LANES, 2 * CH_LANES)), blk((1, 2 * CH_LANES)), blk((1, CH_LANES)),
                  blk((1, 2 * ST_LANES))],
        out_specs=[pl.BlockSpec((1, s_len, CH_LANES), lambda b, g: (b, 0, g)),
                   pl.BlockSpec((1, 1, 1, 2 * ST_LANES), lambda b, g: (b, g, 0, 0))],
        out_shape=(jax.ShapeDtypeStruct((nb, s_len, SSM_WIDTH), BF16),
                   jax.ShapeDtypeStruct((nb, N_GBLK, 1, 2 * ST_LANES), F32)),
        scratch_shapes=[pltpu.VMEM((N_SEG * S5_PITCH, CH_LANES), F32),
                        pltpu.VMEM((N_SEG * S5_PITCH, CH_LANES), F32)],
        compiler_params=_cparams(("arbitrary", "arbitrary")),
        name="s5",
    )(u, wb.astype(BF16), wc.astype(BF16), wg.astype(BF16), bg, dsk, lam)
    hfin = hfin.reshape(nb, N_GBLK, 2, GROUPS_PER_BLOCK, SSM_STATE)
    return y, hfin[:, :, 0].reshape(nb, N_GROUPS, SSM_STATE), hfin[:, :, 1].reshape(nb, N_GROUPS, SSM_STATE)


def _s5s_kernel(u_ref, h0_ref, wb_ref, wc_ref, wg_ref, bg_ref, dsk_ref, lam_ref, y_ref, hfin_ref, *, nb):
    hi = lax.Precision.HIGHEST
    u = u_ref[...]
    n_t = u.shape[0] // nb
    bu = jnp.dot(u, wb_ref[0], precision=hi, preferred_element_type=F32)
    lam = lam_ref[0]
    l_re = jnp.broadcast_to(lam[:, :ST_LANES], (nb, ST_LANES))
    l_im = jnp.broadcast_to(lam[:, ST_LANES:], (nb, ST_LANES))
    h0 = h0_ref[0]
    h_re, h_im = h0[:, :ST_LANES], h0[:, ST_LANES:]
    hs = []
    for t in range(n_t):
        n_re, n_im = _cmul(l_re, l_im, h_re, h_im)
        h_re = n_re + bu[t * nb:(t + 1) * nb, :ST_LANES]
        h_im = n_im + bu[t * nb:(t + 1) * nb, ST_LANES:]
        hs.append(jnp.concatenate([h_re, h_im], axis=1))
    hfin_ref[0] = hs[-1]
    h = jnp.concatenate(hs, axis=0)
    y = jnp.dot(h, wc_ref[0], precision=hi, preferred_element_type=F32) + dsk_ref[0] * u
    z = jax.nn.gelu(y)
    zz = jnp.dot(z, wg_ref[0], precision=hi, preferred_element_type=F32) + bg_ref[0]
    y_ref[...] = (zz[:, :CH_LANES] * jax.nn.sigmoid(zz[:, CH_LANES:])).astype(BF16)


def _s5s(u_tm, h0, wb, wc, wg, bg, dsk, lam, nb):
    rows = u_tm.shape[0]
    blk = lambda shape: pl.BlockSpec((1,) + shape, lambda g: (g, 0, 0))
    return pl.pallas_call(
        functools.partial(_s5s_kernel, nb=nb),
        grid=(N_GBLK,),
        in_specs=[pl.BlockSpec((rows, CH_LANES), lambda g: (0, g)),
                  blk((nb, 2 * ST_LANES)),
                  blk((CH_LANES, 2 * ST_LANES)), blk((2 * ST_LANES, CH_LANES)),
                  blk((CH_LANES, 2 * CH_LANES)), blk((1, 2 * CH_LANES)), blk((1, CH_LANES)),
                  blk((1, 2 * ST_LANES))],
        out_specs=[pl.BlockSpec((rows, CH_LANES), lambda g: (0, g)), blk((nb, 2 * ST_LANES))],
        out_shape=(jax.ShapeDtypeStruct((rows, SSM_WIDTH), BF16),
                   jax.ShapeDtypeStruct((N_GBLK, nb, 2 * ST_LANES), F32)),
        compiler_params=_cparams(("arbitrary",)),
        name="s5s",
    )(u_tm, h0, wb, wc, wg, bg, dsk, lam)


def _outproj_kernel(att_ref, ssm_ref, x_ref, g1_ref, sc2_ref, sh2_ref, lng_ref, lnb_ref, wo_ref,
                    x1_ref, h2_ref):
    tm = x_ref.shape[1]
    n_sub = 2 if tm >= 512 else 1
    sub = tm // n_sub

    def mod_rows(ref, rows):
        return ref[0] if ref.shape[1] == 1 else ref[0, rows, :]

    for c in range(n_sub):
        rows = slice(c * sub, (c + 1) * sub)
        mix = (jnp.dot(att_ref[0, rows, :], wo_ref[:ATT_WIDTH, :], preferred_element_type=F32)
               + jnp.dot(ssm_ref[0, rows, :], wo_ref[ATT_WIDTH:, :], preferred_element_type=F32))
        x1 = (_ln(ALPHA * x_ref[0, rows, :] + (1.0 + mod_rows(g1_ref, rows)) * mix) * lng_ref[...]
              + lnb_ref[...])
        x1_ref[0, rows, :] = x1
        h2_ref[0, rows, :] = (_ln(x1) * (1.0 + mod_rows(sc2_ref, rows)) + mod_rows(sh2_ref, rows)).astype(BF16)


def _outproj(att, ssm, x, gate1, scale2, shift2, ln_g, ln_b, w_o, tm):
    nb, l, _ = x.shape
    r = gate1.shape[1]
    mod_map = (lambda b, i: (b, 0, 0)) if r == 1 else (lambda b, i: (b, i, 0))
    row_map = lambda b, i: (b, i, 0)
    const2 = lambda b, i: (0, 0)
    half = pl.BlockSpec((1, tm, ATT_WIDTH), row_map)
    full = pl.BlockSpec((1, tm, D_MODEL), row_map)
    mod = pl.BlockSpec((1, r, D_MODEL), mod_map)
    vec = pl.BlockSpec((1, D_MODEL), const2)
    return pl.pallas_call(
        _outproj_kernel,
        grid=(nb, l // tm),
        in_specs=[half, half, full, mod, mod, mod, vec, vec,
                  pl.BlockSpec((D_MODEL, D_MODEL), const2)],
        out_specs=[full, full],
        out_shape=(jax.ShapeDtypeStruct((nb, l, D_MODEL), F32),
                   jax.ShapeDtypeStruct((nb, l, D_MODEL), BF16)),
        compiler_params=_cparams(("arbitrary", "arbitrary")),
        name="outproj",
    )(att, ssm, x, gate1, scale2, shift2, ln_g.reshape(1, D_MODEL), ln_b.reshape(1, D_MODEL), w_o)


FFN_TF = 1024
FUSED_TF = 512
FUSED_TM = 1024


def _ffn_zero(f, acc_scr):
    @pl.when(f == 0)
    def _():
        acc_scr[...] = jnp.zeros_like(acc_scr)


def _ffn_accumulate(h2_ref, wut_ref, wd_ref, acc_scr):
    up = _nt_dot(h2_ref[0], wut_ref[...])
    act = jnp.square(jnp.maximum(up, 0.0)).astype(BF16)
    acc_scr[...] += jnp.dot(act, wd_ref[...], preferred_element_type=F32)


def _ffn_finish(f, x1_ref, g2_ref, lng_ref, lnb_ref, y_ref, acc_scr):
    @pl.when(f == pl.num_programs(2) - 1)
    def _():
        y_ref[0] = _ln(ALPHA * x1_ref[0] + (1.0 + g2_ref[0]) * acc_scr[...]) * lng_ref[...] + lnb_ref[...]


def _ffn_kernel(h2_ref, x1_ref, g2_ref, lng_ref, lnb_ref, wut_ref, wd_ref, y_ref, acc_scr):
    f = pl.program_id(2)
    _ffn_zero(f, acc_scr)
    _ffn_accumulate(h2_ref, wut_ref, wd_ref, acc_scr)
    _ffn_finish(f, x1_ref, g2_ref, lng_ref, lnb_ref, y_ref, acc_scr)


def _ffn_specs(gate2, tm, tf, single_buffer_residual):
    r = gate2.shape[1]
    wrap = lambda fn: (lambda b, i, f, *_: fn(b, i, f))
    row_map = wrap(lambda b, i, f: (b, i, 0))
    full = pl.BlockSpec((1, tm, D_MODEL), row_map)
    once = pl.BlockSpec((1, tm, D_MODEL), row_map, pipeline_mode=pl.Buffered(1)) if single_buffer_residual else full
    vec = pl.BlockSpec((1, D_MODEL), wrap(lambda b, i, f: (0, 0)))
    mod = pl.BlockSpec((1, r, D_MODEL), wrap((lambda b, i, f: (b, 0, 0)) if r == 1 else (lambda b, i, f: (b, i, 0))))
    weight = pl.BlockSpec((tf, D_MODEL), wrap(lambda b, i, f: (f, 0)))
    return [once, once, mod, vec, vec, weight, weight], once


def _ffn(h2, x1, gate2, ln_g, ln_b, w_up_t, w_down, tm):
    nb, l, _ = x1.shape
    in_specs, out_spec = _ffn_specs(gate2, tm, FFN_TF, False)
    return pl.pallas_call(
        _ffn_kernel,
        grid=(nb, l // tm, D_FF // FFN_TF),
        in_specs=in_specs,
        out_specs=out_spec,
        out_shape=jax.ShapeDtypeStruct((nb, l, D_MODEL), F32),
        scratch_shapes=[pltpu.VMEM((tm, D_MODEL), F32)],
        compiler_params=_cparams(("arbitrary", "arbitrary", "arbitrary")),
        name="ffn",
    )(h2, x1, gate2, ln_g.reshape(1, D_MODEL), ln_b.reshape(1, D_MODEL), w_up_t, w_down)


def _ffn_attn_kernel(pt_ref, h2_ref, x1_ref, g2_ref, lng_ref, lnb_ref, wut_ref, wd_ref,
                     q_ref, kn_ref, vn_ref, lfn_ref, *refs, n_pg, steps_per_seq):
    kp_refs, vp_refs, wt_refs = refs[:n_pg], refs[n_pg:2 * n_pg], refs[2 * n_pg:3 * n_pg]
    y_ref, o_ref, bias_scr, m_scr, l_scr, a_scr, carry_scr = refs[3 * n_pg:]
    acc_scr = y_ref.at[0]
    f = pl.program_id(2)
    t = (pl.program_id(0) * pl.num_programs(1) + pl.program_id(1)) * pl.num_programs(2) + f
    si = lax.rem(t, steps_per_seq)
    q = q_ref[0]

    @pl.when(si == 0)
    def _():
        _attn_s_init(q, kn_ref, vn_ref, lfn_ref, bias_scr, m_scr, l_scr, a_scr, carry_scr)

    _ffn_zero(f, acc_scr)
    s = _attn_s_scores(q, kp_refs, wt_refs, bias_scr, carry_scr)
    up = _nt_dot(h2_ref[0], wut_ref[...])
    m_new, alpha, p, l_new = _softmax_weights(m_scr[...], l_scr[...], s)
    act = jnp.square(jnp.maximum(up, 0.0)).astype(BF16)
    acc_scr[...] += jnp.dot(act, wd_ref[...], preferred_element_type=F32)
    a_new = alpha * a_scr[...] + jnp.dot(p, _attn_s_values(vp_refs), preferred_element_type=F32)
    m_scr[...], l_scr[...], a_scr[...] = m_new, l_new, a_new

    @pl.when(si == steps_per_seq - 1)
    def _():
        o_ref[0] = a_new / l_new

    _ffn_finish(f, x1_ref, g2_ref, lng_ref, lnb_ref, y_ref, acc_scr)


def _ffn_attn(h2, x1, gate2, ln_g, ln_b, w_up_t, w_down, tm,
              page_table, q, k_new, v_new, lf_new, cache_k, cache_v, decay_wt):
    nb, l, _ = x1.shape
    nbs, rows, _ = q.shape
    n_pages = page_table.shape[1]
    n_i, n_f = l // tm, D_FF // FUSED_TF
    n_steps = nb * n_i * n_f
    n_pg = nbs * n_pages // n_steps
    steps_per_seq = n_pages // n_pg
    assert n_pg * n_steps == nbs * n_pages and steps_per_seq * n_pg == n_pages

    def step(b, i, f):
        t = (b * n_i + i) * n_f + f
        return lax.div(t, steps_per_seq), lax.rem(t, steps_per_seq)

    seq = lambda b, i, f, pt: (step(b, i, f)[0], 0, 0)

    def page_spec(block, n_lead, pg):
        def idx(b, i, f, pt):
            sq, si = step(b, i, f)
            return (0,) * n_lead + (pt[sq, n_pages - 1 - (si * n_pg + pg)],) + (0,) * (len(block) - n_lead - 1)
        return pl.BlockSpec(block, idx)

    ffn_specs, y_spec = _ffn_specs(gate2, tm, FUSED_TF, True)
    kv_block = (None, None, PAGE, N_HEADS, HEAD_DIM)
    tok_spec = pl.BlockSpec((1, rows, HEAD_DIM), seq)
    grid_spec = pltpu.PrefetchScalarGridSpec(
        num_scalar_prefetch=1,
        grid=(nb, n_i, n_f),
        in_specs=ffn_specs + [tok_spec, tok_spec, tok_spec, pl.BlockSpec((1, 1, LANE), seq)]
                 + [page_spec(kv_block, 1, pg) for pg in range(n_pg)]
                 + [page_spec(kv_block, 1, pg) for pg in range(n_pg)]
                 + [page_spec((1, 2, PAGE_LANES), 0, pg) for pg in range(n_pg)],
        out_specs=[y_spec, tok_spec],
        scratch_shapes=[pltpu.VMEM((rows, PAGE_LANES), F32),
                        pltpu.VMEM((rows, 1), F32),
                        pltpu.VMEM((rows, 1), F32),
                        pltpu.VMEM((rows, HEAD_DIM), F32),
                        pltpu.VMEM((1, PAGE_LANES), F32)])
    return pl.pallas_call(
        functools.partial(_ffn_attn_kernel, n_pg=n_pg, steps_per_seq=steps_per_seq),
        grid_spec=grid_spec,
        out_shape=(jax.ShapeDtypeStruct((nb, l, D_MODEL), F32),
                   jax.ShapeDtypeStruct((nbs, rows, HEAD_DIM), F32)),
        compiler_params=_cparams(("arbitrary", "arbitrary", "arbitrary")),
        name="ffn_attn",
    )(page_table, h2, x1, gate2, ln_g.reshape(1, D_MODEL), ln_b.reshape(1, D_MODEL), w_up_t, w_down,
      q, k_new, v_new, lf_new, *([cache_k] * n_pg), *([cache_v] * n_pg), *([decay_wt] * n_pg))


def kernel(x_prompt, x_sample, c_prompt, c_sample, cache_k, cache_v, cache_logf, state_ssm_re,
           state_ssm_im, page_table, w_ada, b_ada, w_in, b_f, w_o, a_re, a_im, log_dt, b_re, b_im,
           c_re, c_im, d_skip, w_glu, b_glu, ln1_g, ln1_b, w_up, w_down, ln2_g, ln2_b):
    assert w_ada.shape[0] == DEPTH == 1
    nbp, s_len, _ = x_prompt.shape
    nbs, n_q, _ = x_sample.shape
    n_seq = nbp + nbs

    c_all = jnp.concatenate([c_prompt, c_sample, jnp.zeros((16 - n_seq, D_MODEL), F32)], axis=0)
    mod = _ada(c_all, w_ada[0], b_ada[0]).reshape(16, N_MOD, D_MODEL)
    mod_p = [mod[:nbp, i][:, None, :] for i in range(N_MOD)]
    mod_s = [jnp.repeat(mod[nbp:n_seq, i], n_q, axis=0)[None] for i in range(N_MOD)]

    a = ATT_WIDTH
    w_in_t = w_in[0].T
    w_qkv = w_in_t[:3 * a].astype(BF16)
    w_u = w_in_t[3 * a + N_HEADS:].astype(BF16)
    w_ft = w_in_t[3 * a:3 * a + N_HEADS].astype(BF16)
    w_f = jnp.pad(w_ft, ((0, LANE - N_HEADS), (0, 0)))
    w_o_b = w_o[0].astype(BF16)
    w_up_b = w_up[0].T.astype(BF16)
    w_down_b = w_down[0].astype(BF16)

    l_re, l_im, bb_re, bb_im = _s5prep(a_re[0], a_im[0], log_dt[0], b_re[0], b_im[0])
    wb, wc, wg, bg, dsk, lam = _s5_weights(l_re, l_im, bb_re, bb_im, c_re[0], c_im[0], d_skip[0],
                                           w_glu[0], b_glu[0])

    rows = nbs * n_q
    xs = x_sample.reshape(1, rows, D_MODEL)
    qs, ks, _, vs, _, us, lfcs, _ = _inproj(xs, mod_s[0], mod_s[1], w_qkv, w_u, w_f, w_ft, b_f[0], tm=rows)
    n_pool = cache_k.shape[1]
    decay_wt = _decay(cache_logf[0].reshape(n_pool, PAGE_LANES))
    per_head = lambda t: t.reshape(nbs, n_q * N_HEADS, HEAD_DIM)
    lf_new = jnp.pad(lfcs.reshape(nbs, 1, n_q * N_HEADS), ((0, 0), (0, 0), (0, LANE - n_q * N_HEADS)))

    q, k, k_b, v, v_b, u, lfc, lfr = _inproj(x_prompt, mod_p[0], mod_p[1], w_qkv, w_u, w_f, w_ft, b_f[0], tm=512)
    att = _attn(q, k_b, v_b, _cumsum(lfr))
    ssm, hp_re, hp_im = _s5(u, wb, wc, wg, bg, dsk, lam)
    x1, h2 = _outproj(att, ssm, x_prompt, mod_p[2], mod_p[4], mod_p[3], ln1_g[0], ln1_b[0], w_o_b, tm=512)
    y_p, att_s = _ffn_attn(h2, x1, mod_p[5], ln2_g[0], ln2_b[0], w_up_b, w_down_b, FUSED_TM,
                           page_table, per_head(qs), per_head(ks), per_head(vs), lf_new, cache_k, cache_v,
                           decay_wt)

    u_tm = us.reshape(nbs, n_q, SSM_WIDTH).transpose(1, 0, 2).reshape(rows, SSM_WIDTH)
    h0 = jnp.concatenate([state_ssm_re[0].reshape(nbs, N_GBLK, ST_LANES),
                          state_ssm_im[0].reshape(nbs, N_GBLK, ST_LANES)], axis=2).transpose(1, 0, 2)
    ssm_tm, hs_fin = _s5s(u_tm, h0, wb, wc, wg, bg, dsk, lam, nbs)
    ssm_s = ssm_tm.reshape(n_q, nbs, SSM_WIDTH).transpose(1, 0, 2).reshape(1, rows, SSM_WIDTH)
    hs_fin = hs_fin.transpose(1, 0, 2).reshape(nbs, N_GBLK, 2, GROUPS_PER_BLOCK, SSM_STATE)
    hs_re = hs_fin[:, :, 0].reshape(nbs, N_GROUPS, SSM_STATE)
    hs_im = hs_fin[:, :, 1].reshape(nbs, N_GROUPS, SSM_STATE)
    x1s, h2s = _outproj(att_s.reshape(1, rows, a).astype(BF16), ssm_s, xs, mod_s[2], mod_s[4], mod_s[3],
                        ln1_g[0], ln1_b[0], w_o_b, tm=rows)
    y_s = _ffn(h2s, x1s, mod_s[5], ln2_g[0], ln2_b[0], w_up_b, w_down_b, tm=rows)

    hd = (N_HEADS, HEAD_DIM)
    return (y_p, y_s.reshape(nbs, n_q, D_MODEL),
            k.reshape(1, nbp, s_len, *hd), v.reshape(1, nbp, s_len, *hd), lfc[None],
            hp_re[None], hp_im[None],
            ks.reshape(1, nbs, n_q, *hd), vs.reshape(1, nbs, n_q, *hd), lfcs.reshape(1, nbs, n_q, N_HEADS),
            hs_re[None], hs_im[None])
```

```python
import functools
import math

import jax
import jax.numpy as jnp
from jax import lax
from jax.experimental import pallas as pl
from jax.experimental.pallas import tpu as pltpu

F32 = jnp.float32
BF16 = jnp.bfloat16

LANE = 128
D_MODEL = 2048
ATT_WIDTH = 1024
SSM_WIDTH = 1024
HEAD_DIM = 128
N_HEADS = 8
SSM_GROUP = 16
N_GROUPS = 64
SSM_STATE = 64
D_FF = 8192
N_MOD = 6
PAGE = 128
DEPTH = 1
ALPHA = (2 * DEPTH) ** 0.25
LN_EPS = 1e-5
LOG2E = math.log2(math.e)
QSCALE = HEAD_DIM ** -0.5 * LOG2E

GROUPS_PER_BLOCK = 8
N_GBLK = N_GROUPS // GROUPS_PER_BLOCK
ST_LANES = GROUPS_PER_BLOCK * SSM_STATE
CH_LANES = GROUPS_PER_BLOCK * SSM_GROUP

VMEM_LIMIT = 56 * 1024 * 1024


def _cparams(sem):
    return pltpu.CompilerParams(dimension_semantics=sem, vmem_limit_bytes=VMEM_LIMIT)


def _ln(x):
    mu = jnp.mean(x, axis=-1, keepdims=True)
    xc = x - mu
    var = jnp.mean(xc * xc, axis=-1, keepdims=True)
    return xc * lax.rsqrt(var + LN_EPS)


def _log_sigmoid(x):
    return jnp.minimum(x, 0.0) - jnp.log1p(jnp.exp(-jnp.abs(x)))


def _nt_dot(a, b):
    return lax.dot_general(a, b, (((1,), (1,)), ((), ())), preferred_element_type=F32)


def _softmax_weights(m, l, s):
    m_new = jnp.maximum(m, jnp.max(s, axis=1, keepdims=True))
    alpha = jnp.exp2(m - m_new)
    p = jnp.exp2(s - m_new)
    return m_new, alpha, p.astype(BF16), alpha * l + jnp.sum(p, axis=1, keepdims=True)


def _dot_row_halves(a, b):
    half = a.shape[0] // 2
    return jnp.concatenate([jnp.dot(a[:half], b, preferred_element_type=F32),
                            jnp.dot(a[half:], b, preferred_element_type=F32)], axis=0)


def _dot_depth_halves(a, b):
    half = a.shape[1] // 2
    return (jnp.dot(a[:, :half], b[:half], preferred_element_type=F32)
            + jnp.dot(a[:, half:], b[half:], preferred_element_type=F32))


def _softmax_step(carry, s, v):
    m, l, acc = carry
    m_new, alpha, p, l_new = _softmax_weights(m, l, s)
    return m_new, l_new, alpha * acc + jnp.dot(p, v, preferred_element_type=F32)


def _ada_kernel(c_ref, w_ref, b_ref, o_ref):
    c = c_ref[...]
    s = c * jax.nn.sigmoid(c)
    o_ref[...] = jnp.dot(s, w_ref[...], preferred_element_type=F32) + b_ref[...]


def _ada(c_all, w_ada, b_ada):
    rows = c_all.shape[0]
    n = w_ada.shape[1]
    tn = 1024
    return pl.pallas_call(
        _ada_kernel,
        grid=(n // tn,),
        in_specs=[pl.BlockSpec((rows, D_MODEL), lambda j: (0, 0)),
                  pl.BlockSpec((D_MODEL, tn), lambda j: (0, j)),
                  pl.BlockSpec((1, tn), lambda j: (0, j))],
        out_specs=pl.BlockSpec((rows, tn), lambda j: (0, j)),
        out_shape=jax.ShapeDtypeStruct((rows, n), F32),
        compiler_params=_cparams(("arbitrary",)),
        name="ada",
    )(c_all, w_ada, b_ada.reshape(1, n))


def _s5prep_kernel(are_ref, aim_ref, ldt_ref, arx_ref, aix_ref, ldx_ref, bre_ref, bim_ref,
                   lre_ref, lim_ref, bbre_ref, bbim_ref):
    def lam(a_re, a_im, log_dt):
        dt = jnp.exp(log_dt)
        mag = jnp.exp(a_re * dt)
        return mag * jnp.cos(a_im * dt), mag * jnp.sin(a_im * dt)

    l_re, l_im = lam(are_ref[...], aim_ref[...], ldt_ref[...])
    lre_ref[...] = l_re
    lim_ref[...] = l_im
    a_re, a_im = arx_ref[...], aix_ref[...]
    x_re, x_im = lam(a_re, a_im, ldx_ref[...])
    den = a_re * a_re + a_im * a_im
    n_re = x_re - 1.0
    k_re = (n_re * a_re + x_im * a_im) / den
    k_im = (x_im * a_re - n_re * a_im) / den
    b_re, b_im = bre_ref[...], bim_ref[...]
    bbre_ref[...] = k_re * b_re - k_im * b_im
    bbim_ref[...] = k_re * b_im + k_im * b_re


def _s5prep(a_re, a_im, log_dt, b_re, b_im):
    g, p, c = b_re.shape
    ldt = jnp.broadcast_to(log_dt[:, None], (g, p))
    ex = lambda a: jnp.broadcast_to(a[:, :, None], (g, p, c)).reshape(g, p * c)
    small = jax.ShapeDtypeStruct((g, p), F32)
    big = jax.ShapeDtypeStruct((g, p * c), F32)
    l_re, l_im, bb_re, bb_im = pl.pallas_call(
        _s5prep_kernel, out_shape=(small, small, big, big), name="s5prep",
    )(a_re, a_im, ldt, ex(a_re), ex(a_im), ex(ldt), b_re.reshape(g, p * c), b_im.reshape(g, p * c))
    return l_re, l_im, bb_re.reshape(g, p, c), bb_im.reshape(g, p, c)


def _blockdiag(w):
    g, a, b = w.shape
    w = w.reshape(N_GBLK, GROUPS_PER_BLOCK, a, b)
    eye = jnp.eye(GROUPS_PER_BLOCK, dtype=w.dtype)
    return jnp.einsum('xgab,gh->xgahb', w, eye).reshape(N_GBLK, GROUPS_PER_BLOCK * a, GROUPS_PER_BLOCK * b)


def _s5_weights(l_re, l_im, bb_re, bb_im, c_re, c_im, d_skip, w_glu, b_glu):
    wb = jnp.concatenate([_blockdiag(jnp.swapaxes(bb_re, 1, 2)),
                          _blockdiag(jnp.swapaxes(bb_im, 1, 2))], axis=2)
    wc = jnp.concatenate([_blockdiag(jnp.swapaxes(c_re, 1, 2)),
                          _blockdiag(-jnp.swapaxes(c_im, 1, 2))], axis=1)
    wg = jnp.concatenate([_blockdiag(w_glu[:, :, :SSM_GROUP]),
                          _blockdiag(w_glu[:, :, SSM_GROUP:])], axis=2)
    bg = jnp.concatenate([b_glu[:, :SSM_GROUP].reshape(N_GBLK, 1, CH_LANES),
                          b_glu[:, SSM_GROUP:].reshape(N_GBLK, 1, CH_LANES)], axis=2)
    dsk = d_skip.reshape(N_GBLK, 1, CH_LANES)
    lam = jnp.concatenate([l_re.reshape(N_GBLK, 1, ST_LANES), l_im.reshape(N_GBLK, 1, ST_LANES)], axis=2)
    return wb, wc, wg, bg, dsk, lam


def _inproj_kernel(x_ref, sh_ref, sc_ref, w_ref, wu_ref, wf_ref, wft_ref, bf_ref, bfr_ref,
                   q_ref, kf_ref, kb_ref, vf_ref, vb_ref, u_ref, lfc_ref, lfr_ref, h_scr):
    j = pl.program_id(2)

    @pl.when(j == 0)
    def _():
        h = _ln(x_ref[0]) * (1.0 + sc_ref[0]) + sh_ref[0]
        hb = h.astype(BF16)
        h_scr[...] = hb
        fl = _nt_dot(hb, wf_ref[...]) + bf_ref[...]
        lfc_ref[0] = _log_sigmoid(fl)[:, :N_HEADS]
        lfr_ref[0] = _log_sigmoid(_nt_dot(wft_ref[...], hb) + bfr_ref[...])

    def proj(w, group):
        return _nt_dot(h_scr[...], w[group * ATT_WIDTH:(group + 1) * ATT_WIDTH, :])

    @pl.when(j == 0)
    def _():
        q_ref[0] = (proj(w_ref, 0) * QSCALE).astype(BF16)

    @pl.when(j == 1)
    def _():
        acc = proj(w_ref, 1)
        kf_ref[0] = acc
        kb_ref[0] = acc.astype(BF16)

    @pl.when(j == 2)
    def _():
        acc = proj(w_ref, 2)
        vf_ref[0] = acc
        vb_ref[0] = acc.astype(BF16)

    @pl.when(j == 3)
    def _():
        u_ref[0] = proj(wu_ref, 0)


def _inproj(x, shift, scale, w_qkv, w_u, w_f, w_ft, b_f, tm):
    nb, l, _ = x.shape
    r = shift.shape[1]
    mod_map = (lambda b, i, j: (b, 0, 0)) if r == 1 else (lambda b, i, j: (b, i, 0))
    row_map = lambda b, i, j: (b, i, 0)
    const2 = lambda b, i, j: (0, 0)
    wide = lambda dt: jax.ShapeDtypeStruct((nb, l, ATT_WIDTH), dt)
    return pl.pallas_call(
        _inproj_kernel,
        grid=(nb, l // tm, 4),
        in_specs=[pl.BlockSpec((1, tm, D_MODEL), row_map),
                  pl.BlockSpec((1, r, D_MODEL), mod_map),
                  pl.BlockSpec((1, r, D_MODEL), mod_map),
                  pl.BlockSpec((3 * ATT_WIDTH, D_MODEL), const2, pipeline_mode=pl.Buffered(1)),
                  pl.BlockSpec((SSM_WIDTH, D_MODEL), const2, pipeline_mode=pl.Buffered(1)),
                  pl.BlockSpec((LANE, D_MODEL), const2),
                  pl.BlockSpec((N_HEADS, D_MODEL), const2),
                  pl.BlockSpec((1, LANE), const2),
                  pl.BlockSpec((N_HEADS, 1), const2)],
        out_specs=[pl.BlockSpec((1, tm, ATT_WIDTH), row_map)] * 6
                  + [pl.BlockSpec((1, tm, N_HEADS), row_map),
                     pl.BlockSpec((1, N_HEADS, tm), lambda b, i, j: (b, 0, i))],
        out_shape=(wide(BF16), wide(F32), wide(BF16), wide(F32), wide(BF16), wide(F32),
                   jax.ShapeDtypeStruct((nb, l, N_HEADS), F32),
                   jax.ShapeDtypeStruct((nb, N_HEADS, l), F32)),
        scratch_shapes=[pltpu.VMEM((tm, D_MODEL), BF16)],
        compiler_params=_cparams(("arbitrary", "arbitrary", "arbitrary")),
        name="inproj",
    )(x, shift, scale, w_qkv, w_u, w_f, w_ft, jnp.pad(b_f, (0, LANE - N_HEADS)).reshape(1, LANE),
      b_f.reshape(N_HEADS, 1))


CUM_BLK = 512


def _cumsum_kernel(lfr_ref, fr_ref):
    s_len = lfr_ref.shape[2]
    r = lax.broadcasted_iota(jnp.int32, (CUM_BLK, CUM_BLK), 0)
    c = lax.broadcasted_iota(jnp.int32, (CUM_BLK, CUM_BLK), 1)
    tri_u = (r <= c).astype(F32)
    carry = jnp.zeros((N_HEADS, 1), F32)
    for blk in range(s_len // CUM_BLK):
        cols = slice(blk * CUM_BLK, (blk + 1) * CUM_BLK)
        fr = jnp.dot(lfr_ref[0, :, cols], tri_u, precision=lax.Precision.HIGHEST,
                     preferred_element_type=F32) + carry
        fr_ref[0, :, cols] = fr * LOG2E
        carry = fr[:, CUM_BLK - 1:CUM_BLK]


def _cumsum(lfr):
    nb, _, s_len = lfr.shape
    rspec = pl.BlockSpec((1, N_HEADS, s_len), lambda b: (b, 0, 0))
    return pl.pallas_call(
        _cumsum_kernel, grid=(nb,), in_specs=[rspec], out_specs=rspec,
        out_shape=jax.ShapeDtypeStruct(lfr.shape, F32),
        compiler_params=_cparams(("arbitrary",)), name="cumsum",
    )(lfr)


ATT_BLK = 512


def _attn_kernel(q_ref, k_ref, v_ref, fr_ref, o_ref, s_scr):
    h = pl.program_id(1)
    qi = pl.program_id(2)
    t = ATT_BLK
    q = q_ref[0]

    def scores(j):
        start = pl.multiple_of(j * t, t)
        fk = fr_ref[0, pl.ds(h, 1), pl.ds(start, t)]
        return _nt_dot(q, k_ref[0, pl.ds(start, t), :]) - fk

    def values(j):
        return v_ref[0, pl.ds(pl.multiple_of(j * t, t), t), :]

    def stage(j, carry, slot):
        s_scr[1 - slot] = scores(j + 1)
        return _softmax_step(carry, s_scr[slot], values(j))

    def pair(jj, carry):
        return stage(2 * jj + 1, stage(2 * jj, carry, 0), 1)

    def diagonal(carry, slot):
        row = lax.broadcasted_iota(jnp.int32, (t, t), 0)
        col = lax.broadcasted_iota(jnp.int32, (t, t), 1)
        return _softmax_step(carry, jnp.where(col <= row, s_scr[slot], -jnp.inf), values(qi))

    s_scr[0] = scores(0)
    init = (jnp.full((t, 1), -jnp.inf, F32), jnp.zeros((t, 1), F32), jnp.zeros((t, HEAD_DIM), F32))
    carry = lax.fori_loop(0, qi // 2, pair, init)
    m, l, acc = lax.cond(qi % 2 == 1,
                         lambda c: diagonal(stage(qi - 1, c, 0), 1),
                         lambda c: diagonal(c, 0), carry)
    o_ref[0] = (acc / l).astype(BF16)


def _attn(q, k, v, fr):
    nb, s_len, _ = q.shape
    t = ATT_BLK
    qspec = pl.BlockSpec((1, t, HEAD_DIM), lambda b, h, i: (b, i, h))
    kvspec = pl.BlockSpec((1, s_len, HEAD_DIM), lambda b, h, i: (b, 0, h))
    return pl.pallas_call(
        _attn_kernel,
        grid=(nb, N_HEADS, s_len // t),
        in_specs=[qspec, kvspec, kvspec,
                  pl.BlockSpec((1, N_HEADS, s_len), lambda b, h, i: (b, 0, 0))],
        out_specs=qspec,
        out_shape=jax.ShapeDtypeStruct((nb, s_len, ATT_WIDTH), BF16),
        scratch_shapes=[pltpu.VMEM((2, t, t), F32)],
        compiler_params=_cparams(("arbitrary", "arbitrary", "arbitrary")),
        name="attn",
    )(q, k, v, fr)


PAGE_LANES = PAGE * N_HEADS


def _decay_kernel(lf_ref, wt_ref):
    x = lf_ref[...]
    n = x.shape[1]
    lane = lax.broadcasted_iota(jnp.int32, x.shape, 1)
    suffix, total = x, x
    k = N_HEADS
    while k < n:
        suffix = suffix + jnp.where(lane < n - k, pltpu.roll(suffix, n - k, axis=1), 0.0)
        total = total + pltpu.roll(total, k, axis=1)
        k *= 2
    wt_ref[:, 0, :] = suffix - x
    wt_ref[:, 1, :] = total


def _decay(lf_flat):
    n_pool, n = lf_flat.shape
    rb = 256
    return pl.pallas_call(
        _decay_kernel,
        grid=(n_pool // rb,),
        in_specs=[pl.BlockSpec((rb, n), lambda i: (i, 0))],
        out_specs=pl.BlockSpec((rb, 2, n), lambda i: (i, 0, 0)),
        out_shape=jax.ShapeDtypeStruct((n_pool, 2, n), F32),
        compiler_params=_cparams(("arbitrary",)),
        name="decay",
    )(lf_flat)


def _attn_s_init(q, kn_ref, vn_ref, lfn_ref, bias_scr, m_scr, l_scr, acc_scr, carry_scr):
    rows = q.shape[0]
    f = jnp.broadcast_to(lfn_ref[0], (N_HEADS, LANE))
    lane8 = lax.broadcasted_iota(jnp.int32, f.shape, 1)
    k = N_HEADS
    while k < rows:
        f = f + jnp.where(lane8 >= k, pltpu.roll(f, k, axis=1), 0.0)
        k *= 2
    f_row = f[0:1, :] * LOG2E
    r = lax.broadcasted_iota(jnp.int32, (rows, LANE), 0)
    c = lax.broadcasted_iota(jnp.int32, (rows, LANE), 1)
    rr = lax.broadcasted_iota(jnp.int32, (rows, PAGE_LANES), 0)
    cc = lax.broadcasted_iota(jnp.int32, (rows, PAGE_LANES), 1)
    same_head = jnp.bitwise_and(rr, N_HEADS - 1) == jnp.bitwise_and(cc, N_HEADS - 1)
    bias_scr[...] = jnp.where(same_head, 0.0, -jnp.inf)
    pad = jnp.zeros((LANE - rows, HEAD_DIM), BF16)
    kn = jnp.concatenate([kn_ref[0].astype(BF16), pad], axis=0)
    vn = jnp.concatenate([vn_ref[0].astype(BF16), pad], axis=0)
    tok_r = jnp.right_shift(r, 3)
    tok_c = jnp.right_shift(c, 3)
    head_ok = jnp.bitwise_and(r, N_HEADS - 1) == jnp.bitwise_and(c, N_HEADS - 1)
    s = _nt_dot(q, kn) - f_row
    s = jnp.where(head_ok, jnp.where(tok_c <= tok_r, s, -jnp.inf), -jnp.inf)
    init = (jnp.full((rows, 1), -jnp.inf, F32), jnp.zeros((rows, 1), F32), jnp.zeros((rows, HEAD_DIM), F32))
    m_scr[...], l_scr[...], acc_scr[...] = _softmax_step(init, s, vn)
    carry_scr[...] = jnp.zeros_like(carry_scr)


def _attn_s_scores(q, kp_refs, wt_refs, bias_scr, carry_scr):
    carry = carry_scr[...]
    bias = bias_scr[...]
    scores = []
    for kp_ref, wt_ref in zip(kp_refs, wt_refs):
        wt = wt_ref[0]
        dec = (wt[0:1, :] + carry) * LOG2E
        carry = carry + wt[1:2, :]
        k = kp_ref[...].reshape(PAGE_LANES, HEAD_DIM).astype(BF16)
        scores.append(_nt_dot(q, k) + bias + dec)
    carry_scr[...] = carry
    return jnp.concatenate(scores, axis=1)


def _attn_s_values(vp_refs):
    return jnp.concatenate([vp_ref[...].reshape(PAGE_LANES, HEAD_DIM).astype(BF16) for vp_ref in vp_refs], axis=0)


N_SEG = 8
S5_TB = 64
S5_PITCH = 520


def _cmul(ar, ai, br, bi):
    return ar * br - ai * bi, ar * bi + ai * br


def _s5_tail(h, u, wc, dsk, wg, bg):
    y = _dot_row_halves(h.astype(BF16), wc) + dsk * u
    z = jax.nn.gelu(y)
    zz = _dot_row_halves(z.astype(BF16), wg) + bg
    return zz[:, :CH_LANES] * jax.nn.sigmoid(zz[:, CH_LANES:])


def _s5_kernel(u_ref, wb_ref, wc_ref, wg_ref, bg_ref, dsk_ref, lam_ref, y_ref, hfin_ref, ubuf, ybuf, bu_scr):
    s_len = u_ref.shape[1]
    seg = s_len // N_SEG
    n_tb = seg // S5_TB
    assert seg <= S5_PITCH and seg % S5_TB == 0
    for i in range(N_SEG):
        ubuf[i * S5_PITCH:i * S5_PITCH + seg, :] = u_ref[0, i * seg:(i + 1) * seg, :]
    wb = wb_ref[0]
    lam = lam_ref[0]
    l_re = jnp.broadcast_to(lam[:, :ST_LANES], (N_SEG, ST_LANES))
    l_im = jnp.broadcast_to(lam[:, ST_LANES:], (N_SEG, ST_LANES))

    def load_u(tb):
        t0 = tb * S5_TB
        return jnp.concatenate([ubuf[pl.ds(t0 + j, N_SEG, stride=S5_PITCH), :] for j in range(S5_TB)], axis=0)

    def project(tb, slot):
        u = load_u(jnp.minimum(tb, n_tb - 1))
        bu_scr[slot] = jnp.dot(u.astype(BF16), wb, preferred_element_type=F32)

    def scan_block(slot, h_re, h_im, keep):
        hs = []
        for j in range(S5_TB):
            n_re, n_im = _cmul(l_re, l_im, h_re, h_im)
            h_re = n_re + bu_scr[slot, j * N_SEG:(j + 1) * N_SEG, :ST_LANES]
            h_im = n_im + bu_scr[slot, j * N_SEG:(j + 1) * N_SEG, ST_LANES:]
            if keep:
                hs.append(jnp.concatenate([h_re, h_im], axis=1))
        return h_re, h_im, hs

    def two_blocks(stage):
        assert n_tb % 2 == 0
        return lambda jj, carry: stage(2 * jj + 1, 1, stage(2 * jj, 0, carry))

    def stage1(tb, slot, carry):
        project(tb + 1, 1 - slot)
        h_re, h_im, _ = scan_block(slot, *carry, keep=False)
        return h_re, h_im

    zero = jnp.zeros((N_SEG, ST_LANES), F32)
    project(0, 0)
    f_re, f_im = lax.fori_loop(0, n_tb // 2, two_blocks(stage1), (zero, zero))
    p_re, p_im = lam[:, :ST_LANES], lam[:, ST_LANES:]
    n_sq = seg.bit_length() - 1
    assert 1 << n_sq == seg
    for _ in range(n_sq):
        p_re, p_im = _cmul(p_re, p_im, p_re, p_im)
    c_re, c_im = [jnp.zeros((1, ST_LANES), F32)], [jnp.zeros((1, ST_LANES), F32)]
    for i in range(1, N_SEG):
        d_re, d_im = _cmul(p_re, p_im, c_re[-1], c_im[-1])
        c_re.append(f_re[i - 1:i, :] + d_re)
        c_im.append(f_im[i - 1:i, :] + d_im)
    wc, wg, bg, dsk = wc_ref[0], wg_ref[0], bg_ref[0], dsk_ref[0]

    def stage2(tb, slot, carry):
        project(tb + 1, 1 - slot)
        h_re, h_im, hs = scan_block(slot, *carry, keep=True)
        y = _s5_tail(jnp.concatenate(hs, axis=0), load_u(tb), wc, dsk, wg, bg)
        t0 = tb * S5_TB
        for j in range(S5_TB):
            ybuf[pl.ds(t0 + j, N_SEG, stride=S5_PITCH), :] = y[j * N_SEG:(j + 1) * N_SEG, :]
        return h_re, h_im

    project(0, 0)
    h_re, h_im = lax.fori_loop(0, n_tb // 2, two_blocks(stage2),
                               (jnp.concatenate(c_re, axis=0), jnp.concatenate(c_im, axis=0)))
    hfin_ref[0, 0] = jnp.concatenate([h_re[N_SEG - 1:, :], h_im[N_SEG - 1:, :]], axis=1)
    for i in range(N_SEG):
        y_ref[0, i * seg:(i + 1) * seg, :] = ybuf[i * S5_PITCH:i * S5_PITCH + seg, :].astype(BF16)


def _s5(u, wb, wc, wg, bg, dsk, lam):
    nb, s_len, _ = u.shape
    blk = lambda shape: pl.BlockSpec((1,) + shape, lambda b, g: (g, 0, 0))
    y, hfin = pl.pallas_call(
        _s5_kernel,
        grid=(nb, N_GBLK),
        in_specs=[pl.BlockSpec((1, s_len, CH_LANES), lambda b, g: (b, 0, g)),
                  blk((CH_LANES, 2 * ST_LANES)), blk((2 * ST_LANES, CH_LANES)),
                  blk((CH_LANES, 2 * CH_LANES)), blk((1, 2 * CH_LANES)), blk((1, CH_LANES)),
                  blk((1, 2 * ST_LANES))],
        out_specs=[pl.BlockSpec((1, s_len, CH_LANES), lambda b, g: (b, 0, g)),
                   pl.BlockSpec((1, 1, 1, 2 * ST_LANES), lambda b, g: (b, g, 0, 0))],
        out_shape=(jax.ShapeDtypeStruct((nb, s_len, SSM_WIDTH), BF16),
                   jax.ShapeDtypeStruct((nb, N_GBLK, 1, 2 * ST_LANES), F32)),
        scratch_shapes=[pltpu.VMEM((N_SEG * S5_PITCH, CH_LANES), F32),
                        pltpu.VMEM((N_SEG * S5_PITCH, CH_LANES), F32),
                        pltpu.VMEM((2, S5_TB * N_SEG, 2 * ST_LANES), F32)],
        compiler_params=_cparams(("arbitrary", "arbitrary")),
        name="s5",
    )(u, wb.astype(BF16), wc.astype(BF16), wg.astype(BF16), bg, dsk, lam)
    hfin = hfin.reshape(nb, N_GBLK, 2, GROUPS_PER_BLOCK, SSM_STATE)
    return y, hfin[:, :, 0].reshape(nb, N_GROUPS, SSM_STATE), hfin[:, :, 1].reshape(nb, N_GROUPS, SSM_STATE)


def _s5s_kernel(u_ref, h0_ref, wb_ref, wc_ref, wg_ref, bg_ref, dsk_ref, lam_ref, y_ref, hfin_ref, *, nb):
    hi = lax.Precision.HIGHEST
    u = u_ref[...]
    n_t = u.shape[0] // nb
    bu = jnp.dot(u, wb_ref[0], precision=hi, preferred_element_type=F32)
    lam = lam_ref[0]
    l_re = jnp.broadcast_to(lam[:, :ST_LANES], (nb, ST_LANES))
    l_im = jnp.broadcast_to(lam[:, ST_LANES:], (nb, ST_LANES))
    h0 = h0_ref[0]
    h_re, h_im = h0[:, :ST_LANES], h0[:, ST_LANES:]
    hs = []
    for t in range(n_t):
        n_re, n_im = _cmul(l_re, l_im, h_re, h_im)
        h_re = n_re + bu[t * nb:(t + 1) * nb, :ST_LANES]
        h_im = n_im + bu[t * nb:(t + 1) * nb, ST_LANES:]
        hs.append(jnp.concatenate([h_re, h_im], axis=1))
    hfin_ref[0] = hs[-1]
    h = jnp.concatenate(hs, axis=0)
    y = jnp.dot(h, wc_ref[0], precision=hi, preferred_element_type=F32) + dsk_ref[0] * u
    z = jax.nn.gelu(y)
    zz = jnp.dot(z, wg_ref[0], precision=hi, preferred_element_type=F32) + bg_ref[0]
    y_ref[...] = (zz[:, :CH_LANES] * jax.nn.sigmoid(zz[:, CH_LANES:])).astype(BF16)


def _s5s(u_tm, h0, wb, wc, wg, bg, dsk, lam, nb):
    rows = u_tm.shape[0]
    blk = lambda shape: pl.BlockSpec((1,) + shape, lambda g: (g, 0, 0))
    return pl.pallas_call(
        functools.partial(_s5s_kernel, nb=nb),
        grid=(N_GBLK,),
        in_specs=[pl.BlockSpec((rows, CH_LANES), lambda g: (0, g)),
                  blk((nb, 2 * ST_LANES)),
                  blk((CH_LANES, 2 * ST_LANES)), blk((2 * ST_LANES, CH_LANES)),
                  blk((CH_LANES, 2 * CH_LANES)), blk((1, 2 * CH_LANES)), blk((1, CH_LANES)),
                  blk((1, 2 * ST_LANES))],
        out_specs=[pl.BlockSpec((rows, CH_LANES), lambda g: (0, g)), blk((nb, 2 * ST_LANES))],
        out_shape=(jax.ShapeDtypeStruct((rows, SSM_WIDTH), BF16),
                   jax.ShapeDtypeStruct((N_GBLK, nb, 2 * ST_LANES), F32)),
        compiler_params=_cparams(("arbitrary",)),
        name="s5s",
    )(u_tm, h0, wb, wc, wg, bg, dsk, lam)


def _outproj_kernel(att_ref, ssm_ref, x_ref, g1_ref, sc2_ref, sh2_ref, lng_ref, lnb_ref, wo_ref,
                    x1_ref, h2_ref):
    tm = x_ref.shape[1]
    n_sub = 2 if tm >= 512 else 1
    sub = tm // n_sub

    def mod_rows(ref, rows):
        return ref[0] if ref.shape[1] == 1 else ref[0, rows, :]

    for c in range(n_sub):
        rows = slice(c * sub, (c + 1) * sub)
        mix = (jnp.dot(att_ref[0, rows, :], wo_ref[:ATT_WIDTH, :], preferred_element_type=F32)
               + jnp.dot(ssm_ref[0, rows, :], wo_ref[ATT_WIDTH:, :], preferred_element_type=F32))
        x1 = (_ln(ALPHA * x_ref[0, rows, :] + (1.0 + mod_rows(g1_ref, rows)) * mix) * lng_ref[...]
              + lnb_ref[...])
        x1_ref[0, rows, :] = x1
        h2_ref[0, rows, :] = (_ln(x1) * (1.0 + mod_rows(sc2_ref, rows)) + mod_rows(sh2_ref, rows)).astype(BF16)


def _outproj(att, ssm, x, gate1, scale2, shift2, ln_g, ln_b, w_o, tm):
    nb, l, _ = x.shape
    r = gate1.shape[1]
    mod_map = (lambda b, i: (b, 0, 0)) if r == 1 else (lambda b, i: (b, i, 0))
    row_map = lambda b, i: (b, i, 0)
    const2 = lambda b, i: (0, 0)
    half = pl.BlockSpec((1, tm, ATT_WIDTH), row_map)
    full = pl.BlockSpec((1, tm, D_MODEL), row_map)
    mod = pl.BlockSpec((1, r, D_MODEL), mod_map)
    vec = pl.BlockSpec((1, D_MODEL), const2)
    return pl.pallas_call(
        _outproj_kernel,
        grid=(nb, l // tm),
        in_specs=[half, half, full, mod, mod, mod, vec, vec,
                  pl.BlockSpec((D_MODEL, D_MODEL), const2)],
        out_specs=[full, full],
        out_shape=(jax.ShapeDtypeStruct((nb, l, D_MODEL), F32),
                   jax.ShapeDtypeStruct((nb, l, D_MODEL), BF16)),
        compiler_params=_cparams(("arbitrary", "arbitrary")),
        name="outproj",
    )(att, ssm, x, gate1, scale2, shift2, ln_g.reshape(1, D_MODEL), ln_b.reshape(1, D_MODEL), w_o)


FFN_TF = 1024
FUSED_TF = 512


def _ffn_zero(f, acc_scr):
    @pl.when(f == 0)
    def _():
        acc_scr[...] = jnp.zeros_like(acc_scr)


def _ffn_accumulate(h2_ref, wu_ref, wd_ref, acc_scr):
    up = jnp.dot(h2_ref[0], wu_ref[...], preferred_element_type=F32)
    act = jnp.square(jnp.maximum(up, 0.0)).astype(BF16)
    acc_scr[...] += jnp.dot(act, wd_ref[...], preferred_element_type=F32)


def _ffn_finish(f, x1_ref, g2_ref, lng_ref, lnb_ref, y_ref, acc_scr):
    @pl.when(f == pl.num_programs(2) - 1)
    def _():
        y_ref[0] = _ln(ALPHA * x1_ref[0] + (1.0 + g2_ref[0]) * acc_scr[...]) * lng_ref[...] + lnb_ref[...]


def _ffn_kernel(h2_ref, x1_ref, g2_ref, lng_ref, lnb_ref, wu_ref, wd_ref, y_ref, acc_scr):
    f = pl.program_id(2)
    _ffn_zero(f, acc_scr)
    _ffn_accumulate(h2_ref, wu_ref, wd_ref, acc_scr)
    _ffn_finish(f, x1_ref, g2_ref, lng_ref, lnb_ref, y_ref, acc_scr)


def _ffn_specs(gate2, tm, tf):
    r = gate2.shape[1]
    wrap = lambda fn: (lambda b, i, f, *_: fn(b, i, f))
    row_map = wrap(lambda b, i, f: (b, i, 0))
    full = pl.BlockSpec((1, tm, D_MODEL), row_map)
    vec = pl.BlockSpec((1, D_MODEL), wrap(lambda b, i, f: (0, 0)))
    mod = pl.BlockSpec((1, r, D_MODEL), wrap((lambda b, i, f: (b, 0, 0)) if r == 1 else (lambda b, i, f: (b, i, 0))))
    in_specs = [full, full, mod, vec, vec,
                pl.BlockSpec((D_MODEL, tf), wrap(lambda b, i, f: (0, f))),
                pl.BlockSpec((tf, D_MODEL), wrap(lambda b, i, f: (f, 0)))]
    return in_specs, full


def _ffn(h2, x1, gate2, ln_g, ln_b, w_up, w_down, tm):
    nb, l, _ = x1.shape
    in_specs, out_spec = _ffn_specs(gate2, tm, FFN_TF)
    return pl.pallas_call(
        _ffn_kernel,
        grid=(nb, l // tm, D_FF // FFN_TF),
        in_specs=in_specs,
        out_specs=out_spec,
        out_shape=jax.ShapeDtypeStruct((nb, l, D_MODEL), F32),
        scratch_shapes=[pltpu.VMEM((tm, D_MODEL), F32)],
        compiler_params=_cparams(("arbitrary", "arbitrary", "arbitrary")),
        name="ffn",
    )(h2, x1, gate2, ln_g.reshape(1, D_MODEL), ln_b.reshape(1, D_MODEL), w_up, w_down)


def _ffn_attn_kernel(pt_ref, h2_ref, x1_ref, g2_ref, lng_ref, lnb_ref, wu_ref, wd_ref,
                     q_ref, kn_ref, vn_ref, lfn_ref, *refs, n_pg, steps_per_seq):
    kp_refs, vp_refs, wt_refs = refs[:n_pg], refs[n_pg:2 * n_pg], refs[2 * n_pg:3 * n_pg]
    y_ref, o_ref, acc_scr, bias_scr, m_scr, l_scr, a_scr, carry_scr = refs[3 * n_pg:]
    f = pl.program_id(2)
    t = (pl.program_id(0) * pl.num_programs(1) + pl.program_id(1)) * pl.num_programs(2) + f
    si = lax.rem(t, steps_per_seq)
    q = q_ref[0]

    @pl.when(si == 0)
    def _():
        _attn_s_init(q, kn_ref, vn_ref, lfn_ref, bias_scr, m_scr, l_scr, a_scr, carry_scr)

    _ffn_zero(f, acc_scr)
    s = _attn_s_scores(q, kp_refs, wt_refs, bias_scr, carry_scr)
    up = jnp.dot(h2_ref[0], wu_ref[...], preferred_element_type=F32)
    m_new, alpha, p, l_new = _softmax_weights(m_scr[...], l_scr[...], s)
    act = jnp.square(jnp.maximum(up, 0.0)).astype(BF16)
    acc_scr[...] += jnp.dot(act, wd_ref[...], preferred_element_type=F32)
    a_new = alpha * a_scr[...] + _dot_depth_halves(p, _attn_s_values(vp_refs))
    m_scr[...], l_scr[...], a_scr[...] = m_new, l_new, a_new

    @pl.when(si == steps_per_seq - 1)
    def _():
        o_ref[0] = a_new / l_new

    _ffn_finish(f, x1_ref, g2_ref, lng_ref, lnb_ref, y_ref, acc_scr)


def _ffn_attn(h2, x1, gate2, ln_g, ln_b, w_up, w_down, tm,
              page_table, q, k_new, v_new, lf_new, cache_k, cache_v, decay_wt):
    nb, l, _ = x1.shape
    nbs, rows, _ = q.shape
    n_pages = page_table.shape[1]
    n_i, n_f = l // tm, D_FF // FUSED_TF
    n_steps = nb * n_i * n_f
    n_pg = nbs * n_pages // n_steps
    steps_per_seq = n_pages // n_pg
    assert n_pg * n_steps == nbs * n_pages and steps_per_seq * n_pg == n_pages

    def step(b, i, f):
        t = (b * n_i + i) * n_f + f
        return lax.div(t, steps_per_seq), lax.rem(t, steps_per_seq)

    seq = lambda b, i, f, pt: (step(b, i, f)[0], 0, 0)

    def page_spec(block, n_lead, pg):
        def idx(b, i, f, pt):
            sq, si = step(b, i, f)
            return (0,) * n_lead + (pt[sq, n_pages - 1 - (si * n_pg + pg)],) + (0,) * (len(block) - n_lead - 1)
        return pl.BlockSpec(block, idx)

    ffn_specs, y_spec = _ffn_specs(gate2, tm, FUSED_TF)
    kv_block = (None, None, PAGE, N_HEADS, HEAD_DIM)
    tok_spec = pl.BlockSpec((1, rows, HEAD_DIM), seq)
    grid_spec = pltpu.PrefetchScalarGridSpec(
        num_scalar_prefetch=1,
        grid=(nb, n_i, n_f),
        in_specs=ffn_specs + [tok_spec, tok_spec, tok_spec, pl.BlockSpec((1, 1, LANE), seq)]
                 + [page_spec(kv_block, 1, pg) for pg in range(n_pg)]
                 + [page_spec(kv_block, 1, pg) for pg in range(n_pg)]
                 + [page_spec((1, 2, PAGE_LANES), 0, pg) for pg in range(n_pg)],
        out_specs=[y_spec, tok_spec],
        scratch_shapes=[pltpu.VMEM((tm, D_MODEL), F32),
                        pltpu.VMEM((rows, PAGE_LANES), F32),
                        pltpu.VMEM((rows, 1), F32),
                        pltpu.VMEM((rows, 1), F32),
                        pltpu.VMEM((rows, HEAD_DIM), F32),
                        pltpu.VMEM((1, PAGE_LANES), F32)])
    return pl.pallas_call(
        functools.partial(_ffn_attn_kernel, n_pg=n_pg, steps_per_seq=steps_per_seq),
        grid_spec=grid_spec,
        out_shape=(jax.ShapeDtypeStruct((nb, l, D_MODEL), F32),
                   jax.ShapeDtypeStruct((nbs, rows, HEAD_DIM), F32)),
        compiler_params=_cparams(("arbitrary", "arbitrary", "arbitrary")),
        name="ffn_attn",
    )(page_table, h2, x1, gate2, ln_g.reshape(1, D_MODEL), ln_b.reshape(1, D_MODEL), w_up, w_down,
      q, k_new, v_new, lf_new, *([cache_k] * n_pg), *([cache_v] * n_pg), *([decay_wt] * n_pg))


def kernel(x_prompt, x_sample, c_prompt, c_sample, cache_k, cache_v, cache_logf, state_ssm_re,
           state_ssm_im, page_table, w_ada, b_ada, w_in, b_f, w_o, a_re, a_im, log_dt, b_re, b_im,
           c_re, c_im, d_skip, w_glu, b_glu, ln1_g, ln1_b, w_up, w_down, ln2_g, ln2_b):
    assert w_ada.shape[0] == DEPTH == 1
    nbp, s_len, _ = x_prompt.shape
    nbs, n_q, _ = x_sample.shape
    n_seq = nbp + nbs

    c_all = jnp.concatenate([c_prompt, c_sample, jnp.zeros((16 - n_seq, D_MODEL), F32)], axis=0)
    mod = _ada(c_all, w_ada[0], b_ada[0]).reshape(16, N_MOD, D_MODEL)
    mod_p = [mod[:nbp, i][:, None, :] for i in range(N_MOD)]
    mod_s = [jnp.repeat(mod[nbp:n_seq, i], n_q, axis=0)[None] for i in range(N_MOD)]

    a = ATT_WIDTH
    w_in_t = w_in[0].T
    w_qkv = w_in_t[:3 * a].astype(BF16)
    w_u = w_in_t[3 * a + N_HEADS:].astype(BF16)
    w_ft = w_in_t[3 * a:3 * a + N_HEADS].astype(BF16)
    w_f = jnp.pad(w_ft, ((0, LANE - N_HEADS), (0, 0)))
    w_o_b = w_o[0].astype(BF16)
    w_up_b = w_up[0].astype(BF16)
    w_down_b = w_down[0].astype(BF16)

    l_re, l_im, bb_re, bb_im = _s5prep(a_re[0], a_im[0], log_dt[0], b_re[0], b_im[0])
    wb, wc, wg, bg, dsk, lam = _s5_weights(l_re, l_im, bb_re, bb_im, c_re[0], c_im[0], d_skip[0],
                                           w_glu[0], b_glu[0])

    rows = nbs * n_q
    xs = x_sample.reshape(1, rows, D_MODEL)
    qs, ks, _, vs, _, us, lfcs, _ = _inproj(xs, mod_s[0], mod_s[1], w_qkv, w_u, w_f, w_ft, b_f[0], tm=rows)
    n_pool = cache_k.shape[1]
    decay_wt = _decay(cache_logf[0].reshape(n_pool, PAGE_LANES))
    per_head = lambda t: t.reshape(nbs, n_q * N_HEADS, HEAD_DIM)
    lf_new = jnp.pad(lfcs.reshape(nbs, 1, n_q * N_HEADS), ((0, 0), (0, 0), (0, LANE - n_q * N_HEADS)))

    q, k, k_b, v, v_b, u, lfc, lfr = _inproj(x_prompt, mod_p[0], mod_p[1], w_qkv, w_u, w_f, w_ft, b_f[0], tm=512)
    att = _attn(q, k_b, v_b, _cumsum(lfr))
    ssm, hp_re, hp_im = _s5(u, wb, wc, wg, bg, dsk, lam)
    x1, h2 = _outproj(att, ssm, x_prompt, mod_p[2], mod_p[4], mod_p[3], ln1_g[0], ln1_b[0], w_o_b, tm=512)
    y_p, att_s = _ffn_attn(h2, x1, mod_p[5], ln2_g[0], ln2_b[0], w_up_b, w_down_b, 512,
                           page_table, per_head(qs), per_head(ks), per_head(vs), lf_new, cache_k, cache_v,
                           decay_wt)

    u_tm = us.reshape(nbs, n_q, SSM_WIDTH).transpose(1, 0, 2).reshape(rows, SSM_WIDTH)
    h0 = jnp.concatenate([state_ssm_re[0].reshape(nbs, N_GBLK, ST_LANES),
                          state_ssm_im[0].reshape(nbs, N_GBLK, ST_LANES)], axis=2).transpose(1, 0, 2)
    ssm_tm, hs_fin = _s5s(u_tm, h0, wb, wc, wg, bg, dsk, lam, nbs)
    ssm_s = ssm_tm.reshape(n_q, nbs, SSM_WIDTH).transpose(1, 0, 2).reshape(1, rows, SSM_WIDTH)
    hs_fin = hs_fin.transpose(1, 0, 2).reshape(nbs, N_GBLK, 2, GROUPS_PER_BLOCK, SSM_STATE)
    hs_re = hs_fin[:, :, 0].reshape(nbs, N_GROUPS, SSM_STATE)
    hs_im = hs_fin[:, :, 1].reshape(nbs, N_GROUPS, SSM_STATE)
    x1s, h2s = _outproj(att_s.reshape(1, rows, a).astype(BF16), ssm_s, xs, mod_s[2], mod_s[4], mod_s[3],
                        ln1_g[0], ln1_b[0], w_o_b, tm=rows)
    y_s = _ffn(h2s, x1s, mod_s[5], ln2_g[0], ln2_b[0], w_up_b, w_down_b, tm=rows)

    hd = (N_HEADS, HEAD_DIM)
    return (y_p, y_s.reshape(nbs, n_q, D_MODEL),
            k.reshape(1, nbp, s_len, *hd), v.reshape(1, nbp, s_len, *hd), lfc[None],
            hp_re[None], hp_im[None],
            ks.reshape(1, nbs, n_q, *hd), vs.reshape(1, nbs, n_q, *hd), lfcs.reshape(1, nbs, n_q, N_HEADS),
            hs_re[None], hs_im[None])
```

```python
import functools
import math

import jax
import jax.numpy as jnp
from jax import lax
from jax.experimental import pallas as pl
from jax.experimental.pallas import tpu as pltpu

F32 = jnp.float32
BF16 = jnp.bfloat16

LANE = 128
D_MODEL = 2048
ATT_WIDTH = 1024
SSM_WIDTH = 1024
HEAD_DIM = 128
N_HEADS = 8
SSM_GROUP = 16
N_GROUPS = 64
SSM_STATE = 64
D_FF = 8192
N_MOD = 6
PAGE = 128
DEPTH = 1
ALPHA = (2 * DEPTH) ** 0.25
LN_EPS = 1e-5
LOG2E = math.log2(math.e)
QSCALE = HEAD_DIM ** -0.5 * LOG2E

GROUPS_PER_BLOCK = 8
N_GBLK = N_GROUPS // GROUPS_PER_BLOCK
ST_LANES = GROUPS_PER_BLOCK * SSM_STATE
CH_LANES = GROUPS_PER_BLOCK * SSM_GROUP

VMEM_LIMIT = 56 * 1024 * 1024


def _cparams(sem):
    return pltpu.CompilerParams(dimension_semantics=sem, vmem_limit_bytes=VMEM_LIMIT)


def _ln(x):
    mu = jnp.mean(x, axis=-1, keepdims=True)
    xc = x - mu
    var = jnp.mean(xc * xc, axis=-1, keepdims=True)
    return xc * lax.rsqrt(var + LN_EPS)


def _log_sigmoid(x):
    return jnp.minimum(x, 0.0) - jnp.log1p(jnp.exp(-jnp.abs(x)))


def _nt_dot(a, b):
    return lax.dot_general(a, b, (((1,), (1,)), ((), ())), preferred_element_type=F32)


def _softmax_weights(m, l, s):
    m_new = jnp.maximum(m, jnp.max(s, axis=1, keepdims=True))
    alpha = jnp.exp2(m - m_new)
    p = jnp.exp2(s - m_new)
    return m_new, alpha, p.astype(BF16), alpha * l + jnp.sum(p, axis=1, keepdims=True)


def _dot_row_halves(a, b):
    half = a.shape[0] // 2
    return jnp.concatenate([jnp.dot(a[:half], b, preferred_element_type=F32),
                            jnp.dot(a[half:], b, preferred_element_type=F32)], axis=0)


def _dot_depth_halves(a, b):
    half = a.shape[1] // 2
    return (jnp.dot(a[:, :half], b[:half], preferred_element_type=F32)
            + jnp.dot(a[:, half:], b[half:], preferred_element_type=F32))


def _softmax_step(carry, s, v):
    m, l, acc = carry
    m_new, alpha, p, l_new = _softmax_weights(m, l, s)
    return m_new, l_new, alpha * acc + jnp.dot(p, v, preferred_element_type=F32)


def _ada_kernel(c_ref, w_ref, b_ref, o_ref):
    c = c_ref[...]
    s = c * jax.nn.sigmoid(c)
    o_ref[...] = jnp.dot(s, w_ref[...], preferred_element_type=F32) + b_ref[...]


def _ada(c_all, w_ada, b_ada):
    rows = c_all.shape[0]
    n = w_ada.shape[1]
    tn = 1024
    return pl.pallas_call(
        _ada_kernel,
        grid=(n // tn,),
        in_specs=[pl.BlockSpec((rows, D_MODEL), lambda j: (0, 0)),
                  pl.BlockSpec((D_MODEL, tn), lambda j: (0, j)),
                  pl.BlockSpec((1, tn), lambda j: (0, j))],
        out_specs=pl.BlockSpec((rows, tn), lambda j: (0, j)),
        out_shape=jax.ShapeDtypeStruct((rows, n), F32),
        compiler_params=_cparams(("arbitrary",)),
        name="ada",
    )(c_all, w_ada, b_ada.reshape(1, n))


def _s5prep_kernel(are_ref, aim_ref, ldt_ref, arx_ref, aix_ref, ldx_ref, bre_ref, bim_ref,
                   lre_ref, lim_ref, bbre_ref, bbim_ref):
    def lam(a_re, a_im, log_dt):
        dt = jnp.exp(log_dt)
        mag = jnp.exp(a_re * dt)
        return mag * jnp.cos(a_im * dt), mag * jnp.sin(a_im * dt)

    l_re, l_im = lam(are_ref[...], aim_ref[...], ldt_ref[...])
    lre_ref[...] = l_re
    lim_ref[...] = l_im
    a_re, a_im = arx_ref[...], aix_ref[...]
    x_re, x_im = lam(a_re, a_im, ldx_ref[...])
    den = a_re * a_re + a_im * a_im
    n_re = x_re - 1.0
    k_re = (n_re * a_re + x_im * a_im) / den
    k_im = (x_im * a_re - n_re * a_im) / den
    b_re, b_im = bre_ref[...], bim_ref[...]
    bbre_ref[...] = k_re * b_re - k_im * b_im
    bbim_ref[...] = k_re * b_im + k_im * b_re


def _s5prep(a_re, a_im, log_dt, b_re, b_im):
    g, p, c = b_re.shape
    ldt = jnp.broadcast_to(log_dt[:, None], (g, p))
    ex = lambda a: jnp.broadcast_to(a[:, :, None], (g, p, c)).reshape(g, p * c)
    small = jax.ShapeDtypeStruct((g, p), F32)
    big = jax.ShapeDtypeStruct((g, p * c), F32)
    l_re, l_im, bb_re, bb_im = pl.pallas_call(
        _s5prep_kernel, out_shape=(small, small, big, big), name="s5prep",
    )(a_re, a_im, ldt, ex(a_re), ex(a_im), ex(ldt), b_re.reshape(g, p * c), b_im.reshape(g, p * c))
    return l_re, l_im, bb_re.reshape(g, p, c), bb_im.reshape(g, p, c)


def _blockdiag(w):
    g, a, b = w.shape
    w = w.reshape(N_GBLK, GROUPS_PER_BLOCK, a, b)
    eye = jnp.eye(GROUPS_PER_BLOCK, dtype=w.dtype)
    return jnp.einsum('xgab,gh->xgahb', w, eye).reshape(N_GBLK, GROUPS_PER_BLOCK * a, GROUPS_PER_BLOCK * b)


def _s5_weights(l_re, l_im, bb_re, bb_im, c_re, c_im, d_skip, w_glu, b_glu):
    wb = jnp.concatenate([_blockdiag(jnp.swapaxes(bb_re, 1, 2)),
                          _blockdiag(jnp.swapaxes(bb_im, 1, 2))], axis=2)
    wc = jnp.concatenate([_blockdiag(jnp.swapaxes(c_re, 1, 2)),
                          _blockdiag(-jnp.swapaxes(c_im, 1, 2))], axis=1)
    wg = jnp.concatenate([_blockdiag(w_glu[:, :, :SSM_GROUP]),
                          _blockdiag(w_glu[:, :, SSM_GROUP:])], axis=2)
    bg = jnp.concatenate([b_glu[:, :SSM_GROUP].reshape(N_GBLK, 1, CH_LANES),
                          b_glu[:, SSM_GROUP:].reshape(N_GBLK, 1, CH_LANES)], axis=2)
    dsk = d_skip.reshape(N_GBLK, 1, CH_LANES)
    lam = jnp.concatenate([l_re.reshape(N_GBLK, 1, ST_LANES), l_im.reshape(N_GBLK, 1, ST_LANES)], axis=2)
    return wb, wc, wg, bg, dsk, lam


def _inproj_kernel(x_ref, sh_ref, sc_ref, w_ref, wu_ref, wf_ref, wft_ref, bf_ref, bfr_ref,
                   q_ref, kf_ref, kb_ref, vf_ref, vb_ref, u_ref, lfc_ref, lfr_ref, h_scr):
    j = pl.program_id(2)

    @pl.when(j == 0)
    def _():
        h = _ln(x_ref[0]) * (1.0 + sc_ref[0]) + sh_ref[0]
        hb = h.astype(BF16)
        h_scr[...] = hb
        fl = _nt_dot(hb, wf_ref[...]) + bf_ref[...]
        lfc_ref[0] = _log_sigmoid(fl)[:, :N_HEADS]
        lfr_ref[0] = _log_sigmoid(_nt_dot(wft_ref[...], hb) + bfr_ref[...])
        q_ref[0] = (_nt_dot(hb, w_ref[:ATT_WIDTH, :]) * QSCALE).astype(BF16)

    def proj(w, group):
        return _nt_dot(h_scr[...], w[group * ATT_WIDTH:(group + 1) * ATT_WIDTH, :])

    @pl.when(j == 1)
    def _():
        acc = proj(w_ref, 1)
        kf_ref[0] = acc
        kb_ref[0] = acc.astype(BF16)

    @pl.when(j == 2)
    def _():
        acc = proj(w_ref, 2)
        vf_ref[0] = acc
        vb_ref[0] = acc.astype(BF16)

    @pl.when(j == 3)
    def _():
        u_ref[0] = proj(wu_ref, 0)


def _inproj(x, shift, scale, w_qkv, w_u, w_f, w_ft, b_f, tm):
    nb, l, _ = x.shape
    r = shift.shape[1]
    mod_map = (lambda b, i, j: (b, 0, 0)) if r == 1 else (lambda b, i, j: (b, i, 0))
    row_map = lambda b, i, j: (b, i, 0)
    const2 = lambda b, i, j: (0, 0)
    wide = lambda dt: jax.ShapeDtypeStruct((nb, l, ATT_WIDTH), dt)
    return pl.pallas_call(
        _inproj_kernel,
        grid=(nb, l // tm, 4),
        in_specs=[pl.BlockSpec((1, tm, D_MODEL), row_map),
                  pl.BlockSpec((1, r, D_MODEL), mod_map),
                  pl.BlockSpec((1, r, D_MODEL), mod_map),
                  pl.BlockSpec((3 * ATT_WIDTH, D_MODEL), const2, pipeline_mode=pl.Buffered(1)),
                  pl.BlockSpec((SSM_WIDTH, D_MODEL), const2, pipeline_mode=pl.Buffered(1)),
                  pl.BlockSpec((LANE, D_MODEL), const2),
                  pl.BlockSpec((N_HEADS, D_MODEL), const2),
                  pl.BlockSpec((1, LANE), const2),
                  pl.BlockSpec((N_HEADS, 1), const2)],
        out_specs=[pl.BlockSpec((1, tm, ATT_WIDTH), row_map)] * 6
                  + [pl.BlockSpec((1, tm, N_HEADS), row_map),
                     pl.BlockSpec((1, N_HEADS, tm), lambda b, i, j: (b, 0, i))],
        out_shape=(wide(BF16), wide(F32), wide(BF16), wide(F32), wide(BF16), wide(F32),
                   jax.ShapeDtypeStruct((nb, l, N_HEADS), F32),
                   jax.ShapeDtypeStruct((nb, N_HEADS, l), F32)),
        scratch_shapes=[pltpu.VMEM((tm, D_MODEL), BF16)],
        compiler_params=_cparams(("arbitrary", "arbitrary", "arbitrary")),
        name="inproj",
    )(x, shift, scale, w_qkv, w_u, w_f, w_ft, jnp.pad(b_f, (0, LANE - N_HEADS)).reshape(1, LANE),
      b_f.reshape(N_HEADS, 1))


CUM_BLK = 512


def _cumsum_kernel(lfr_ref, fr_ref):
    s_len = lfr_ref.shape[2]
    r = lax.broadcasted_iota(jnp.int32, (CUM_BLK, CUM_BLK), 0)
    c = lax.broadcasted_iota(jnp.int32, (CUM_BLK, CUM_BLK), 1)
    tri_u = (r <= c).astype(F32)
    carry = jnp.zeros((N_HEADS, 1), F32)
    for blk in range(s_len // CUM_BLK):
        cols = slice(blk * CUM_BLK, (blk + 1) * CUM_BLK)
        fr = jnp.dot(lfr_ref[0, :, cols], tri_u, precision=lax.Precision.HIGHEST,
                     preferred_element_type=F32) + carry
        fr_ref[0, :, cols] = fr * LOG2E
        carry = fr[:, CUM_BLK - 1:CUM_BLK]


def _cumsum(lfr):
    nb, _, s_len = lfr.shape
    rspec = pl.BlockSpec((1, N_HEADS, s_len), lambda b: (b, 0, 0))
    return pl.pallas_call(
        _cumsum_kernel, grid=(nb,), in_specs=[rspec], out_specs=rspec,
        out_shape=jax.ShapeDtypeStruct(lfr.shape, F32),
        compiler_params=_cparams(("arbitrary",)), name="cumsum",
    )(lfr)


ATT_BLK = 512


def _attn_kernel(q_ref, k_ref, v_ref, fr_ref, *refs):
    n_w = (len(refs) - 2) // 2
    w_refs, o_ref, wb_refs, s_scr = refs[:n_w], refs[n_w], refs[n_w + 1:2 * n_w + 1], refs[2 * n_w + 1]
    for w_ref, wb_ref in zip(w_refs, wb_refs):
        wb_ref[...] = w_ref[...].astype(BF16)
    h = pl.program_id(1)
    qi = pl.program_id(2)
    t = ATT_BLK
    q = q_ref[0]

    def scores(j):
        start = pl.multiple_of(j * t, t)
        fk = fr_ref[0, pl.ds(h, 1), pl.ds(start, t)]
        return _nt_dot(q, k_ref[0, pl.ds(start, t), :]) - fk

    def values(j):
        return v_ref[0, pl.ds(pl.multiple_of(j * t, t), t), :]

    def stage(j, carry, slot):
        s_scr[1 - slot] = scores(j + 1)
        return _softmax_step(carry, s_scr[slot], values(j))

    def pair(jj, carry):
        return stage(2 * jj + 1, stage(2 * jj, carry, 0), 1)

    def diagonal(carry, slot):
        row = lax.broadcasted_iota(jnp.int32, (t, t), 0)
        col = lax.broadcasted_iota(jnp.int32, (t, t), 1)
        return _softmax_step(carry, jnp.where(col <= row, s_scr[slot], -jnp.inf), values(qi))

    s_scr[0] = scores(0)
    init = (jnp.full((t, 1), -jnp.inf, F32), jnp.zeros((t, 1), F32), jnp.zeros((t, HEAD_DIM), F32))
    carry = lax.fori_loop(0, qi // 2, pair, init)
    m, l, acc = lax.cond(qi % 2 == 1,
                         lambda c: diagonal(stage(qi - 1, c, 0), 1),
                         lambda c: diagonal(c, 0), carry)
    o_ref[0] = (acc / l).astype(BF16)


def _attn(q, k, v, fr, weights):
    nb, s_len, _ = q.shape
    t = ATT_BLK
    n_q = s_len // t
    n_steps = nb * N_HEADS * n_q
    qspec = pl.BlockSpec((1, t, HEAD_DIM), lambda b, h, i: (b, i, h))
    kvspec = pl.BlockSpec((1, s_len, HEAD_DIM), lambda b, h, i: (b, 0, h))
    wspecs = [pl.BlockSpec((w.shape[0] // n_steps, w.shape[1]), lambda b, h, i: ((b * N_HEADS + h) * n_q + i, 0))
              for w in weights]
    assert all(w.shape[0] % (16 * n_steps) == 0 for w in weights)
    return pl.pallas_call(
        _attn_kernel,
        grid=(nb, N_HEADS, n_q),
        in_specs=[qspec, kvspec, kvspec,
                  pl.BlockSpec((1, N_HEADS, s_len), lambda b, h, i: (b, 0, 0))] + wspecs,
        out_specs=[qspec] + wspecs,
        out_shape=[jax.ShapeDtypeStruct((nb, s_len, ATT_WIDTH), BF16)]
                  + [jax.ShapeDtypeStruct(w.shape, BF16) for w in weights],
        scratch_shapes=[pltpu.VMEM((2, t, t), F32)],
        compiler_params=_cparams(("arbitrary", "arbitrary", "arbitrary")),
        name="attn",
    )(q, k, v, fr, *weights)


PAGE_LANES = PAGE * N_HEADS


def _decay_kernel(lf_ref, wt_ref):
    x = lf_ref[...]
    n = x.shape[1]
    lane = lax.broadcasted_iota(jnp.int32, x.shape, 1)
    suffix, total = x, x
    k = N_HEADS
    while k < n:
        suffix = suffix + jnp.where(lane < n - k, pltpu.roll(suffix, n - k, axis=1), 0.0)
        total = total + pltpu.roll(total, k, axis=1)
        k *= 2
    wt_ref[:, 0, :] = suffix - x
    wt_ref[:, 1, :] = total


def _decay(lf_flat):
    n_pool, n = lf_flat.shape
    rb = 256
    return pl.pallas_call(
        _decay_kernel,
        grid=(n_pool // rb,),
        in_specs=[pl.BlockSpec((rb, n), lambda i: (i, 0))],
        out_specs=pl.BlockSpec((rb, 2, n), lambda i: (i, 0, 0)),
        out_shape=jax.ShapeDtypeStruct((n_pool, 2, n), F32),
        compiler_params=_cparams(("arbitrary",)),
        name="decay",
    )(lf_flat)


def _attn_s_init(q, kn_ref, vn_ref, lfn_ref, bias_scr, m_scr, l_scr, acc_scr, carry_scr):
    rows = q.shape[0]
    f = jnp.broadcast_to(lfn_ref[0], (N_HEADS, LANE))
    lane8 = lax.broadcasted_iota(jnp.int32, f.shape, 1)
    k = N_HEADS
    while k < rows:
        f = f + jnp.where(lane8 >= k, pltpu.roll(f, k, axis=1), 0.0)
        k *= 2
    f_row = f[0:1, :] * LOG2E
    r = lax.broadcasted_iota(jnp.int32, (rows, LANE), 0)
    c = lax.broadcasted_iota(jnp.int32, (rows, LANE), 1)
    rr = lax.broadcasted_iota(jnp.int32, (rows, PAGE_LANES), 0)
    cc = lax.broadcasted_iota(jnp.int32, (rows, PAGE_LANES), 1)
    same_head = jnp.bitwise_and(rr, N_HEADS - 1) == jnp.bitwise_and(cc, N_HEADS - 1)
    bias_scr[...] = jnp.where(same_head, 0.0, -jnp.inf)
    pad = jnp.zeros((LANE - rows, HEAD_DIM), BF16)
    kn = jnp.concatenate([kn_ref[0].astype(BF16), pad], axis=0)
    vn = jnp.concatenate([vn_ref[0].astype(BF16), pad], axis=0)
    tok_r = jnp.right_shift(r, 3)
    tok_c = jnp.right_shift(c, 3)
    head_ok = jnp.bitwise_and(r, N_HEADS - 1) == jnp.bitwise_and(c, N_HEADS - 1)
    s = _nt_dot(q, kn) - f_row
    s = jnp.where(head_ok, jnp.where(tok_c <= tok_r, s, -jnp.inf), -jnp.inf)
    init = (jnp.full((rows, 1), -jnp.inf, F32), jnp.zeros((rows, 1), F32), jnp.zeros((rows, HEAD_DIM), F32))
    m_scr[...], l_scr[...], acc_scr[...] = _softmax_step(init, s, vn)
    carry_scr[...] = jnp.zeros_like(carry_scr)


def _attn_s_scores(q, kp_refs, wt_refs, bias_scr, carry_scr):
    carry = carry_scr[...]
    bias = bias_scr[...]
    scores = []
    for kp_ref, wt_ref in zip(kp_refs, wt_refs):
        wt = wt_ref[0]
        dec = (wt[0:1, :] + carry) * LOG2E
        carry = carry + wt[1:2, :]
        k = kp_ref[...].reshape(PAGE_LANES, HEAD_DIM).astype(BF16)
        scores.append(_nt_dot(q, k) + bias + dec)
    carry_scr[...] = carry
    return jnp.concatenate(scores, axis=1)


def _attn_s_values(vp_refs):
    return jnp.concatenate([vp_ref[...].reshape(PAGE_LANES, HEAD_DIM).astype(BF16) for vp_ref in vp_refs], axis=0)


N_SEG = 8
S5_TB = 64
S5_PITCH = 520


def _cmul(ar, ai, br, bi):
    return ar * br - ai * bi, ar * bi + ai * br


def _s5_tail(h, u, wc, dsk, wg, bg):
    y = _dot_row_halves(h.astype(BF16), wc) + dsk * u
    z = jax.nn.gelu(y)
    zz = _dot_row_halves(z.astype(BF16), wg) + bg
    return zz[:, :CH_LANES] * jax.nn.sigmoid(zz[:, CH_LANES:])


def _s5_kernel(u_ref, wb_ref, wc_ref, wg_ref, bg_ref, dsk_ref, lam_ref, y_ref, hfin_ref, ubuf, ybuf, bu_scr):
    s_len = u_ref.shape[1]
    seg = s_len // N_SEG
    n_tb = seg // S5_TB
    assert seg <= S5_PITCH and seg % S5_TB == 0
    for i in range(N_SEG):
        ubuf[i * S5_PITCH:i * S5_PITCH + seg, :] = u_ref[0, i * seg:(i + 1) * seg, :]
    wb = wb_ref[0]
    lam = lam_ref[0]
    l_re = jnp.broadcast_to(lam[:, :ST_LANES], (N_SEG, ST_LANES))
    l_im = jnp.broadcast_to(lam[:, ST_LANES:], (N_SEG, ST_LANES))

    def load_u(tb):
        t0 = tb * S5_TB
        return jnp.concatenate([ubuf[pl.ds(t0 + j, N_SEG, stride=S5_PITCH), :] for j in range(S5_TB)], axis=0)

    def project(tb, slot):
        u = load_u(jnp.minimum(tb, n_tb - 1))
        bu_scr[slot] = jnp.dot(u.astype(BF16), wb, preferred_element_type=F32)

    def scan_block(slot, h_re, h_im, keep):
        hs = []
        for j in range(S5_TB):
            n_re, n_im = _cmul(l_re, l_im, h_re, h_im)
            h_re = n_re + bu_scr[slot, j * N_SEG:(j + 1) * N_SEG, :ST_LANES]
            h_im = n_im + bu_scr[slot, j * N_SEG:(j + 1) * N_SEG, ST_LANES:]
            if keep:
                hs.append(jnp.concatenate([h_re, h_im], axis=1))
        return h_re, h_im, hs

    def two_blocks(stage):
        assert n_tb % 2 == 0
        return lambda jj, carry: stage(2 * jj + 1, 1, stage(2 * jj, 0, carry))

    def stage1(tb, slot, carry):
        project(tb + 1, 1 - slot)
        h_re, h_im, _ = scan_block(slot, *carry, keep=False)
        return h_re, h_im

    zero = jnp.zeros((N_SEG, ST_LANES), F32)
    project(0, 0)
    f_re, f_im = lax.fori_loop(0, n_tb // 2, two_blocks(stage1), (zero, zero))
    p_re, p_im = lam[:, :ST_LANES], lam[:, ST_LANES:]
    n_sq = seg.bit_length() - 1
    assert 1 << n_sq == seg
    for _ in range(n_sq):
        p_re, p_im = _cmul(p_re, p_im, p_re, p_im)
    c_re, c_im = [jnp.zeros((1, ST_LANES), F32)], [jnp.zeros((1, ST_LANES), F32)]
    for i in range(1, N_SEG):
        d_re, d_im = _cmul(p_re, p_im, c_re[-1], c_im[-1])
        c_re.append(f_re[i - 1:i, :] + d_re)
        c_im.append(f_im[i - 1:i, :] + d_im)
    wc, wg, bg, dsk = wc_ref[0], wg_ref[0], bg_ref[0], dsk_ref[0]

    def stage2(tb, slot, carry):
        project(tb + 1, 1 - slot)
        h_re, h_im, hs = scan_block(slot, *carry, keep=True)
        y = _s5_tail(jnp.concatenate(hs, axis=0), load_u(tb), wc, dsk, wg, bg)
        t0 = tb * S5_TB
        for j in range(S5_TB):
            ybuf[pl.ds(t0 + j, N_SEG, stride=S5_PITCH), :] = y[j * N_SEG:(j + 1) * N_SEG, :]
        return h_re, h_im

    project(0, 0)
    h_re, h_im = lax.fori_loop(0, n_tb // 2, two_blocks(stage2),
                               (jnp.concatenate(c_re, axis=0), jnp.concatenate(c_im, axis=0)))
    hfin_ref[0, 0] = jnp.concatenate([h_re[N_SEG - 1:, :], h_im[N_SEG - 1:, :]], axis=1)
    for i in range(N_SEG):
        y_ref[0, i * seg:(i + 1) * seg, :] = ybuf[i * S5_PITCH:i * S5_PITCH + seg, :].astype(BF16)


def _s5(u, wb, wc, wg, bg, dsk, lam):
    nb, s_len, _ = u.shape
    blk = lambda shape: pl.BlockSpec((1,) + shape, lambda b, g: (g, 0, 0))
    y, hfin = pl.pallas_call(
        _s5_kernel,
        grid=(nb, N_GBLK),
        in_specs=[pl.BlockSpec((1, s_len, CH_LANES), lambda b, g: (b, 0, g)),
                  blk((CH_LANES, 2 * ST_LANES)), blk((2 * ST_LANES, CH_LANES)),
                  blk((CH_LANES, 2 * CH_LANES)), blk((1, 2 * CH_LANES)), blk((1, CH_LANES)),
                  blk((1, 2 * ST_LANES))],
        out_specs=[pl.BlockSpec((1, s_len, CH_LANES), lambda b, g: (b, 0, g)),
                   pl.BlockSpec((1, 1, 1, 2 * ST_LANES), lambda b, g: (b, g, 0, 0))],
        out_shape=(jax.ShapeDtypeStruct((nb, s_len, SSM_WIDTH), BF16),
                   jax.ShapeDtypeStruct((nb, N_GBLK, 1, 2 * ST_LANES), F32)),
        scratch_shapes=[pltpu.VMEM((N_SEG * S5_PITCH, CH_LANES), F32),
                        pltpu.VMEM((N_SEG * S5_PITCH, CH_LANES), F32),
                        pltpu.VMEM((2, S5_TB * N_SEG, 2 * ST_LANES), F32)],
        compiler_params=_cparams(("arbitrary", "arbitrary")),
        name="s5",
    )(u, wb.astype(BF16), wc.astype(BF16), wg.astype(BF16), bg, dsk, lam)
    hfin = hfin.reshape(nb, N_GBLK, 2, GROUPS_PER_BLOCK, SSM_STATE)
    return y, hfin[:, :, 0].reshape(nb, N_GROUPS, SSM_STATE), hfin[:, :, 1].reshape(nb, N_GROUPS, SSM_STATE)


def _s5s_kernel(u_ref, h0_ref, wb_ref, wc_ref, wg_ref, bg_ref, dsk_ref, lam_ref, y_ref, hfin_ref, *, nb):
    hi = lax.Precision.HIGHEST
    u = u_ref[...]
    n_t = u.shape[0] // nb
    bu = jnp.dot(u, wb_ref[0], precision=hi, preferred_element_type=F32)
    lam = lam_ref[0]
    l_re = jnp.broadcast_to(lam[:, :ST_LANES], (nb, ST_LANES))
    l_im = jnp.broadcast_to(lam[:, ST_LANES:], (nb, ST_LANES))
    h0 = h0_ref[0]
    h_re, h_im = h0[:, :ST_LANES], h0[:, ST_LANES:]
    hs = []
    for t in range(n_t):
        n_re, n_im = _cmul(l_re, l_im, h_re, h_im)
        h_re = n_re + bu[t * nb:(t + 1) * nb, :ST_LANES]
        h_im = n_im + bu[t * nb:(t + 1) * nb, ST_LANES:]
        hs.append(jnp.concatenate([h_re, h_im], axis=1))
    hfin_ref[0] = hs[-1]
    h = jnp.concatenate(hs, axis=0)
    y = jnp.dot(h, wc_ref[0], precision=hi, preferred_element_type=F32) + dsk_ref[0] * u
    z = jax.nn.gelu(y)
    zz = jnp.dot(z, wg_ref[0], precision=hi, preferred_element_type=F32) + bg_ref[0]
    y_ref[...] = (zz[:, :CH_LANES] * jax.nn.sigmoid(zz[:, CH_LANES:])).astype(BF16)


def _s5s(u_tm, h0, wb, wc, wg, bg, dsk, lam, nb):
    rows = u_tm.shape[0]
    blk = lambda shape: pl.BlockSpec((1,) + shape, lambda g: (g, 0, 0))
    return pl.pallas_call(
        functools.partial(_s5s_kernel, nb=nb),
        grid=(N_GBLK,),
        in_specs=[pl.BlockSpec((rows, CH_LANES), lambda g: (0, g)),
                  blk((nb, 2 * ST_LANES)),
                  blk((CH_LANES, 2 * ST_LANES)), blk((2 * ST_LANES, CH_LANES)),
                  blk((CH_LANES, 2 * CH_LANES)), blk((1, 2 * CH_LANES)), blk((1, CH_LANES)),
                  blk((1, 2 * ST_LANES))],
        out_specs=[pl.BlockSpec((rows, CH_LANES), lambda g: (0, g)), blk((nb, 2 * ST_LANES))],
        out_shape=(jax.ShapeDtypeStruct((rows, SSM_WIDTH), BF16),
                   jax.ShapeDtypeStruct((N_GBLK, nb, 2 * ST_LANES), F32)),
        compiler_params=_cparams(("arbitrary",)),
        name="s5s",
    )(u_tm, h0, wb, wc, wg, bg, dsk, lam)


def _outproj_kernel(att_ref, ssm_ref, x_ref, g1_ref, sc2_ref, sh2_ref, lng_ref, lnb_ref, wo_ref,
                    x1_ref, h2_ref):
    tm = x_ref.shape[1]
    n_sub = 2 if tm >= 512 else 1
    sub = tm // n_sub

    def mod_rows(ref, rows):
        return ref[0] if ref.shape[1] == 1 else ref[0, rows, :]

    for c in range(n_sub):
        rows = slice(c * sub, (c + 1) * sub)
        mix = (jnp.dot(att_ref[0, rows, :], wo_ref[:ATT_WIDTH, :], preferred_element_type=F32)
               + jnp.dot(ssm_ref[0, rows, :], wo_ref[ATT_WIDTH:, :], preferred_element_type=F32))
        x1 = (_ln(ALPHA * x_ref[0, rows, :] + (1.0 + mod_rows(g1_ref, rows)) * mix) * lng_ref[...]
              + lnb_ref[...])
        x1_ref[0, rows, :] = x1
        h2_ref[0, rows, :] = (_ln(x1) * (1.0 + mod_rows(sc2_ref, rows)) + mod_rows(sh2_ref, rows)).astype(BF16)


def _outproj(att, ssm, x, gate1, scale2, shift2, ln_g, ln_b, w_o, tm):
    nb, l, _ = x.shape
    r = gate1.shape[1]
    mod_map = (lambda b, i: (b, 0, 0)) if r == 1 else (lambda b, i: (b, i, 0))
    row_map = lambda b, i: (b, i, 0)
    const2 = lambda b, i: (0, 0)
    half = pl.BlockSpec((1, tm, ATT_WIDTH), row_map)
    full = pl.BlockSpec((1, tm, D_MODEL), row_map)
    mod = pl.BlockSpec((1, r, D_MODEL), mod_map)
    vec = pl.BlockSpec((1, D_MODEL), const2)
    return pl.pallas_call(
        _outproj_kernel,
        grid=(nb, l // tm),
        in_specs=[half, half, full, mod, mod, mod, vec, vec,
                  pl.BlockSpec((D_MODEL, D_MODEL), const2)],
        out_specs=[full, full],
        out_shape=(jax.ShapeDtypeStruct((nb, l, D_MODEL), F32),
                   jax.ShapeDtypeStruct((nb, l, D_MODEL), BF16)),
        compiler_params=_cparams(("arbitrary", "arbitrary")),
        name="outproj",
    )(att, ssm, x, gate1, scale2, shift2, ln_g.reshape(1, D_MODEL), ln_b.reshape(1, D_MODEL), w_o)


FFN_TF = 1024
FUSED_TF = 512


def _ffn_zero(f, acc_scr):
    @pl.when(f == 0)
    def _():
        acc_scr[...] = jnp.zeros_like(acc_scr)


def _ffn_accumulate(h2_ref, wu_ref, wd_ref, acc_scr):
    up = jnp.dot(h2_ref[0], wu_ref[...], preferred_element_type=F32)
    act = jnp.square(jnp.maximum(up, 0.0)).astype(BF16)
    acc_scr[...] += jnp.dot(act, wd_ref[...], preferred_element_type=F32)


def _ffn_finish(f, x1_ref, g2_ref, lng_ref, lnb_ref, y_ref, acc_scr):
    @pl.when(f == pl.num_programs(2) - 1)
    def _():
        y_ref[0] = _ln(ALPHA * x1_ref[0] + (1.0 + g2_ref[0]) * acc_scr[...]) * lng_ref[...] + lnb_ref[...]


def _ffn_kernel(h2_ref, x1_ref, g2_ref, lng_ref, lnb_ref, wu_ref, wd_ref, y_ref, acc_scr):
    f = pl.program_id(2)
    _ffn_zero(f, acc_scr)
    _ffn_accumulate(h2_ref, wu_ref, wd_ref, acc_scr)
    _ffn_finish(f, x1_ref, g2_ref, lng_ref, lnb_ref, y_ref, acc_scr)


def _ffn_specs(gate2, tm, tf):
    r = gate2.shape[1]
    wrap = lambda fn: (lambda b, i, f, *_: fn(b, i, f))
    row_map = wrap(lambda b, i, f: (b, i, 0))
    full = pl.BlockSpec((1, tm, D_MODEL), row_map)
    vec = pl.BlockSpec((1, D_MODEL), wrap(lambda b, i, f: (0, 0)))
    mod = pl.BlockSpec((1, r, D_MODEL), wrap((lambda b, i, f: (b, 0, 0)) if r == 1 else (lambda b, i, f: (b, i, 0))))
    in_specs = [full, full, mod, vec, vec,
                pl.BlockSpec((D_MODEL, tf), wrap(lambda b, i, f: (0, f))),
                pl.BlockSpec((tf, D_MODEL), wrap(lambda b, i, f: (f, 0)))]
    return in_specs, full


def _ffn(h2, x1, gate2, ln_g, ln_b, w_up, w_down, tm):
    nb, l, _ = x1.shape
    in_specs, out_spec = _ffn_specs(gate2, tm, FFN_TF)
    return pl.pallas_call(
        _ffn_kernel,
        grid=(nb, l // tm, D_FF // FFN_TF),
        in_specs=in_specs,
        out_specs=out_spec,
        out_shape=jax.ShapeDtypeStruct((nb, l, D_MODEL), F32),
        scratch_shapes=[pltpu.VMEM((tm, D_MODEL), F32)],
        compiler_params=_cparams(("arbitrary", "arbitrary", "arbitrary")),
        name="ffn",
    )(h2, x1, gate2, ln_g.reshape(1, D_MODEL), ln_b.reshape(1, D_MODEL), w_up, w_down)


def _ffn_attn_kernel(pt_ref, h2_ref, x1_ref, g2_ref, lng_ref, lnb_ref, wu_ref, wd_ref,
                     q_ref, kn_ref, vn_ref, lfn_ref, *refs, n_pg, steps_per_seq):
    kp_refs, vp_refs, wt_refs = refs[:n_pg], refs[n_pg:2 * n_pg], refs[2 * n_pg:3 * n_pg]
    y_ref, o_ref, acc_scr, bias_scr, m_scr, l_scr, a_scr, carry_scr = refs[3 * n_pg:]
    f = pl.program_id(2)
    t = (pl.program_id(0) * pl.num_programs(1) + pl.program_id(1)) * pl.num_programs(2) + f
    si = lax.rem(t, steps_per_seq)
    q = q_ref[0]

    @pl.when(si == 0)
    def _():
        _attn_s_init(q, kn_ref, vn_ref, lfn_ref, bias_scr, m_scr, l_scr, a_scr, carry_scr)

    _ffn_zero(f, acc_scr)
    s = _attn_s_scores(q, kp_refs, wt_refs, bias_scr, carry_scr)
    up = jnp.dot(h2_ref[0], wu_ref[...], preferred_element_type=F32)
    m_new, alpha, p, l_new = _softmax_weights(m_scr[...], l_scr[...], s)
    act = jnp.square(jnp.maximum(up, 0.0)).astype(BF16)
    acc_scr[...] += jnp.dot(act, wd_ref[...], preferred_element_type=F32)
    a_new = alpha * a_scr[...] + _dot_depth_halves(p, _attn_s_values(vp_refs))
    m_scr[...], l_scr[...], a_scr[...] = m_new, l_new, a_new

    @pl.when(si == steps_per_seq - 1)
    def _():
        o_ref[0] = a_new / l_new

    _ffn_finish(f, x1_ref, g2_ref, lng_ref, lnb_ref, y_ref, acc_scr)


def _ffn_attn(h2, x1, gate2, ln_g, ln_b, w_up, w_down, tm,
              page_table, q, k_new, v_new, lf_new, cache_k, cache_v, decay_wt):
    nb, l, _ = x1.shape
    nbs, rows, _ = q.shape
    n_pages = page_table.shape[1]
    n_i, n_f = l // tm, D_FF // FUSED_TF
    n_steps = nb * n_i * n_f
    n_pg = nbs * n_pages // n_steps
    steps_per_seq = n_pages // n_pg
    assert n_pg * n_steps == nbs * n_pages and steps_per_seq * n_pg == n_pages

    def step(b, i, f):
        t = (b * n_i + i) * n_f + f
        return lax.div(t, steps_per_seq), lax.rem(t, steps_per_seq)

    seq = lambda b, i, f, pt: (step(b, i, f)[0], 0, 0)

    def page_spec(block, n_lead, pg):
        def idx(b, i, f, pt):
            sq, si = step(b, i, f)
            return (0,) * n_lead + (pt[sq, n_pages - 1 - (si * n_pg + pg)],) + (0,) * (len(block) - n_lead - 1)
        return pl.BlockSpec(block, idx)

    ffn_specs, y_spec = _ffn_specs(gate2, tm, FUSED_TF)
    kv_block = (None, None, PAGE, N_HEADS, HEAD_DIM)
    tok_spec = pl.BlockSpec((1, rows, HEAD_DIM), seq)
    grid_spec = pltpu.PrefetchScalarGridSpec(
        num_scalar_prefetch=1,
        grid=(nb, n_i, n_f),
        in_specs=ffn_specs + [tok_spec, tok_spec, tok_spec, pl.BlockSpec((1, 1, LANE), seq)]
                 + [page_spec(kv_block, 1, pg) for pg in range(n_pg)]
                 + [page_spec(kv_block, 1, pg) for pg in range(n_pg)]
                 + [page_spec((1, 2, PAGE_LANES), 0, pg) for pg in range(n_pg)],
        out_specs=[y_spec, tok_spec],
        scratch_shapes=[pltpu.VMEM((tm, D_MODEL), F32),
                        pltpu.VMEM((rows, PAGE_LANES), F32),
                        pltpu.VMEM((rows, 1), F32),
                        pltpu.VMEM((rows, 1), F32),
                        pltpu.VMEM((rows, HEAD_DIM), F32),
                        pltpu.VMEM((1, PAGE_LANES), F32)])
    return pl.pallas_call(
        functools.partial(_ffn_attn_kernel, n_pg=n_pg, steps_per_seq=steps_per_seq),
        grid_spec=grid_spec,
        out_shape=(jax.ShapeDtypeStruct((nb, l, D_MODEL), F32),
                   jax.ShapeDtypeStruct((nbs, rows, HEAD_DIM), F32)),
        compiler_params=_cparams(("arbitrary", "arbitrary", "arbitrary")),
        name="ffn_attn",
    )(page_table, h2, x1, gate2, ln_g.reshape(1, D_MODEL), ln_b.reshape(1, D_MODEL), w_up, w_down,
      q, k_new, v_new, lf_new, *([cache_k] * n_pg), *([cache_v] * n_pg), *([decay_wt] * n_pg))


def kernel(x_prompt, x_sample, c_prompt, c_sample, cache_k, cache_v, cache_logf, state_ssm_re,
           state_ssm_im, page_table, w_ada, b_ada, w_in, b_f, w_o, a_re, a_im, log_dt, b_re, b_im,
           c_re, c_im, d_skip, w_glu, b_glu, ln1_g, ln1_b, w_up, w_down, ln2_g, ln2_b):
    assert w_ada.shape[0] == DEPTH == 1
    nbp, s_len, _ = x_prompt.shape
    nbs, n_q, _ = x_sample.shape
    n_seq = nbp + nbs

    c_all = jnp.concatenate([c_prompt, c_sample, jnp.zeros((16 - n_seq, D_MODEL), F32)], axis=0)
    mod = _ada(c_all, w_ada[0], b_ada[0]).reshape(16, N_MOD, D_MODEL)
    mod_p = [mod[:nbp, i][:, None, :] for i in range(N_MOD)]
    mod_s = [jnp.repeat(mod[nbp:n_seq, i], n_q, axis=0)[None] for i in range(N_MOD)]

    a = ATT_WIDTH
    w_in_t = w_in[0].T
    w_qkv = w_in_t[:3 * a].astype(BF16)
    w_u = w_in_t[3 * a + N_HEADS:].astype(BF16)
    w_ft = w_in_t[3 * a:3 * a + N_HEADS].astype(BF16)
    w_f = jnp.pad(w_ft, ((0, LANE - N_HEADS), (0, 0)))

    l_re, l_im, bb_re, bb_im = _s5prep(a_re[0], a_im[0], log_dt[0], b_re[0], b_im[0])
    wb, wc, wg, bg, dsk, lam = _s5_weights(l_re, l_im, bb_re, bb_im, c_re[0], c_im[0], d_skip[0],
                                           w_glu[0], b_glu[0])

    rows = nbs * n_q
    xs = x_sample.reshape(1, rows, D_MODEL)
    qs, ks, _, vs, _, us, lfcs, _ = _inproj(xs, mod_s[0], mod_s[1], w_qkv, w_u, w_f, w_ft, b_f[0], tm=rows)
    n_pool = cache_k.shape[1]
    decay_wt = _decay(cache_logf[0].reshape(n_pool, PAGE_LANES))
    per_head = lambda t: t.reshape(nbs, n_q * N_HEADS, HEAD_DIM)
    lf_new = jnp.pad(lfcs.reshape(nbs, 1, n_q * N_HEADS), ((0, 0), (0, 0), (0, LANE - n_q * N_HEADS)))

    q, k, k_b, v, v_b, u, lfc, lfr = _inproj(x_prompt, mod_p[0], mod_p[1], w_qkv, w_u, w_f, w_ft, b_f[0], tm=512)
    att, w_up_b, w_down_b, w_o_b = _attn(q, k_b, v_b, _cumsum(lfr), (w_up[0], w_down[0], w_o[0]))
    ssm, hp_re, hp_im = _s5(u, wb, wc, wg, bg, dsk, lam)
    x1, h2 = _outproj(att, ssm, x_prompt, mod_p[2], mod_p[4], mod_p[3], ln1_g[0], ln1_b[0], w_o_b, tm=512)
    y_p, att_s = _ffn_attn(h2, x1, mod_p[5], ln2_g[0], ln2_b[0], w_up_b, w_down_b, 512,
                           page_table, per_head(qs), per_head(ks), per_head(vs), lf_new, cache_k, cache_v,
                           decay_wt)

    u_tm = us.reshape(nbs, n_q, SSM_WIDTH).transpose(1, 0, 2).reshape(rows, SSM_WIDTH)
    h0 = jnp.concatenate([state_ssm_re[0].reshape(nbs, N_GBLK, ST_LANES),
                          state_ssm_im[0].reshape(nbs, N_GBLK, ST_LANES)], axis=2).transpose(1, 0, 2)
    ssm_tm, hs_fin = _s5s(u_tm, h0, wb, wc, wg, bg, dsk, lam, nbs)
    ssm_s = ssm_tm.reshape(n_q, nbs, SSM_WIDTH).transpose(1, 0, 2).reshape(1, rows, SSM_WIDTH)
    hs_fin = hs_fin.transpose(1, 0, 2).reshape(nbs, N_GBLK, 2, GROUPS_PER_BLOCK, SSM_STATE)
    hs_re = hs_fin[:, :, 0].reshape(nbs, N_GROUPS, SSM_STATE)
    hs_im = hs_fin[:, :, 1].reshape(nbs, N_GROUPS, SSM_STATE)
    x1s, h2s = _outproj(att_s.reshape(1, rows, a).astype(BF16), ssm_s, xs, mod_s[2], mod_s[4], mod_s[3],
                        ln1_g[0], ln1_b[0], w_o_b, tm=rows)
    y_s = _ffn(h2s, x1s, mod_s[5], ln2_g[0], ln2_b[0], w_up_b, w_down_b, tm=rows)

    hd = (N_HEADS, HEAD_DIM)
    return (y_p, y_s.reshape(nbs, n_q, D_MODEL),
            k.reshape(1, nbp, s_len, *hd), v.reshape(1, nbp, s_len, *hd), lfc[None],
            hp_re[None], hp_im[None],
            ks.reshape(1, nbs, n_q, *hd), vs.reshape(1, nbs, n_q, *hd), lfcs.reshape(1, nbs, n_q, N_HEADS),
            hs_re[None], hs_im[None])
```

```python
import functools
import math

import jax
import jax.numpy as jnp
from jax import lax
from jax.experimental import pallas as pl
from jax.experimental.pallas import tpu as pltpu

F32 = jnp.float32
BF16 = jnp.bfloat16

LANE = 128
D_MODEL = 2048
ATT_WIDTH = 1024
SSM_WIDTH = 1024
HEAD_DIM = 128
N_HEADS = 8
SSM_GROUP = 16
N_GROUPS = 64
SSM_STATE = 64
D_FF = 8192
N_MOD = 6
PAGE = 128
DEPTH = 1
ALPHA = (2 * DEPTH) ** 0.25
LN_EPS = 1e-5
LOG2E = math.log2(math.e)
QSCALE = HEAD_DIM ** -0.5 * LOG2E

GROUPS_PER_BLOCK = 8
N_GBLK = N_GROUPS // GROUPS_PER_BLOCK
ST_LANES = GROUPS_PER_BLOCK * SSM_STATE
CH_LANES = GROUPS_PER_BLOCK * SSM_GROUP

VMEM_LIMIT = 56 * 1024 * 1024


def _cparams(sem):
    return pltpu.CompilerParams(dimension_semantics=sem, vmem_limit_bytes=VMEM_LIMIT)


def _ln(x):
    mu = jnp.mean(x, axis=-1, keepdims=True)
    xc = x - mu
    var = jnp.mean(xc * xc, axis=-1, keepdims=True)
    return xc * lax.rsqrt(var + LN_EPS)


def _log_sigmoid(x):
    return jnp.minimum(x, 0.0) - jnp.log1p(jnp.exp(-jnp.abs(x)))


def _nt_dot(a, b):
    return lax.dot_general(a, b, (((1,), (1,)), ((), ())), preferred_element_type=F32)


def _softmax_weights(m, l, s):
    m_new = jnp.maximum(m, jnp.max(s, axis=1, keepdims=True))
    alpha = jnp.exp2(m - m_new)
    p = jnp.exp2(s - m_new)
    return m_new, alpha, p.astype(BF16), alpha * l + jnp.sum(p, axis=1, keepdims=True)


def _dot_row_halves(a, b):
    half = a.shape[0] // 2
    return jnp.concatenate([jnp.dot(a[:half], b, preferred_element_type=F32),
                            jnp.dot(a[half:], b, preferred_element_type=F32)], axis=0)


def _dot_depth_halves(a, b):
    half = a.shape[1] // 2
    return (jnp.dot(a[:, :half], b[:half], preferred_element_type=F32)
            + jnp.dot(a[:, half:], b[half:], preferred_element_type=F32))


def _softmax_step(carry, s, v):
    m, l, acc = carry
    m_new, alpha, p, l_new = _softmax_weights(m, l, s)
    return m_new, l_new, alpha * acc + jnp.dot(p, v, preferred_element_type=F32)


def _ada_kernel(c_ref, w_ref, b_ref, o_ref):
    c = c_ref[...]
    s = c * jax.nn.sigmoid(c)
    o_ref[...] = jnp.dot(s, w_ref[...], preferred_element_type=F32) + b_ref[...]


def _ada(c_all, w_ada, b_ada):
    rows = c_all.shape[0]
    n = w_ada.shape[1]
    tn = 1024
    return pl.pallas_call(
        _ada_kernel,
        grid=(n // tn,),
        in_specs=[pl.BlockSpec((rows, D_MODEL), lambda j: (0, 0)),
                  pl.BlockSpec((D_MODEL, tn), lambda j: (0, j)),
                  pl.BlockSpec((1, tn), lambda j: (0, j))],
        out_specs=pl.BlockSpec((rows, tn), lambda j: (0, j)),
        out_shape=jax.ShapeDtypeStruct((rows, n), F32),
        compiler_params=_cparams(("arbitrary",)),
        name="ada",
    )(c_all, w_ada, b_ada.reshape(1, n))


SHIFT1, SCALE1, GATE1, SHIFT2, SCALE2, GATE2 = range(N_MOD)


def _mod_pack(mod_rows, repeat):
    n_seq = mod_rows.shape[0]
    m = jnp.swapaxes(mod_rows, 0, 1)
    if repeat == 1:
        return m.reshape(N_MOD * n_seq, 1, D_MODEL)
    return jnp.repeat(m, repeat, axis=1)


def _mod_spec(mod, chunk, nb):
    r = mod.shape[1]
    return pl.BlockSpec((1, r, D_MODEL), lambda b, i, *_: (chunk * nb + b, 0 if r == 1 else i, 0))


def _s5prep_kernel(are_ref, aim_ref, ldt_ref, arx_ref, aix_ref, ldx_ref, bre_ref, bim_ref,
                   lre_ref, lim_ref, bbre_ref, bbim_ref):
    def lam(a_re, a_im, log_dt):
        dt = jnp.exp(log_dt)
        mag = jnp.exp(a_re * dt)
        return mag * jnp.cos(a_im * dt), mag * jnp.sin(a_im * dt)

    l_re, l_im = lam(are_ref[...], aim_ref[...], ldt_ref[...])
    lre_ref[...] = l_re
    lim_ref[...] = l_im
    a_re, a_im = arx_ref[...], aix_ref[...]
    x_re, x_im = lam(a_re, a_im, ldx_ref[...])
    den = a_re * a_re + a_im * a_im
    n_re = x_re - 1.0
    k_re = (n_re * a_re + x_im * a_im) / den
    k_im = (x_im * a_re - n_re * a_im) / den
    b_re, b_im = bre_ref[...], bim_ref[...]
    bbre_ref[...] = k_re * b_re - k_im * b_im
    bbim_ref[...] = k_re * b_im + k_im * b_re


def _s5prep(a_re, a_im, log_dt, b_re, b_im):
    g, p, c = b_re.shape
    ldt = jnp.broadcast_to(log_dt[:, None], (g, p))
    ex = lambda a: jnp.broadcast_to(a[:, :, None], (g, p, c)).reshape(g, p * c)
    small = jax.ShapeDtypeStruct((g, p), F32)
    big = jax.ShapeDtypeStruct((g, p * c), F32)
    l_re, l_im, bb_re, bb_im = pl.pallas_call(
        _s5prep_kernel, out_shape=(small, small, big, big), name="s5prep",
    )(a_re, a_im, ldt, ex(a_re), ex(a_im), ex(ldt), b_re.reshape(g, p * c), b_im.reshape(g, p * c))
    return l_re, l_im, bb_re.reshape(g, p, c), bb_im.reshape(g, p, c)


def _blockdiag(w):
    g, a, b = w.shape
    w = w.reshape(N_GBLK, GROUPS_PER_BLOCK, a, b)
    eye = jnp.eye(GROUPS_PER_BLOCK, dtype=w.dtype)
    return jnp.einsum('xgab,gh->xgahb', w, eye).reshape(N_GBLK, GROUPS_PER_BLOCK * a, GROUPS_PER_BLOCK * b)


def _s5_weights(l_re, l_im, bb_re, bb_im, c_re, c_im, d_skip, w_glu, b_glu):
    wb = jnp.concatenate([_blockdiag(jnp.swapaxes(bb_re, 1, 2)),
                          _blockdiag(jnp.swapaxes(bb_im, 1, 2))], axis=2)
    wc = jnp.concatenate([_blockdiag(jnp.swapaxes(c_re, 1, 2)),
                          _blockdiag(-jnp.swapaxes(c_im, 1, 2))], axis=1)
    wg = jnp.concatenate([_blockdiag(w_glu[:, :, :SSM_GROUP]),
                          _blockdiag(w_glu[:, :, SSM_GROUP:])], axis=2)
    bg = jnp.concatenate([b_glu[:, :SSM_GROUP].reshape(N_GBLK, 1, CH_LANES),
                          b_glu[:, SSM_GROUP:].reshape(N_GBLK, 1, CH_LANES)], axis=2)
    dsk = d_skip.reshape(N_GBLK, 1, CH_LANES)
    lam = jnp.concatenate([l_re.reshape(N_GBLK, 1, ST_LANES), l_im.reshape(N_GBLK, 1, ST_LANES)], axis=2)
    return wb, wc, wg, bg, dsk, lam


def _inproj_kernel(x_ref, sh_ref, sc_ref, w_ref, wu_ref, wf_ref, wft_ref, bf_ref, bfr_ref,
                   q_ref, kf_ref, kb_ref, vf_ref, vb_ref, u_ref, lfc_ref, lfr_ref, h_scr):
    j = pl.program_id(2)

    @pl.when(j == 0)
    def _():
        h = _ln(x_ref[0]) * (1.0 + sc_ref[0]) + sh_ref[0]
        hb = h.astype(BF16)
        h_scr[...] = hb
        fl = _nt_dot(hb, wf_ref[...]) + bf_ref[...]
        lfc_ref[0] = _log_sigmoid(fl)[:, :N_HEADS]
        lfr_ref[0] = _log_sigmoid(_nt_dot(wft_ref[...], hb) + bfr_ref[...])
        q_ref[0] = (_nt_dot(hb, w_ref[:ATT_WIDTH, :]) * QSCALE).astype(BF16)

    def proj(w, group):
        return _nt_dot(h_scr[...], w[group * ATT_WIDTH:(group + 1) * ATT_WIDTH, :])

    @pl.when(j == 1)
    def _():
        acc = proj(w_ref, 1)
        kf_ref[0] = acc
        kb_ref[0] = acc.astype(BF16)

    @pl.when(j == 2)
    def _():
        acc = proj(w_ref, 2)
        vf_ref[0] = acc
        vb_ref[0] = acc.astype(BF16)

    @pl.when(j == 3)
    def _():
        u_ref[0] = proj(wu_ref, 0)


def _inproj(x, mod, w_qkv, w_u, w_f, w_ft, b_f, tm):
    nb, l, _ = x.shape
    row_map = lambda b, i, j: (b, i, 0)
    const2 = lambda b, i, j: (0, 0)
    wide = lambda dt: jax.ShapeDtypeStruct((nb, l, ATT_WIDTH), dt)
    return pl.pallas_call(
        _inproj_kernel,
        grid=(nb, l // tm, 4),
        in_specs=[pl.BlockSpec((1, tm, D_MODEL), row_map),
                  _mod_spec(mod, SHIFT1, nb),
                  _mod_spec(mod, SCALE1, nb),
                  pl.BlockSpec((3 * ATT_WIDTH, D_MODEL), const2, pipeline_mode=pl.Buffered(1)),
                  pl.BlockSpec((SSM_WIDTH, D_MODEL), const2, pipeline_mode=pl.Buffered(1)),
                  pl.BlockSpec((LANE, D_MODEL), const2),
                  pl.BlockSpec((N_HEADS, D_MODEL), const2),
                  pl.BlockSpec((1, LANE), const2),
                  pl.BlockSpec((N_HEADS, 1), const2)],
        out_specs=[pl.BlockSpec((1, tm, ATT_WIDTH), row_map)] * 6
                  + [pl.BlockSpec((1, tm, N_HEADS), row_map),
                     pl.BlockSpec((1, N_HEADS, tm), lambda b, i, j: (b, 0, i))],
        out_shape=(wide(BF16), wide(F32), wide(BF16), wide(F32), wide(BF16), wide(F32),
                   jax.ShapeDtypeStruct((nb, l, N_HEADS), F32),
                   jax.ShapeDtypeStruct((nb, N_HEADS, l), F32)),
        scratch_shapes=[pltpu.VMEM((tm, D_MODEL), BF16)],
        compiler_params=_cparams(("arbitrary", "arbitrary", "arbitrary")),
        name="inproj",
    )(x, mod, mod, w_qkv, w_u, w_f, w_ft, jnp.pad(b_f, (0, LANE - N_HEADS)).reshape(1, LANE),
      b_f.reshape(N_HEADS, 1))


CUM_BLK = 512


def _cumsum_kernel(lfr_ref, fr_ref):
    s_len = lfr_ref.shape[2]
    r = lax.broadcasted_iota(jnp.int32, (CUM_BLK, CUM_BLK), 0)
    c = lax.broadcasted_iota(jnp.int32, (CUM_BLK, CUM_BLK), 1)
    tri_u = (r <= c).astype(F32)
    carry = jnp.zeros((N_HEADS, 1), F32)
    for blk in range(s_len // CUM_BLK):
        cols = slice(blk * CUM_BLK, (blk + 1) * CUM_BLK)
        fr = jnp.dot(lfr_ref[0, :, cols], tri_u, precision=lax.Precision.HIGHEST,
                     preferred_element_type=F32) + carry
        fr_ref[0, :, cols] = fr * LOG2E
        carry = fr[:, CUM_BLK - 1:CUM_BLK]


def _cumsum(lfr):
    nb, _, s_len = lfr.shape
    rspec = pl.BlockSpec((1, N_HEADS, s_len), lambda b: (b, 0, 0))
    return pl.pallas_call(
        _cumsum_kernel, grid=(nb,), in_specs=[rspec], out_specs=rspec,
        out_shape=jax.ShapeDtypeStruct(lfr.shape, F32),
        compiler_params=_cparams(("arbitrary",)), name="cumsum",
    )(lfr)


ATT_BLK = 512


def _attn_kernel(q_ref, k_ref, v_ref, fr_ref, *refs):
    n_w = (len(refs) - 2) // 2
    w_refs, o_ref, wb_refs, s_scr = refs[:n_w], refs[n_w], refs[n_w + 1:2 * n_w + 1], refs[2 * n_w + 1]
    for w_ref, wb_ref in zip(w_refs, wb_refs):
        wb_ref[...] = w_ref[...].astype(BF16)
    h = pl.program_id(1)
    qi = pl.program_id(2)
    t = ATT_BLK
    q = q_ref[0]

    def scores(j):
        start = pl.multiple_of(j * t, t)
        fk = fr_ref[0, pl.ds(h, 1), pl.ds(start, t)]
        return _nt_dot(q, k_ref[0, pl.ds(start, t), :]) - fk

    def values(j):
        return v_ref[0, pl.ds(pl.multiple_of(j * t, t), t), :]

    def stage(j, carry, slot):
        s_scr[1 - slot] = scores(j + 1)
        return _softmax_step(carry, s_scr[slot], values(j))

    def pair(jj, carry):
        return stage(2 * jj + 1, stage(2 * jj, carry, 0), 1)

    def diagonal(carry, slot):
        row = lax.broadcasted_iota(jnp.int32, (t, t), 0)
        col = lax.broadcasted_iota(jnp.int32, (t, t), 1)
        return _softmax_step(carry, jnp.where(col <= row, s_scr[slot], -jnp.inf), values(qi))

    s_scr[0] = scores(0)
    init = (jnp.full((t, 1), -jnp.inf, F32), jnp.zeros((t, 1), F32), jnp.zeros((t, HEAD_DIM), F32))
    carry = lax.fori_loop(0, qi // 2, pair, init)
    m, l, acc = lax.cond(qi % 2 == 1,
                         lambda c: diagonal(stage(qi - 1, c, 0), 1),
                         lambda c: diagonal(c, 0), carry)
    o_ref[0] = (acc / l).astype(BF16)


def _attn(q, k, v, fr, weights):
    nb, s_len, _ = q.shape
    t = ATT_BLK
    n_q = s_len // t
    n_steps = nb * N_HEADS * n_q
    qspec = pl.BlockSpec((1, t, HEAD_DIM), lambda b, h, i: (b, i, h))
    kvspec = pl.BlockSpec((1, s_len, HEAD_DIM), lambda b, h, i: (b, 0, h))
    wspecs = [pl.BlockSpec((w.shape[0] // n_steps, w.shape[1]), lambda b, h, i: ((b * N_HEADS + h) * n_q + i, 0))
              for w in weights]
    assert all(w.shape[0] % (16 * n_steps) == 0 for w in weights)
    return pl.pallas_call(
        _attn_kernel,
        grid=(nb, N_HEADS, n_q),
        in_specs=[qspec, kvspec, kvspec,
                  pl.BlockSpec((1, N_HEADS, s_len), lambda b, h, i: (b, 0, 0))] + wspecs,
        out_specs=[qspec] + wspecs,
        out_shape=[jax.ShapeDtypeStruct((nb, s_len, ATT_WIDTH), BF16)]
                  + [jax.ShapeDtypeStruct(w.shape, BF16) for w in weights],
        scratch_shapes=[pltpu.VMEM((2, t, t), F32)],
        compiler_params=_cparams(("arbitrary", "arbitrary", "arbitrary")),
        name="attn",
    )(q, k, v, fr, *weights)


PAGE_LANES = PAGE * N_HEADS


def _decay_kernel(lf_ref, wt_ref):
    x = lf_ref[...]
    n = x.shape[1]
    lane = lax.broadcasted_iota(jnp.int32, x.shape, 1)
    suffix, total = x, x
    k = N_HEADS
    while k < n:
        suffix = suffix + jnp.where(lane < n - k, pltpu.roll(suffix, n - k, axis=1), 0.0)
        total = total + pltpu.roll(total, k, axis=1)
        k *= 2
    wt_ref[:, 0, :] = suffix - x
    wt_ref[:, 1, :] = total


def _decay(lf_flat):
    n_pool, n = lf_flat.shape
    rb = 256
    return pl.pallas_call(
        _decay_kernel,
        grid=(n_pool // rb,),
        in_specs=[pl.BlockSpec((rb, n), lambda i: (i, 0))],
        out_specs=pl.BlockSpec((rb, 2, n), lambda i: (i, 0, 0)),
        out_shape=jax.ShapeDtypeStruct((n_pool, 2, n), F32),
        compiler_params=_cparams(("arbitrary",)),
        name="decay",
    )(lf_flat)


def _attn_s_init(q, kn_ref, vn_ref, lfn_ref, bias_scr, m_scr, l_scr, acc_scr, carry_scr):
    rows = q.shape[0]
    f = jnp.broadcast_to(lfn_ref[0], (N_HEADS, LANE))
    lane8 = lax.broadcasted_iota(jnp.int32, f.shape, 1)
    k = N_HEADS
    while k < rows:
        f = f + jnp.where(lane8 >= k, pltpu.roll(f, k, axis=1), 0.0)
        k *= 2
    f_row = f[0:1, :] * LOG2E
    r = lax.broadcasted_iota(jnp.int32, (rows, LANE), 0)
    c = lax.broadcasted_iota(jnp.int32, (rows, LANE), 1)
    rr = lax.broadcasted_iota(jnp.int32, (rows, PAGE_LANES), 0)
    cc = lax.broadcasted_iota(jnp.int32, (rows, PAGE_LANES), 1)
    same_head = jnp.bitwise_and(rr, N_HEADS - 1) == jnp.bitwise_and(cc, N_HEADS - 1)
    bias_scr[...] = jnp.where(same_head, 0.0, -jnp.inf)
    pad = jnp.zeros((LANE - rows, HEAD_DIM), BF16)
    kn = jnp.concatenate([kn_ref[0].astype(BF16), pad], axis=0)
    vn = jnp.concatenate([vn_ref[0].astype(BF16), pad], axis=0)
    tok_r = jnp.right_shift(r, 3)
    tok_c = jnp.right_shift(c, 3)
    head_ok = jnp.bitwise_and(r, N_HEADS - 1) == jnp.bitwise_and(c, N_HEADS - 1)
    s = _nt_dot(q, kn) - f_row
    s = jnp.where(head_ok, jnp.where(tok_c <= tok_r, s, -jnp.inf), -jnp.inf)
    init = (jnp.full((rows, 1), -jnp.inf, F32), jnp.zeros((rows, 1), F32), jnp.zeros((rows, HEAD_DIM), F32))
    m_scr[...], l_scr[...], acc_scr[...] = _softmax_step(init, s, vn)
    carry_scr[...] = jnp.zeros_like(carry_scr)


def _attn_s_scores(q, kp_refs, wt_refs, bias_scr, carry_scr):
    carry = carry_scr[...]
    bias = bias_scr[...]
    scores = []
    for kp_ref, wt_ref in zip(kp_refs, wt_refs):
        wt = wt_ref[0]
        dec = (wt[0:1, :] + carry) * LOG2E
        carry = carry + wt[1:2, :]
        k = kp_ref[...].reshape(PAGE_LANES, HEAD_DIM).astype(BF16)
        scores.append(_nt_dot(q, k) + bias + dec)
    carry_scr[...] = carry
    return jnp.concatenate(scores, axis=1)


def _attn_s_values(vp_refs):
    return jnp.concatenate([vp_ref[...].reshape(PAGE_LANES, HEAD_DIM).astype(BF16) for vp_ref in vp_refs], axis=0)


N_SEG = 8
S5_TB = 64
S5_PITCH = 520


def _cmul(ar, ai, br, bi):
    return ar * br - ai * bi, ar * bi + ai * br


def _s5_tail(h, u, wc, dsk, wg, bg):
    y = _dot_row_halves(h.astype(BF16), wc) + dsk * u
    z = jax.nn.gelu(y)
    zz = _dot_row_halves(z.astype(BF16), wg) + bg
    return zz[:, :CH_LANES] * jax.nn.sigmoid(zz[:, CH_LANES:])


def _s5_kernel(u_ref, wb_ref, wc_ref, wg_ref, bg_ref, dsk_ref, lam_ref, y_ref, hfin_ref, ubuf, ybuf, bu_scr):
    s_len = u_ref.shape[1]
    seg = s_len // N_SEG
    n_tb = seg // S5_TB
    assert seg <= S5_PITCH and seg % S5_TB == 0
    for i in range(N_SEG):
        ubuf[i * S5_PITCH:i * S5_PITCH + seg, :] = u_ref[0, i * seg:(i + 1) * seg, :]
    wb = wb_ref[0].astype(BF16)
    lam = lam_ref[0]
    l_re = jnp.broadcast_to(lam[:, :ST_LANES], (N_SEG, ST_LANES))
    l_im = jnp.broadcast_to(lam[:, ST_LANES:], (N_SEG, ST_LANES))

    def load_u(tb):
        t0 = tb * S5_TB
        return jnp.concatenate([ubuf[pl.ds(t0 + j, N_SEG, stride=S5_PITCH), :] for j in range(S5_TB)], axis=0)

    def project(tb, slot):
        u = load_u(jnp.minimum(tb, n_tb - 1))
        bu_scr[slot] = jnp.dot(u.astype(BF16), wb, preferred_element_type=F32)

    def scan_block(slot, h_re, h_im, keep):
        hs = []
        for j in range(S5_TB):
            n_re, n_im = _cmul(l_re, l_im, h_re, h_im)
            h_re = n_re + bu_scr[slot, j * N_SEG:(j + 1) * N_SEG, :ST_LANES]
            h_im = n_im + bu_scr[slot, j * N_SEG:(j + 1) * N_SEG, ST_LANES:]
            if keep:
                hs.append(jnp.concatenate([h_re, h_im], axis=1))
        return h_re, h_im, hs

    def two_blocks(stage):
        assert n_tb % 2 == 0
        return lambda jj, carry: stage(2 * jj + 1, 1, stage(2 * jj, 0, carry))

    def stage1(tb, slot, carry):
        project(tb + 1, 1 - slot)
        h_re, h_im, _ = scan_block(slot, *carry, keep=False)
        return h_re, h_im

    zero = jnp.zeros((N_SEG, ST_LANES), F32)
    project(0, 0)
    f_re, f_im = lax.fori_loop(0, n_tb // 2, two_blocks(stage1), (zero, zero))
    p_re, p_im = lam[:, :ST_LANES], lam[:, ST_LANES:]
    n_sq = seg.bit_length() - 1
    assert 1 << n_sq == seg
    for _ in range(n_sq):
        p_re, p_im = _cmul(p_re, p_im, p_re, p_im)
    c_re, c_im = [jnp.zeros((1, ST_LANES), F32)], [jnp.zeros((1, ST_LANES), F32)]
    for i in range(1, N_SEG):
        d_re, d_im = _cmul(p_re, p_im, c_re[-1], c_im[-1])
        c_re.append(f_re[i - 1:i, :] + d_re)
        c_im.append(f_im[i - 1:i, :] + d_im)
    wc, wg, bg, dsk = wc_ref[0].astype(BF16), wg_ref[0].astype(BF16), bg_ref[0], dsk_ref[0]

    def stage2(tb, slot, carry):
        project(tb + 1, 1 - slot)
        h_re, h_im, hs = scan_block(slot, *carry, keep=True)
        y = _s5_tail(jnp.concatenate(hs, axis=0), load_u(tb), wc, dsk, wg, bg)
        t0 = tb * S5_TB
        for j in range(S5_TB):
            ybuf[pl.ds(t0 + j, N_SEG, stride=S5_PITCH), :] = y[j * N_SEG:(j + 1) * N_SEG, :]
        return h_re, h_im

    project(0, 0)
    h_re, h_im = lax.fori_loop(0, n_tb // 2, two_blocks(stage2),
                               (jnp.concatenate(c_re, axis=0), jnp.concatenate(c_im, axis=0)))
    hfin_ref[0, 0] = jnp.concatenate([h_re[N_SEG - 1:, :], h_im[N_SEG - 1:, :]], axis=1)
    for i in range(N_SEG):
        y_ref[0, i * seg:(i + 1) * seg, :] = ybuf[i * S5_PITCH:i * S5_PITCH + seg, :].astype(BF16)


def _s5(u, wb, wc, wg, bg, dsk, lam):
    nb, s_len, _ = u.shape
    blk = lambda shape: pl.BlockSpec((1,) + shape, lambda b, g: (g, 0, 0))
    y, hfin = pl.pallas_call(
        _s5_kernel,
        grid=(nb, N_GBLK),
        in_specs=[pl.BlockSpec((1, s_len, CH_LANES), lambda b, g: (b, 0, g)),
                  blk((CH_LANES, 2 * ST_LANES)), blk((2 * ST_LANES, CH_LANES)),
                  blk((CH_LANES, 2 * CH_LANES)), blk((1, 2 * CH_LANES)), blk((1, CH_LANES)),
                  blk((1, 2 * ST_LANES))],
        out_specs=[pl.BlockSpec((1, s_len, CH_LANES), lambda b, g: (b, 0, g)),
                   pl.BlockSpec((1, 1, 1, 2 * ST_LANES), lambda b, g: (b, g, 0, 0))],
        out_shape=(jax.ShapeDtypeStruct((nb, s_len, SSM_WIDTH), BF16),
                   jax.ShapeDtypeStruct((nb, N_GBLK, 1, 2 * ST_LANES), F32)),
        scratch_shapes=[pltpu.VMEM((N_SEG * S5_PITCH, CH_LANES), F32),
                        pltpu.VMEM((N_SEG * S5_PITCH, CH_LANES), F32),
                        pltpu.VMEM((2, S5_TB * N_SEG, 2 * ST_LANES), F32)],
        compiler_params=_cparams(("arbitrary", "arbitrary")),
        name="s5",
    )(u, wb, wc, wg, bg, dsk, lam)
    hfin = hfin.reshape(nb, N_GBLK, 2, GROUPS_PER_BLOCK, SSM_STATE)
    return y, hfin[:, :, 0].reshape(nb, N_GROUPS, SSM_STATE), hfin[:, :, 1].reshape(nb, N_GROUPS, SSM_STATE)


def _s5s_kernel(u_ref, h0_ref, wb_ref, wc_ref, wg_ref, bg_ref, dsk_ref, lam_ref, y_ref, hfin_ref, *, nb):
    hi = lax.Precision.HIGHEST
    u = u_ref[...]
    n_t = u.shape[0] // nb
    bu = jnp.dot(u, wb_ref[0], precision=hi, preferred_element_type=F32)
    lam = lam_ref[0]
    l_re = jnp.broadcast_to(lam[:, :ST_LANES], (nb, ST_LANES))
    l_im = jnp.broadcast_to(lam[:, ST_LANES:], (nb, ST_LANES))
    h0 = h0_ref[0]
    h_re, h_im = h0[:, :ST_LANES], h0[:, ST_LANES:]
    hs = []
    for t in range(n_t):
        n_re, n_im = _cmul(l_re, l_im, h_re, h_im)
        h_re = n_re + bu[t * nb:(t + 1) * nb, :ST_LANES]
        h_im = n_im + bu[t * nb:(t + 1) * nb, ST_LANES:]
        hs.append(jnp.concatenate([h_re, h_im], axis=1))
    hfin_ref[0] = hs[-1]
    h = jnp.concatenate(hs, axis=0)
    y = jnp.dot(h, wc_ref[0], precision=hi, preferred_element_type=F32) + dsk_ref[0] * u
    z = jax.nn.gelu(y)
    zz = jnp.dot(z, wg_ref[0], precision=hi, preferred_element_type=F32) + bg_ref[0]
    y_ref[...] = (zz[:, :CH_LANES] * jax.nn.sigmoid(zz[:, CH_LANES:])).astype(BF16)


def _s5s(u_tm, h0, wb, wc, wg, bg, dsk, lam, nb):
    rows = u_tm.shape[0]
    blk = lambda shape: pl.BlockSpec((1,) + shape, lambda g: (g, 0, 0))
    return pl.pallas_call(
        functools.partial(_s5s_kernel, nb=nb),
        grid=(N_GBLK,),
        in_specs=[pl.BlockSpec((rows, CH_LANES), lambda g: (0, g)),
                  blk((nb, 2 * ST_LANES)),
                  blk((CH_LANES, 2 * ST_LANES)), blk((2 * ST_LANES, CH_LANES)),
                  blk((CH_LANES, 2 * CH_LANES)), blk((1, 2 * CH_LANES)), blk((1, CH_LANES)),
                  blk((1, 2 * ST_LANES))],
        out_specs=[pl.BlockSpec((rows, CH_LANES), lambda g: (0, g)), blk((nb, 2 * ST_LANES))],
        out_shape=(jax.ShapeDtypeStruct((rows, SSM_WIDTH), BF16),
                   jax.ShapeDtypeStruct((N_GBLK, nb, 2 * ST_LANES), F32)),
        compiler_params=_cparams(("arbitrary",)),
        name="s5s",
    )(u_tm, h0, wb, wc, wg, bg, dsk, lam)


def _outproj_kernel(att_ref, ssm_ref, x_ref, g1_ref, sc2_ref, sh2_ref, lng_ref, lnb_ref, wo_ref,
                    x1_ref, h2_ref):
    tm = x_ref.shape[1]
    n_sub = 2 if tm >= 512 else 1
    sub = tm // n_sub

    def mod_rows(ref, rows):
        return ref[0] if ref.shape[1] == 1 else ref[0, rows, :]

    for c in range(n_sub):
        rows = slice(c * sub, (c + 1) * sub)
        mix = (jnp.dot(att_ref[0, rows, :], wo_ref[:ATT_WIDTH, :], preferred_element_type=F32)
               + jnp.dot(ssm_ref[0, rows, :], wo_ref[ATT_WIDTH:, :], preferred_element_type=F32))
        x1 = (_ln(ALPHA * x_ref[0, rows, :] + (1.0 + mod_rows(g1_ref, rows)) * mix) * lng_ref[...]
              + lnb_ref[...])
        x1_ref[0, rows, :] = x1
        h2_ref[0, rows, :] = (_ln(x1) * (1.0 + mod_rows(sc2_ref, rows)) + mod_rows(sh2_ref, rows)).astype(BF16)


def _outproj(att, ssm, x, mod, ln_g, ln_b, w_o, tm):
    nb, l, _ = x.shape
    row_map = lambda b, i: (b, i, 0)
    const2 = lambda b, i: (0, 0)
    half = pl.BlockSpec((1, tm, ATT_WIDTH), row_map)
    full = pl.BlockSpec((1, tm, D_MODEL), row_map)
    vec = pl.BlockSpec((1, D_MODEL), const2)
    return pl.pallas_call(
        _outproj_kernel,
        grid=(nb, l // tm),
        in_specs=[half, half, full,
                  _mod_spec(mod, GATE1, nb), _mod_spec(mod, SCALE2, nb), _mod_spec(mod, SHIFT2, nb), vec, vec,
                  pl.BlockSpec((D_MODEL, D_MODEL), const2)],
        out_specs=[full, full],
        out_shape=(jax.ShapeDtypeStruct((nb, l, D_MODEL), F32),
                   jax.ShapeDtypeStruct((nb, l, D_MODEL), BF16)),
        compiler_params=_cparams(("arbitrary", "arbitrary")),
        name="outproj",
    )(att, ssm, x, mod, mod, mod, ln_g.reshape(1, D_MODEL), ln_b.reshape(1, D_MODEL), w_o)


FFN_TF = 1024
FUSED_TF = 512


def _ffn_zero(f, acc_scr):
    @pl.when(f == 0)
    def _():
        acc_scr[...] = jnp.zeros_like(acc_scr)


def _ffn_accumulate(h2_ref, wu_ref, wd_ref, acc_scr):
    up = jnp.dot(h2_ref[0], wu_ref[...], preferred_element_type=F32)
    act = jnp.square(jnp.maximum(up, 0.0)).astype(BF16)
    acc_scr[...] += jnp.dot(act, wd_ref[...], preferred_element_type=F32)


def _ffn_finish(f, x1_ref, g2_ref, lng_ref, lnb_ref, y_ref, acc_scr):
    @pl.when(f == pl.num_programs(2) - 1)
    def _():
        y_ref[0] = _ln(ALPHA * x1_ref[0] + (1.0 + g2_ref[0]) * acc_scr[...]) * lng_ref[...] + lnb_ref[...]


def _ffn_kernel(h2_ref, x1_ref, g2_ref, lng_ref, lnb_ref, wu_ref, wd_ref, y_ref, acc_scr):
    f = pl.program_id(2)
    _ffn_zero(f, acc_scr)
    _ffn_accumulate(h2_ref, wu_ref, wd_ref, acc_scr)
    _ffn_finish(f, x1_ref, g2_ref, lng_ref, lnb_ref, y_ref, acc_scr)


def _ffn_specs(mod, nb, tm, tf):
    wrap = lambda fn: (lambda b, i, f, *_: fn(b, i, f))
    row_map = wrap(lambda b, i, f: (b, i, 0))
    full = pl.BlockSpec((1, tm, D_MODEL), row_map)
    vec = pl.BlockSpec((1, D_MODEL), wrap(lambda b, i, f: (0, 0)))
    in_specs = [full, full, _mod_spec(mod, GATE2, nb), vec, vec,
                pl.BlockSpec((D_MODEL, tf), wrap(lambda b, i, f: (0, f))),
                pl.BlockSpec((tf, D_MODEL), wrap(lambda b, i, f: (f, 0)))]
    return in_specs, full


def _ffn(h2, x1, mod, ln_g, ln_b, w_up, w_down, tm):
    nb, l, _ = x1.shape
    in_specs, out_spec = _ffn_specs(mod, nb, tm, FFN_TF)
    return pl.pallas_call(
        _ffn_kernel,
        grid=(nb, l // tm, D_FF // FFN_TF),
        in_specs=in_specs,
        out_specs=out_spec,
        out_shape=jax.ShapeDtypeStruct((nb, l, D_MODEL), F32),
        scratch_shapes=[pltpu.VMEM((tm, D_MODEL), F32)],
        compiler_params=_cparams(("arbitrary", "arbitrary", "arbitrary")),
        name="ffn",
    )(h2, x1, mod, ln_g.reshape(1, D_MODEL), ln_b.reshape(1, D_MODEL), w_up, w_down)


def _ffn_attn_kernel(pt_ref, h2_ref, x1_ref, g2_ref, lng_ref, lnb_ref, wu_ref, wd_ref,
                     q_ref, kn_ref, vn_ref, lfn_ref, *refs, n_pg, steps_per_seq):
    kp_refs, vp_refs, wt_refs = refs[:n_pg], refs[n_pg:2 * n_pg], refs[2 * n_pg:3 * n_pg]
    y_ref, o_ref, acc_scr, bias_scr, m_scr, l_scr, a_scr, carry_scr = refs[3 * n_pg:]
    f = pl.program_id(2)
    t = (pl.program_id(0) * pl.num_programs(1) + pl.program_id(1)) * pl.num_programs(2) + f
    si = lax.rem(t, steps_per_seq)
    q = q_ref[0]

    @pl.when(si == 0)
    def _():
        _attn_s_init(q, kn_ref, vn_ref, lfn_ref, bias_scr, m_scr, l_scr, a_scr, carry_scr)

    _ffn_zero(f, acc_scr)
    s = _attn_s_scores(q, kp_refs, wt_refs, bias_scr, carry_scr)
    up = jnp.dot(h2_ref[0], wu_ref[...], preferred_element_type=F32)
    m_new, alpha, p, l_new = _softmax_weights(m_scr[...], l_scr[...], s)
    act = jnp.square(jnp.maximum(up, 0.0)).astype(BF16)
    acc_scr[...] += jnp.dot(act, wd_ref[...], preferred_element_type=F32)
    a_new = alpha * a_scr[...] + _dot_depth_halves(p, _attn_s_values(vp_refs))
    m_scr[...], l_scr[...], a_scr[...] = m_new, l_new, a_new

    @pl.when(si == steps_per_seq - 1)
    def _():
        o_ref[0] = a_new / l_new

    _ffn_finish(f, x1_ref, g2_ref, lng_ref, lnb_ref, y_ref, acc_scr)


def _ffn_attn(h2, x1, mod, ln_g, ln_b, w_up, w_down, tm,
              page_table, q, k_new, v_new, lf_new, cache_k, cache_v, decay_wt):
    nb, l, _ = x1.shape
    nbs, rows, _ = q.shape
    n_pages = page_table.shape[1]
    n_i, n_f = l // tm, D_FF // FUSED_TF
    n_steps = nb * n_i * n_f
    n_pg = nbs * n_pages // n_steps
    steps_per_seq = n_pages // n_pg
    assert n_pg * n_steps == nbs * n_pages and steps_per_seq * n_pg == n_pages

    def step(b, i, f):
        t = (b * n_i + i) * n_f + f
        return lax.div(t, steps_per_seq), lax.rem(t, steps_per_seq)

    seq = lambda b, i, f, pt: (step(b, i, f)[0], 0, 0)

    def page_spec(block, n_lead, pg):
        def idx(b, i, f, pt):
            sq, si = step(b, i, f)
            return (0,) * n_lead + (pt[sq, n_pages - 1 - (si * n_pg + pg)],) + (0,) * (len(block) - n_lead - 1)
        return pl.BlockSpec(block, idx)

    ffn_specs, y_spec = _ffn_specs(mod, nb, tm, FUSED_TF)
    kv_block = (None, None, PAGE, N_HEADS, HEAD_DIM)
    tok_spec = pl.BlockSpec((1, rows, HEAD_DIM), seq)
    grid_spec = pltpu.PrefetchScalarGridSpec(
        num_scalar_prefetch=1,
        grid=(nb, n_i, n_f),
        in_specs=ffn_specs + [tok_spec, tok_spec, tok_spec, pl.BlockSpec((1, 1, LANE), seq)]
                 + [page_spec(kv_block, 1, pg) for pg in range(n_pg)]
                 + [page_spec(kv_block, 1, pg) for pg in range(n_pg)]
                 + [page_spec((1, 2, PAGE_LANES), 0, pg) for pg in range(n_pg)],
        out_specs=[y_spec, tok_spec],
        scratch_shapes=[pltpu.VMEM((tm, D_MODEL), F32),
                        pltpu.VMEM((rows, PAGE_LANES), F32),
                        pltpu.VMEM((rows, 1), F32),
                        pltpu.VMEM((rows, 1), F32),
                        pltpu.VMEM((rows, HEAD_DIM), F32),
                        pltpu.VMEM((1, PAGE_LANES), F32)])
    return pl.pallas_call(
        functools.partial(_ffn_attn_kernel, n_pg=n_pg, steps_per_seq=steps_per_seq),
        grid_spec=grid_spec,
        out_shape=(jax.ShapeDtypeStruct((nb, l, D_MODEL), F32),
                   jax.ShapeDtypeStruct((nbs, rows, HEAD_DIM), F32)),
        compiler_params=_cparams(("arbitrary", "arbitrary", "arbitrary")),
        name="ffn_attn",
    )(page_table, h2, x1, mod, ln_g.reshape(1, D_MODEL), ln_b.reshape(1, D_MODEL), w_up, w_down,
      q, k_new, v_new, lf_new, *([cache_k] * n_pg), *([cache_v] * n_pg), *([decay_wt] * n_pg))


def kernel(x_prompt, x_sample, c_prompt, c_sample, cache_k, cache_v, cache_logf, state_ssm_re,
           state_ssm_im, page_table, w_ada, b_ada, w_in, b_f, w_o, a_re, a_im, log_dt, b_re, b_im,
           c_re, c_im, d_skip, w_glu, b_glu, ln1_g, ln1_b, w_up, w_down, ln2_g, ln2_b):
    assert w_ada.shape[0] == DEPTH == 1
    nbp, s_len, _ = x_prompt.shape
    nbs, n_q, _ = x_sample.shape
    n_seq = nbp + nbs

    c_all = jnp.concatenate([c_prompt, c_sample, jnp.zeros((16 - n_seq, D_MODEL), F32)], axis=0)
    mod = _ada(c_all, w_ada[0], b_ada[0]).reshape(16, N_MOD, D_MODEL)
    mod_p = _mod_pack(mod[:nbp], 1)
    mod_s = _mod_pack(mod[nbp:n_seq], n_q)

    a = ATT_WIDTH
    w_in_t = w_in[0].T
    w_qkv = w_in_t[:3 * a].astype(BF16)
    w_u = w_in_t[3 * a + N_HEADS:].astype(BF16)
    w_ft = w_in_t[3 * a:3 * a + N_HEADS].astype(BF16)
    w_f = jnp.pad(w_ft, ((0, LANE - N_HEADS), (0, 0)))

    l_re, l_im, bb_re, bb_im = _s5prep(a_re[0], a_im[0], log_dt[0], b_re[0], b_im[0])
    wb, wc, wg, bg, dsk, lam = _s5_weights(l_re, l_im, bb_re, bb_im, c_re[0], c_im[0], d_skip[0],
                                           w_glu[0], b_glu[0])

    rows = nbs * n_q
    xs = x_sample.reshape(1, rows, D_MODEL)
    qs, ks, _, vs, _, us, lfcs, _ = _inproj(xs, mod_s, w_qkv, w_u, w_f, w_ft, b_f[0], tm=rows)
    n_pool = cache_k.shape[1]
    decay_wt = _decay(cache_logf[0].reshape(n_pool, PAGE_LANES))
    per_head = lambda t: t.reshape(nbs, n_q * N_HEADS, HEAD_DIM)
    lf_new = jnp.pad(lfcs.reshape(nbs, 1, n_q * N_HEADS), ((0, 0), (0, 0), (0, LANE - n_q * N_HEADS)))

    q, k, k_b, v, v_b, u, lfc, lfr = _inproj(x_prompt, mod_p, w_qkv, w_u, w_f, w_ft, b_f[0], tm=512)
    att, w_up_b, w_down_b, w_o_b = _attn(q, k_b, v_b, _cumsum(lfr), (w_up[0], w_down[0], w_o[0]))
    ssm, hp_re, hp_im = _s5(u, wb, wc, wg, bg, dsk, lam)
    x1, h2 = _outproj(att, ssm, x_prompt, mod_p, ln1_g[0], ln1_b[0], w_o_b, tm=512)
    y_p, att_s = _ffn_attn(h2, x1, mod_p, ln2_g[0], ln2_b[0], w_up_b, w_down_b, 512,
                           page_table, per_head(qs), per_head(ks), per_head(vs), lf_new, cache_k, cache_v,
                           decay_wt)

    u_tm = us.reshape(nbs, n_q, SSM_WIDTH).transpose(1, 0, 2).reshape(rows, SSM_WIDTH)
    h0 = jnp.concatenate([state_ssm_re[0].reshape(nbs, N_GBLK, ST_LANES),
                          state_ssm_im[0].reshape(nbs, N_GBLK, ST_LANES)], axis=2).transpose(1, 0, 2)
    ssm_tm, hs_fin = _s5s(u_tm, h0, wb, wc, wg, bg, dsk, lam, nbs)
    ssm_s = ssm_tm.reshape(n_q, nbs, SSM_WIDTH).transpose(1, 0, 2).reshape(1, rows, SSM_WIDTH)
    hs_fin = hs_fin.transpose(1, 0, 2).reshape(nbs, N_GBLK, 2, GROUPS_PER_BLOCK, SSM_STATE)
    hs_re = hs_fin[:, :, 0].reshape(nbs, N_GROUPS, SSM_STATE)
    hs_im = hs_fin[:, :, 1].reshape(nbs, N_GROUPS, SSM_STATE)
    x1s, h2s = _outproj(att_s.reshape(1, rows, a).astype(BF16), ssm_s, xs, mod_s,
                        ln1_g[0], ln1_b[0], w_o_b, tm=rows)
    y_s = _ffn(h2s, x1s, mod_s, ln2_g[0], ln2_b[0], w_up_b, w_down_b, tm=rows)

    hd = (N_HEADS, HEAD_DIM)
    return (y_p, y_s.reshape(nbs, n_q, D_MODEL),
            k.reshape(1, nbp, s_len, *hd), v.reshape(1, nbp, s_len, *hd), lfc[None],
            hp_re[None], hp_im[None],
            ks.reshape(1, nbs, n_q, *hd), vs.reshape(1, nbs, n_q, *hd), lfcs.reshape(1, nbs, n_q, N_HEADS),
            hs_re[None], hs_im[None])
```

```python
import functools
import math

import jax
import jax.numpy as jnp
from jax import lax
from jax.experimental import pallas as pl
from jax.experimental.pallas import tpu as pltpu

F32 = jnp.float32
BF16 = jnp.bfloat16

LANE = 128
D_MODEL = 2048
ATT_WIDTH = 1024
SSM_WIDTH = 1024
HEAD_DIM = 128
N_HEADS = 8
SSM_GROUP = 16
N_GROUPS = 64
SSM_STATE = 64
D_FF = 8192
N_MOD = 6
PAGE = 128
DEPTH = 1
ALPHA = (2 * DEPTH) ** 0.25
LN_EPS = 1e-5
LOG2E = math.log2(math.e)
QSCALE = HEAD_DIM ** -0.5 * LOG2E

GROUPS_PER_BLOCK = 8
N_GBLK = N_GROUPS // GROUPS_PER_BLOCK
ST_LANES = GROUPS_PER_BLOCK * SSM_STATE
CH_LANES = GROUPS_PER_BLOCK * SSM_GROUP

VMEM_LIMIT = 56 * 1024 * 1024


def _cparams(sem):
    return pltpu.CompilerParams(dimension_semantics=sem, vmem_limit_bytes=VMEM_LIMIT)


def _ln(x):
    mu = jnp.mean(x, axis=-1, keepdims=True)
    xc = x - mu
    var = jnp.mean(xc * xc, axis=-1, keepdims=True)
    return xc * lax.rsqrt(var + LN_EPS)


def _log_sigmoid(x):
    return jnp.minimum(x, 0.0) - jnp.log1p(jnp.exp(-jnp.abs(x)))


def _nt_dot(a, b):
    return lax.dot_general(a, b, (((1,), (1,)), ((), ())), preferred_element_type=F32)


def _softmax_weights(m, l, s):
    m_new = jnp.maximum(m, jnp.max(s, axis=1, keepdims=True))
    alpha = jnp.exp2(m - m_new)
    p = jnp.exp2(s - m_new)
    return m_new, alpha, p.astype(BF16), alpha * l + jnp.sum(p, axis=1, keepdims=True)


def _dot_row_halves(a, b):
    half = a.shape[0] // 2
    return jnp.concatenate([jnp.dot(a[:half], b, preferred_element_type=F32),
                            jnp.dot(a[half:], b, preferred_element_type=F32)], axis=0)


def _dot_depth_halves(a, b):
    half = a.shape[1] // 2
    return (jnp.dot(a[:, :half], b[:half], preferred_element_type=F32)
            + jnp.dot(a[:, half:], b[half:], preferred_element_type=F32))


def _softmax_step(carry, s, v):
    m, l, acc = carry
    m_new, alpha, p, l_new = _softmax_weights(m, l, s)
    return m_new, l_new, alpha * acc + jnp.dot(p, v, preferred_element_type=F32)


def _ada_kernel(c_ref, w_ref, b_ref, o_ref):
    c = c_ref[...]
    s = c * jax.nn.sigmoid(c)
    o_ref[...] = jnp.dot(s, w_ref[...], preferred_element_type=F32) + b_ref[...]


def _ada(c_all, w_ada, b_ada):
    rows = c_all.shape[0]
    n = w_ada.shape[1]
    tn = 1024
    return pl.pallas_call(
        _ada_kernel,
        grid=(n // tn,),
        in_specs=[pl.BlockSpec((rows, D_MODEL), lambda j: (0, 0)),
                  pl.BlockSpec((D_MODEL, tn), lambda j: (0, j)),
                  pl.BlockSpec((1, tn), lambda j: (0, j))],
        out_specs=pl.BlockSpec((rows, tn), lambda j: (0, j)),
        out_shape=jax.ShapeDtypeStruct((rows, n), F32),
        compiler_params=_cparams(("arbitrary",)),
        name="ada",
    )(c_all, w_ada, b_ada.reshape(1, n))


SHIFT1, SCALE1, GATE1, SHIFT2, SCALE2, GATE2 = range(N_MOD)


def _mod_pack(mod_rows, repeat):
    n_seq = mod_rows.shape[0]
    m = jnp.swapaxes(mod_rows, 0, 1)
    if repeat == 1:
        return m.reshape(N_MOD * n_seq, 1, D_MODEL)
    return jnp.repeat(m, repeat, axis=1)


def _mod_spec(mod, chunk, nb):
    r = mod.shape[1]
    return pl.BlockSpec((1, r, D_MODEL), lambda b, i, *_: (chunk * nb + b, 0 if r == 1 else i, 0))


def _s5prep_kernel(are_ref, aim_ref, ldt_ref, arx_ref, aix_ref, ldx_ref, bre_ref, bim_ref,
                   lre_ref, lim_ref, bbre_ref, bbim_ref):
    def lam(a_re, a_im, log_dt):
        dt = jnp.exp(log_dt)
        mag = jnp.exp(a_re * dt)
        return mag * jnp.cos(a_im * dt), mag * jnp.sin(a_im * dt)

    l_re, l_im = lam(are_ref[...], aim_ref[...], ldt_ref[...])
    lre_ref[...] = l_re
    lim_ref[...] = l_im
    a_re, a_im = arx_ref[...], aix_ref[...]
    x_re, x_im = lam(a_re, a_im, ldx_ref[...])
    den = a_re * a_re + a_im * a_im
    n_re = x_re - 1.0
    k_re = (n_re * a_re + x_im * a_im) / den
    k_im = (x_im * a_re - n_re * a_im) / den
    b_re, b_im = bre_ref[...], bim_ref[...]
    bbre_ref[...] = k_re * b_re - k_im * b_im
    bbim_ref[...] = k_re * b_im + k_im * b_re


def _s5prep(a_re, a_im, log_dt, b_re, b_im):
    g, p, c = b_re.shape
    ldt = jnp.broadcast_to(log_dt[:, None], (g, p))
    ex = lambda a: jnp.broadcast_to(a[:, :, None], (g, p, c)).reshape(g, p * c)
    small = jax.ShapeDtypeStruct((g, p), F32)
    big = jax.ShapeDtypeStruct((g, p * c), F32)
    l_re, l_im, bb_re, bb_im = pl.pallas_call(
        _s5prep_kernel, out_shape=(small, small, big, big), name="s5prep",
    )(a_re, a_im, ldt, ex(a_re), ex(a_im), ex(ldt), b_re.reshape(g, p * c), b_im.reshape(g, p * c))
    return l_re, l_im, bb_re.reshape(g, p, c), bb_im.reshape(g, p, c)


def _blockdiag(w):
    g, a, b = w.shape
    w = w.reshape(N_GBLK, GROUPS_PER_BLOCK, a, b)
    eye = jnp.eye(GROUPS_PER_BLOCK, dtype=w.dtype)
    return jnp.einsum('xgab,gh->xgahb', w, eye).reshape(N_GBLK, GROUPS_PER_BLOCK * a, GROUPS_PER_BLOCK * b)


def _s5_weights(l_re, l_im, bb_re, bb_im, c_re, c_im, d_skip, w_glu, b_glu):
    wb = jnp.concatenate([_blockdiag(jnp.swapaxes(bb_re, 1, 2)),
                          _blockdiag(jnp.swapaxes(bb_im, 1, 2))], axis=2)
    wc = jnp.concatenate([_blockdiag(jnp.swapaxes(c_re, 1, 2)),
                          _blockdiag(-jnp.swapaxes(c_im, 1, 2))], axis=1)
    wg = jnp.concatenate([_blockdiag(w_glu[:, :, :SSM_GROUP]),
                          _blockdiag(w_glu[:, :, SSM_GROUP:])], axis=2)
    bg = jnp.concatenate([b_glu[:, :SSM_GROUP].reshape(N_GBLK, 1, CH_LANES),
                          b_glu[:, SSM_GROUP:].reshape(N_GBLK, 1, CH_LANES)], axis=2)
    dsk = d_skip.reshape(N_GBLK, 1, CH_LANES)
    lam = jnp.concatenate([l_re.reshape(N_GBLK, 1, ST_LANES), l_im.reshape(N_GBLK, 1, ST_LANES)], axis=2)
    return wb, wc, wg, bg, dsk, lam


def _inproj_kernel(x_ref, sh_ref, sc_ref, w_ref, wu_ref, wf_ref, wft_ref, bf_ref, bfr_ref,
                   q_ref, kf_ref, kb_ref, vf_ref, vb_ref, u_ref, lfc_ref, lfr_ref):
    hb = (_ln(x_ref[0]) * (1.0 + sc_ref[0]) + sh_ref[0]).astype(BF16)
    lfc_ref[0] = _log_sigmoid(_nt_dot(hb, wf_ref[...]) + bf_ref[...])[:, :N_HEADS]
    lfr_ref[0] = _log_sigmoid(_nt_dot(wft_ref[...], hb) + bfr_ref[...])

    def proj(w, group):
        return _nt_dot(hb, w[group * ATT_WIDTH:(group + 1) * ATT_WIDTH, :])

    q_ref[0] = (proj(w_ref, 0) * QSCALE).astype(BF16)
    k = proj(w_ref, 1)
    kf_ref[0] = k
    kb_ref[0] = k.astype(BF16)
    v = proj(w_ref, 2)
    vf_ref[0] = v
    vb_ref[0] = v.astype(BF16)
    u_ref[0] = proj(wu_ref, 0)


def _inproj(x, mod, w_qkv, w_u, w_f, w_ft, b_f, tm):
    nb, l, _ = x.shape
    row_map = lambda b, i: (b, i, 0)
    const2 = lambda b, i: (0, 0)
    wide = lambda dt: jax.ShapeDtypeStruct((nb, l, ATT_WIDTH), dt)
    return pl.pallas_call(
        _inproj_kernel,
        grid=(nb, l // tm),
        in_specs=[pl.BlockSpec((1, tm, D_MODEL), row_map),
                  _mod_spec(mod, SHIFT1, nb),
                  _mod_spec(mod, SCALE1, nb),
                  pl.BlockSpec((3 * ATT_WIDTH, D_MODEL), const2, pipeline_mode=pl.Buffered(1)),
                  pl.BlockSpec((SSM_WIDTH, D_MODEL), const2, pipeline_mode=pl.Buffered(1)),
                  pl.BlockSpec((LANE, D_MODEL), const2),
                  pl.BlockSpec((N_HEADS, D_MODEL), const2),
                  pl.BlockSpec((1, LANE), const2),
                  pl.BlockSpec((N_HEADS, 1), const2)],
        out_specs=[pl.BlockSpec((1, tm, ATT_WIDTH), row_map)] * 6
                  + [pl.BlockSpec((1, tm, N_HEADS), row_map),
                     pl.BlockSpec((1, N_HEADS, tm), lambda b, i: (b, 0, i))],
        out_shape=(wide(BF16), wide(F32), wide(BF16), wide(F32), wide(BF16), wide(F32),
                   jax.ShapeDtypeStruct((nb, l, N_HEADS), F32),
                   jax.ShapeDtypeStruct((nb, N_HEADS, l), F32)),
        compiler_params=_cparams(("arbitrary", "arbitrary")),
        name="inproj",
    )(x, mod, mod, w_qkv, w_u, w_f, w_ft, jnp.pad(b_f, (0, LANE - N_HEADS)).reshape(1, LANE),
      b_f.reshape(N_HEADS, 1))


CUM_BLK = 512


def _cumsum_kernel(lfr_ref, fr_ref):
    s_len = lfr_ref.shape[2]
    r = lax.broadcasted_iota(jnp.int32, (CUM_BLK, CUM_BLK), 0)
    c = lax.broadcasted_iota(jnp.int32, (CUM_BLK, CUM_BLK), 1)
    tri_u = (r <= c).astype(F32)
    carry = jnp.zeros((N_HEADS, 1), F32)
    for blk in range(s_len // CUM_BLK):
        cols = slice(blk * CUM_BLK, (blk + 1) * CUM_BLK)
        fr = jnp.dot(lfr_ref[0, :, cols], tri_u, precision=lax.Precision.HIGHEST,
                     preferred_element_type=F32) + carry
        fr_ref[0, :, cols] = fr * LOG2E
        carry = fr[:, CUM_BLK - 1:CUM_BLK]


def _cumsum(lfr):
    nb, _, s_len = lfr.shape
    rspec = pl.BlockSpec((1, N_HEADS, s_len), lambda b: (b, 0, 0))
    return pl.pallas_call(
        _cumsum_kernel, grid=(nb,), in_specs=[rspec], out_specs=rspec,
        out_shape=jax.ShapeDtypeStruct(lfr.shape, F32),
        compiler_params=_cparams(("arbitrary",)), name="cumsum",
    )(lfr)


ATT_BLK = 512


def _attn_kernel(q_ref, k_ref, v_ref, fr_ref, *refs):
    n_w = (len(refs) - 2) // 2
    w_refs, o_ref, wb_refs, s_scr = refs[:n_w], refs[n_w], refs[n_w + 1:2 * n_w + 1], refs[2 * n_w + 1]
    for w_ref, wb_ref in zip(w_refs, wb_refs):
        wb_ref[...] = w_ref[...].astype(BF16)
    h = pl.program_id(1)
    qi = pl.program_id(2)
    t = ATT_BLK
    q = q_ref[0]

    def scores(j):
        start = pl.multiple_of(j * t, t)
        fk = fr_ref[0, pl.ds(h, 1), pl.ds(start, t)]
        return _nt_dot(q, k_ref[0, pl.ds(start, t), :]) - fk

    def values(j):
        return v_ref[0, pl.ds(pl.multiple_of(j * t, t), t), :]

    def stage(j, carry, slot):
        s_scr[1 - slot] = scores(j + 1)
        return _softmax_step(carry, s_scr[slot], values(j))

    def pair(jj, carry):
        return stage(2 * jj + 1, stage(2 * jj, carry, 0), 1)

    def diagonal(carry, slot):
        row = lax.broadcasted_iota(jnp.int32, (t, t), 0)
        col = lax.broadcasted_iota(jnp.int32, (t, t), 1)
        return _softmax_step(carry, jnp.where(col <= row, s_scr[slot], -jnp.inf), values(qi))

    s_scr[0] = scores(0)
    init = (jnp.full((t, 1), -jnp.inf, F32), jnp.zeros((t, 1), F32), jnp.zeros((t, HEAD_DIM), F32))
    carry = lax.fori_loop(0, qi // 2, pair, init)
    m, l, acc = lax.cond(qi % 2 == 1,
                         lambda c: diagonal(stage(qi - 1, c, 0), 1),
                         lambda c: diagonal(c, 0), carry)
    o_ref[0] = (acc / l).astype(BF16)


def _attn(q, k, v, fr, weights):
    nb, s_len, _ = q.shape
    t = ATT_BLK
    n_q = s_len // t
    n_steps = nb * N_HEADS * n_q
    qspec = pl.BlockSpec((1, t, HEAD_DIM), lambda b, h, i: (b, i, h))
    kvspec = pl.BlockSpec((1, s_len, HEAD_DIM), lambda b, h, i: (b, 0, h))
    wspecs = [pl.BlockSpec((w.shape[0] // n_steps, w.shape[1]), lambda b, h, i: ((b * N_HEADS + h) * n_q + i, 0))
              for w in weights]
    assert all(w.shape[0] % (16 * n_steps) == 0 for w in weights)
    return pl.pallas_call(
        _attn_kernel,
        grid=(nb, N_HEADS, n_q),
        in_specs=[qspec, kvspec, kvspec,
                  pl.BlockSpec((1, N_HEADS, s_len), lambda b, h, i: (b, 0, 0))] + wspecs,
        out_specs=[qspec] + wspecs,
        out_shape=[jax.ShapeDtypeStruct((nb, s_len, ATT_WIDTH), BF16)]
                  + [jax.ShapeDtypeStruct(w.shape, BF16) for w in weights],
        scratch_shapes=[pltpu.VMEM((2, t, t), F32)],
        compiler_params=_cparams(("arbitrary", "arbitrary", "arbitrary")),
        name="attn",
    )(q, k, v, fr, *weights)


PAGE_LANES = PAGE * N_HEADS


def _decay_kernel(lf_ref, wt_ref):
    x = lf_ref[...]
    n = x.shape[1]
    lane = lax.broadcasted_iota(jnp.int32, x.shape, 1)
    suffix, total = x, x
    k = N_HEADS
    while k < n:
        suffix = suffix + jnp.where(lane < n - k, pltpu.roll(suffix, n - k, axis=1), 0.0)
        total = total + pltpu.roll(total, k, axis=1)
        k *= 2
    wt_ref[:, 0, :] = suffix - x
    wt_ref[:, 1, :] = total


def _decay(lf_flat):
    n_pool, n = lf_flat.shape
    rb = 256
    return pl.pallas_call(
        _decay_kernel,
        grid=(n_pool // rb,),
        in_specs=[pl.BlockSpec((rb, n), lambda i: (i, 0))],
        out_specs=pl.BlockSpec((rb, 2, n), lambda i: (i, 0, 0)),
        out_shape=jax.ShapeDtypeStruct((n_pool, 2, n), F32),
        compiler_params=_cparams(("arbitrary",)),
        name="decay",
    )(lf_flat)


def _attn_s_init(q, kn_ref, vn_ref, lfn_ref, bias_scr, m_scr, l_scr, acc_scr, carry_scr):
    rows = q.shape[0]
    f = jnp.broadcast_to(lfn_ref[0], (N_HEADS, LANE))
    lane8 = lax.broadcasted_iota(jnp.int32, f.shape, 1)
    k = N_HEADS
    while k < rows:
        f = f + jnp.where(lane8 >= k, pltpu.roll(f, k, axis=1), 0.0)
        k *= 2
    f_row = f[0:1, :] * LOG2E
    r = lax.broadcasted_iota(jnp.int32, (rows, LANE), 0)
    c = lax.broadcasted_iota(jnp.int32, (rows, LANE), 1)
    rr = lax.broadcasted_iota(jnp.int32, (rows, PAGE_LANES), 0)
    cc = lax.broadcasted_iota(jnp.int32, (rows, PAGE_LANES), 1)
    same_head = jnp.bitwise_and(rr, N_HEADS - 1) == jnp.bitwise_and(cc, N_HEADS - 1)
    bias_scr[...] = jnp.where(same_head, 0.0, -jnp.inf)
    pad = jnp.zeros((LANE - rows, HEAD_DIM), BF16)
    kn = jnp.concatenate([kn_ref[0].astype(BF16), pad], axis=0)
    vn = jnp.concatenate([vn_ref[0].astype(BF16), pad], axis=0)
    tok_r = jnp.right_shift(r, 3)
    tok_c = jnp.right_shift(c, 3)
    head_ok = jnp.bitwise_and(r, N_HEADS - 1) == jnp.bitwise_and(c, N_HEADS - 1)
    s = _nt_dot(q, kn) - f_row
    s = jnp.where(head_ok, jnp.where(tok_c <= tok_r, s, -jnp.inf), -jnp.inf)
    init = (jnp.full((rows, 1), -jnp.inf, F32), jnp.zeros((rows, 1), F32), jnp.zeros((rows, HEAD_DIM), F32))
    m_scr[...], l_scr[...], acc_scr[...] = _softmax_step(init, s, vn)
    carry_scr[...] = jnp.zeros_like(carry_scr)


def _attn_s_scores(q, kp_refs, wt_refs, bias_scr, carry_scr):
    carry = carry_scr[...]
    bias = bias_scr[...]
    scores = []
    for kp_ref, wt_ref in zip(kp_refs, wt_refs):
        wt = wt_ref[0]
        dec = (wt[0:1, :] + carry) * LOG2E
        carry = carry + wt[1:2, :]
        k = kp_ref[...].reshape(PAGE_LANES, HEAD_DIM).astype(BF16)
        scores.append(_nt_dot(q, k) + bias + dec)
    carry_scr[...] = carry
    return jnp.concatenate(scores, axis=1)


def _attn_s_values(vp_refs):
    return jnp.concatenate([vp_ref[...].reshape(PAGE_LANES, HEAD_DIM).astype(BF16) for vp_ref in vp_refs], axis=0)


N_SEG = 8
S5_TB = 64
S5_PITCH = 520


def _cmul(ar, ai, br, bi):
    return ar * br - ai * bi, ar * bi + ai * br


def _s5_tail(h, u, wc, dsk, wg, bg):
    y = _dot_row_halves(h.astype(BF16), wc) + dsk * u
    z = jax.nn.gelu(y)
    zz = _dot_row_halves(z.astype(BF16), wg) + bg
    return zz[:, :CH_LANES] * jax.nn.sigmoid(zz[:, CH_LANES:])


def _s5_kernel(u_ref, wb_ref, wc_ref, wg_ref, bg_ref, dsk_ref, lam_ref, y_ref, hfin_ref, ubuf, ybuf, bu_scr):
    s_len = u_ref.shape[1]
    seg = s_len // N_SEG
    n_tb = seg // S5_TB
    assert seg <= S5_PITCH and seg % S5_TB == 0
    for i in range(N_SEG):
        ubuf[i * S5_PITCH:i * S5_PITCH + seg, :] = u_ref[0, i * seg:(i + 1) * seg, :]
    wb = wb_ref[0].astype(BF16)
    lam = lam_ref[0]
    l_re = jnp.broadcast_to(lam[:, :ST_LANES], (N_SEG, ST_LANES))
    l_im = jnp.broadcast_to(lam[:, ST_LANES:], (N_SEG, ST_LANES))

    def load_u(tb):
        t0 = tb * S5_TB
        return jnp.concatenate([ubuf[pl.ds(t0 + j, N_SEG, stride=S5_PITCH), :] for j in range(S5_TB)], axis=0)

    def project(tb, slot):
        u = load_u(jnp.minimum(tb, n_tb - 1))
        bu_scr[slot] = jnp.dot(u.astype(BF16), wb, preferred_element_type=F32)

    def scan_block(slot, h_re, h_im, keep):
        hs = []
        for j in range(S5_TB):
            n_re, n_im = _cmul(l_re, l_im, h_re, h_im)
            h_re = n_re + bu_scr[slot, j * N_SEG:(j + 1) * N_SEG, :ST_LANES]
            h_im = n_im + bu_scr[slot, j * N_SEG:(j + 1) * N_SEG, ST_LANES:]
            if keep:
                hs.append(jnp.concatenate([h_re, h_im], axis=1))
        return h_re, h_im, hs

    def two_blocks(stage):
        assert n_tb % 2 == 0
        return lambda jj, carry: stage(2 * jj + 1, 1, stage(2 * jj, 0, carry))

    def stage1(tb, slot, carry):
        project(tb + 1, 1 - slot)
        h_re, h_im, _ = scan_block(slot, *carry, keep=False)
        return h_re, h_im

    zero = jnp.zeros((N_SEG, ST_LANES), F32)
    project(0, 0)
    f_re, f_im = lax.fori_loop(0, n_tb // 2, two_blocks(stage1), (zero, zero))
    p_re, p_im = lam[:, :ST_LANES], lam[:, ST_LANES:]
    n_sq = seg.bit_length() - 1
    assert 1 << n_sq == seg
    for _ in range(n_sq):
        p_re, p_im = _cmul(p_re, p_im, p_re, p_im)
    c_re, c_im = [jnp.zeros((1, ST_LANES), F32)], [jnp.zeros((1, ST_LANES), F32)]
    for i in range(1, N_SEG):
        d_re, d_im = _cmul(p_re, p_im, c_re[-1], c_im[-1])
        c_re.append(f_re[i - 1:i, :] + d_re)
        c_im.append(f_im[i - 1:i, :] + d_im)
    wc, wg, bg, dsk = wc_ref[0].astype(BF16), wg_ref[0].astype(BF16), bg_ref[0], dsk_ref[0]

    def stage2(tb, slot, carry):
        project(tb + 1, 1 - slot)
        h_re, h_im, hs = scan_block(slot, *carry, keep=True)
        y = _s5_tail(jnp.concatenate(hs, axis=0), load_u(tb), wc, dsk, wg, bg)
        t0 = tb * S5_TB
        for j in range(S5_TB):
            ybuf[pl.ds(t0 + j, N_SEG, stride=S5_PITCH), :] = y[j * N_SEG:(j + 1) * N_SEG, :]
        return h_re, h_im

    project(0, 0)
    h_re, h_im = lax.fori_loop(0, n_tb // 2, two_blocks(stage2),
                               (jnp.concatenate(c_re, axis=0), jnp.concatenate(c_im, axis=0)))
    hfin_ref[0, 0] = jnp.concatenate([h_re[N_SEG - 1:, :], h_im[N_SEG - 1:, :]], axis=1)
    for i in range(N_SEG):
        y_ref[0, i * seg:(i + 1) * seg, :] = ybuf[i * S5_PITCH:i * S5_PITCH + seg, :].astype(BF16)


def _s5(u, wb, wc, wg, bg, dsk, lam):
    nb, s_len, _ = u.shape
    blk = lambda shape: pl.BlockSpec((1,) + shape, lambda b, g: (g, 0, 0))
    y, hfin = pl.pallas_call(
        _s5_kernel,
        grid=(nb, N_GBLK),
        in_specs=[pl.BlockSpec((1, s_len, CH_LANES), lambda b, g: (b, 0, g)),
                  blk((CH_LANES, 2 * ST_LANES)), blk((2 * ST_LANES, CH_LANES)),
                  blk((CH_LANES, 2 * CH_LANES)), blk((1, 2 * CH_LANES)), blk((1, CH_LANES)),
                  blk((1, 2 * ST_LANES))],
        out_specs=[pl.BlockSpec((1, s_len, CH_LANES), lambda b, g: (b, 0, g)),
                   pl.BlockSpec((1, 1, 1, 2 * ST_LANES), lambda b, g: (b, g, 0, 0))],
        out_shape=(jax.ShapeDtypeStruct((nb, s_len, SSM_WIDTH), BF16),
                   jax.ShapeDtypeStruct((nb, N_GBLK, 1, 2 * ST_LANES), F32)),
        scratch_shapes=[pltpu.VMEM((N_SEG * S5_PITCH, CH_LANES), F32),
                        pltpu.VMEM((N_SEG * S5_PITCH, CH_LANES), F32),
                        pltpu.VMEM((2, S5_TB * N_SEG, 2 * ST_LANES), F32)],
        compiler_params=_cparams(("arbitrary", "arbitrary")),
        name="s5",
    )(u, wb, wc, wg, bg, dsk, lam)
    hfin = hfin.reshape(nb, N_GBLK, 2, GROUPS_PER_BLOCK, SSM_STATE)
    return y, hfin[:, :, 0].reshape(nb, N_GROUPS, SSM_STATE), hfin[:, :, 1].reshape(nb, N_GROUPS, SSM_STATE)


def _s5s_kernel(u_ref, h0_ref, wb_ref, wc_ref, wg_ref, bg_ref, dsk_ref, lam_ref, y_ref, hfin_ref, *, nb):
    hi = lax.Precision.HIGHEST
    u = u_ref[...]
    n_t = u.shape[0] // nb
    bu = jnp.dot(u, wb_ref[0], precision=hi, preferred_element_type=F32)
    lam = lam_ref[0]
    l_re = jnp.broadcast_to(lam[:, :ST_LANES], (nb, ST_LANES))
    l_im = jnp.broadcast_to(lam[:, ST_LANES:], (nb, ST_LANES))
    h0 = h0_ref[0]
    h_re, h_im = h0[:, :ST_LANES], h0[:, ST_LANES:]
    hs = []
    for t in range(n_t):
        n_re, n_im = _cmul(l_re, l_im, h_re, h_im)
        h_re = n_re + bu[t * nb:(t + 1) * nb, :ST_LANES]
        h_im = n_im + bu[t * nb:(t + 1) * nb, ST_LANES:]
        hs.append(jnp.concatenate([h_re, h_im], axis=1))
    hfin_ref[0] = hs[-1]
    h = jnp.concatenate(hs, axis=0)
    y = jnp.dot(h, wc_ref[0], precision=hi, preferred_element_type=F32) + dsk_ref[0] * u
    z = jax.nn.gelu(y)
    zz = jnp.dot(z, wg_ref[0], precision=hi, preferred_element_type=F32) + bg_ref[0]
    y_ref[...] = (zz[:, :CH_LANES] * jax.nn.sigmoid(zz[:, CH_LANES:])).astype(BF16)


def _s5s(u_tm, h0, wb, wc, wg, bg, dsk, lam, nb):
    rows = u_tm.shape[0]
    blk = lambda shape: pl.BlockSpec((1,) + shape, lambda g: (g, 0, 0))
    return pl.pallas_call(
        functools.partial(_s5s_kernel, nb=nb),
        grid=(N_GBLK,),
        in_specs=[pl.BlockSpec((rows, CH_LANES), lambda g: (0, g)),
                  blk((nb, 2 * ST_LANES)),
                  blk((CH_LANES, 2 * ST_LANES)), blk((2 * ST_LANES, CH_LANES)),
                  blk((CH_LANES, 2 * CH_LANES)), blk((1, 2 * CH_LANES)), blk((1, CH_LANES)),
                  blk((1, 2 * ST_LANES))],
        out_specs=[pl.BlockSpec((rows, CH_LANES), lambda g: (0, g)), blk((nb, 2 * ST_LANES))],
        out_shape=(jax.ShapeDtypeStruct((rows, SSM_WIDTH), BF16),
                   jax.ShapeDtypeStruct((N_GBLK, nb, 2 * ST_LANES), F32)),
        compiler_params=_cparams(("arbitrary",)),
        name="s5s",
    )(u_tm, h0, wb, wc, wg, bg, dsk, lam)


def _outproj_kernel(att_ref, ssm_ref, x_ref, g1_ref, sc2_ref, sh2_ref, lng_ref, lnb_ref, wo_ref,
                    x1_ref, h2_ref):
    tm = x_ref.shape[1]
    n_sub = 2 if tm >= 512 else 1
    sub = tm // n_sub

    def mod_rows(ref, rows):
        return ref[0] if ref.shape[1] == 1 else ref[0, rows, :]

    for c in range(n_sub):
        rows = slice(c * sub, (c + 1) * sub)
        mix = (jnp.dot(att_ref[0, rows, :], wo_ref[:ATT_WIDTH, :], preferred_element_type=F32)
               + jnp.dot(ssm_ref[0, rows, :], wo_ref[ATT_WIDTH:, :], preferred_element_type=F32))
        x1 = (_ln(ALPHA * x_ref[0, rows, :] + (1.0 + mod_rows(g1_ref, rows)) * mix) * lng_ref[...]
              + lnb_ref[...])
        x1_ref[0, rows, :] = x1
        h2_ref[0, rows, :] = (_ln(x1) * (1.0 + mod_rows(sc2_ref, rows)) + mod_rows(sh2_ref, rows)).astype(BF16)


def _outproj(att, ssm, x, mod, ln_g, ln_b, w_o, tm):
    nb, l, _ = x.shape
    row_map = lambda b, i: (b, i, 0)
    const2 = lambda b, i: (0, 0)
    half = pl.BlockSpec((1, tm, ATT_WIDTH), row_map)
    full = pl.BlockSpec((1, tm, D_MODEL), row_map)
    vec = pl.BlockSpec((1, D_MODEL), const2)
    return pl.pallas_call(
        _outproj_kernel,
        grid=(nb, l // tm),
        in_specs=[half, half, full,
                  _mod_spec(mod, GATE1, nb), _mod_spec(mod, SCALE2, nb), _mod_spec(mod, SHIFT2, nb), vec, vec,
                  pl.BlockSpec((D_MODEL, D_MODEL), const2)],
        out_specs=[full, full],
        out_shape=(jax.ShapeDtypeStruct((nb, l, D_MODEL), F32),
                   jax.ShapeDtypeStruct((nb, l, D_MODEL), BF16)),
        compiler_params=_cparams(("arbitrary", "arbitrary")),
        name="outproj",
    )(att, ssm, x, mod, mod, mod, ln_g.reshape(1, D_MODEL), ln_b.reshape(1, D_MODEL), w_o)


FFN_TF = 1024
FUSED_TF = 512


def _ffn_zero(f, acc_scr):
    @pl.when(f == 0)
    def _():
        acc_scr[...] = jnp.zeros_like(acc_scr)


def _ffn_accumulate(h2_ref, wu_ref, wd_ref, acc_scr):
    up = jnp.dot(h2_ref[0], wu_ref[...], preferred_element_type=F32)
    act = jnp.square(jnp.maximum(up, 0.0)).astype(BF16)
    acc_scr[...] += jnp.dot(act, wd_ref[...], preferred_element_type=F32)


def _ffn_finish(f, x1_ref, g2_ref, lng_ref, lnb_ref, y_ref, acc_scr):
    @pl.when(f == pl.num_programs(2) - 1)
    def _():
        y_ref[0] = _ln(ALPHA * x1_ref[0] + (1.0 + g2_ref[0]) * acc_scr[...]) * lng_ref[...] + lnb_ref[...]


def _ffn_kernel(h2_ref, x1_ref, g2_ref, lng_ref, lnb_ref, wu_ref, wd_ref, y_ref, acc_scr):
    f = pl.program_id(2)
    _ffn_zero(f, acc_scr)
    _ffn_accumulate(h2_ref, wu_ref, wd_ref, acc_scr)
    _ffn_finish(f, x1_ref, g2_ref, lng_ref, lnb_ref, y_ref, acc_scr)


def _ffn_specs(mod, nb, tm, tf):
    wrap = lambda fn: (lambda b, i, f, *_: fn(b, i, f))
    row_map = wrap(lambda b, i, f: (b, i, 0))
    full = pl.BlockSpec((1, tm, D_MODEL), row_map)
    vec = pl.BlockSpec((1, D_MODEL), wrap(lambda b, i, f: (0, 0)))
    in_specs = [full, full, _mod_spec(mod, GATE2, nb), vec, vec,
                pl.BlockSpec((D_MODEL, tf), wrap(lambda b, i, f: (0, f))),
                pl.BlockSpec((tf, D_MODEL), wrap(lambda b, i, f: (f, 0)))]
    return in_specs, full


def _ffn(h2, x1, mod, ln_g, ln_b, w_up, w_down, tm):
    nb, l, _ = x1.shape
    in_specs, out_spec = _ffn_specs(mod, nb, tm, FFN_TF)
    return pl.pallas_call(
        _ffn_kernel,
        grid=(nb, l // tm, D_FF // FFN_TF),
        in_specs=in_specs,
        out_specs=out_spec,
        out_shape=jax.ShapeDtypeStruct((nb, l, D_MODEL), F32),
        scratch_shapes=[pltpu.VMEM((tm, D_MODEL), F32)],
        compiler_params=_cparams(("arbitrary", "arbitrary", "arbitrary")),
        name="ffn",
    )(h2, x1, mod, ln_g.reshape(1, D_MODEL), ln_b.reshape(1, D_MODEL), w_up, w_down)


def _ffn_attn_kernel(pt_ref, h2_ref, x1_ref, g2_ref, lng_ref, lnb_ref, wu_ref, wd_ref,
                     q_ref, kn_ref, vn_ref, lfn_ref, *refs, n_pg, steps_per_seq):
    kp_refs, vp_refs, wt_refs = refs[:n_pg], refs[n_pg:2 * n_pg], refs[2 * n_pg:3 * n_pg]
    y_ref, o_ref, acc_scr, bias_scr, m_scr, l_scr, a_scr, carry_scr = refs[3 * n_pg:]
    f = pl.program_id(2)
    t = (pl.program_id(0) * pl.num_programs(1) + pl.program_id(1)) * pl.num_programs(2) + f
    si = lax.rem(t, steps_per_seq)
    q = q_ref[0]

    @pl.when(si == 0)
    def _():
        _attn_s_init(q, kn_ref, vn_ref, lfn_ref, bias_scr, m_scr, l_scr, a_scr, carry_scr)

    _ffn_zero(f, acc_scr)
    s = _attn_s_scores(q, kp_refs, wt_refs, bias_scr, carry_scr)
    up = jnp.dot(h2_ref[0], wu_ref[...], preferred_element_type=F32)
    m_new, alpha, p, l_new = _softmax_weights(m_scr[...], l_scr[...], s)
    act = jnp.square(jnp.maximum(up, 0.0)).astype(BF16)
    acc_scr[...] += jnp.dot(act, wd_ref[...], preferred_element_type=F32)
    a_new = alpha * a_scr[...] + _dot_depth_halves(p, _attn_s_values(vp_refs))
    m_scr[...], l_scr[...], a_scr[...] = m_new, l_new, a_new

    @pl.when(si == steps_per_seq - 1)
    def _():
        o_ref[0] = a_new / l_new

    _ffn_finish(f, x1_ref, g2_ref, lng_ref, lnb_ref, y_ref, acc_scr)


def _ffn_attn(h2, x1, mod, ln_g, ln_b, w_up, w_down, tm,
              page_table, q, k_new, v_new, lf_new, cache_k, cache_v, decay_wt):
    nb, l, _ = x1.shape
    nbs, rows, _ = q.shape
    n_pages = page_table.shape[1]
    n_i, n_f = l // tm, D_FF // FUSED_TF
    n_steps = nb * n_i * n_f
    n_pg = nbs * n_pages // n_steps
    steps_per_seq = n_pages // n_pg
    assert n_pg * n_steps == nbs * n_pages and steps_per_seq * n_pg == n_pages

    def step(b, i, f):
        t = (b * n_i + i) * n_f + f
        return lax.div(t, steps_per_seq), lax.rem(t, steps_per_seq)

    seq = lambda b, i, f, pt: (step(b, i, f)[0], 0, 0)

    def page_spec(block, n_lead, pg):
        def idx(b, i, f, pt):
            sq, si = step(b, i, f)
            return (0,) * n_lead + (pt[sq, n_pages - 1 - (si * n_pg + pg)],) + (0,) * (len(block) - n_lead - 1)
        return pl.BlockSpec(block, idx)

    ffn_specs, y_spec = _ffn_specs(mod, nb, tm, FUSED_TF)
    kv_block = (None, None, PAGE, N_HEADS, HEAD_DIM)
    tok_spec = pl.BlockSpec((1, rows, HEAD_DIM), seq)
    grid_spec = pltpu.PrefetchScalarGridSpec(
        num_scalar_prefetch=1,
        grid=(nb, n_i, n_f),
        in_specs=ffn_specs + [tok_spec, tok_spec, tok_spec, pl.BlockSpec((1, 1, LANE), seq)]
                 + [page_spec(kv_block, 1, pg) for pg in range(n_pg)]
                 + [page_spec(kv_block, 1, pg) for pg in range(n_pg)]
                 + [page_spec((1, 2, PAGE_LANES), 0, pg) for pg in range(n_pg)],
        out_specs=[y_spec, tok_spec],
        scratch_shapes=[pltpu.VMEM((tm, D_MODEL), F32),
                        pltpu.VMEM((rows, PAGE_LANES), F32),
                        pltpu.VMEM((rows, 1), F32),
                        pltpu.VMEM((rows, 1), F32),
                        pltpu.VMEM((rows, HEAD_DIM), F32),
                        pltpu.VMEM((1, PAGE_LANES), F32)])
    return pl.pallas_call(
        functools.partial(_ffn_attn_kernel, n_pg=n_pg, steps_per_seq=steps_per_seq),
        grid_spec=grid_spec,
        out_shape=(jax.ShapeDtypeStruct((nb, l, D_MODEL), F32),
                   jax.ShapeDtypeStruct((nbs, rows, HEAD_DIM), F32)),
        compiler_params=_cparams(("arbitrary", "arbitrary", "arbitrary")),
        name="ffn_attn",
    )(page_table, h2, x1, mod, ln_g.reshape(1, D_MODEL), ln_b.reshape(1, D_MODEL), w_up, w_down,
      q, k_new, v_new, lf_new, *([cache_k] * n_pg), *([cache_v] * n_pg), *([decay_wt] * n_pg))


def kernel(x_prompt, x_sample, c_prompt, c_sample, cache_k, cache_v, cache_logf, state_ssm_re,
           state_ssm_im, page_table, w_ada, b_ada, w_in, b_f, w_o, a_re, a_im, log_dt, b_re, b_im,
           c_re, c_im, d_skip, w_glu, b_glu, ln1_g, ln1_b, w_up, w_down, ln2_g, ln2_b):
    assert w_ada.shape[0] == DEPTH == 1
    nbp, s_len, _ = x_prompt.shape
    nbs, n_q, _ = x_sample.shape
    n_seq = nbp + nbs

    c_all = jnp.concatenate([c_prompt, c_sample, jnp.zeros((16 - n_seq, D_MODEL), F32)], axis=0)
    mod = _ada(c_all, w_ada[0], b_ada[0]).reshape(16, N_MOD, D_MODEL)
    mod_p = _mod_pack(mod[:nbp], 1)
    mod_s = _mod_pack(mod[nbp:n_seq], n_q)

    a = ATT_WIDTH
    w_in_t = w_in[0].T
    w_qkv = w_in_t[:3 * a].astype(BF16)
    w_u = w_in_t[3 * a + N_HEADS:].astype(BF16)
    w_ft = w_in_t[3 * a:3 * a + N_HEADS].astype(BF16)
    w_f = jnp.pad(w_ft, ((0, LANE - N_HEADS), (0, 0)))

    l_re, l_im, bb_re, bb_im = _s5prep(a_re[0], a_im[0], log_dt[0], b_re[0], b_im[0])
    wb, wc, wg, bg, dsk, lam = _s5_weights(l_re, l_im, bb_re, bb_im, c_re[0], c_im[0], d_skip[0],
                                           w_glu[0], b_glu[0])

    rows = nbs * n_q
    xs = x_sample.reshape(1, rows, D_MODEL)
    qs, ks, _, vs, _, us, lfcs, _ = _inproj(xs, mod_s, w_qkv, w_u, w_f, w_ft, b_f[0], tm=rows)
    n_pool = cache_k.shape[1]
    decay_wt = _decay(cache_logf[0].reshape(n_pool, PAGE_LANES))
    per_head = lambda t: t.reshape(nbs, n_q * N_HEADS, HEAD_DIM)
    lf_new = jnp.pad(lfcs.reshape(nbs, 1, n_q * N_HEADS), ((0, 0), (0, 0), (0, LANE - n_q * N_HEADS)))

    q, k, k_b, v, v_b, u, lfc, lfr = _inproj(x_prompt, mod_p, w_qkv, w_u, w_f, w_ft, b_f[0], tm=512)
    att, w_up_b, w_down_b, w_o_b = _attn(q, k_b, v_b, _cumsum(lfr), (w_up[0], w_down[0], w_o[0]))
    ssm, hp_re, hp_im = _s5(u, wb, wc, wg, bg, dsk, lam)
    x1, h2 = _outproj(att, ssm, x_prompt, mod_p, ln1_g[0], ln1_b[0], w_o_b, tm=512)
    y_p, att_s = _ffn_attn(h2, x1, mod_p, ln2_g[0], ln2_b[0], w_up_b, w_down_b, 512,
                           page_table, per_head(qs), per_head(ks), per_head(vs), lf_new, cache_k, cache_v,
                           decay_wt)

    u_tm = us.reshape(nbs, n_q, SSM_WIDTH).transpose(1, 0, 2).reshape(rows, SSM_WIDTH)
    h0 = jnp.concatenate([state_ssm_re[0].reshape(nbs, N_GBLK, ST_LANES),
                          state_ssm_im[0].reshape(nbs, N_GBLK, ST_LANES)], axis=2).transpose(1, 0, 2)
    ssm_tm, hs_fin = _s5s(u_tm, h0, wb, wc, wg, bg, dsk, lam, nbs)
    ssm_s = ssm_tm.reshape(n_q, nbs, SSM_WIDTH).transpose(1, 0, 2).reshape(1, rows, SSM_WIDTH)
    hs_fin = hs_fin.transpose(1, 0, 2).reshape(nbs, N_GBLK, 2, GROUPS_PER_BLOCK, SSM_STATE)
    hs_re = hs_fin[:, :, 0].reshape(nbs, N_GROUPS, SSM_STATE)
    hs_im = hs_fin[:, :, 1].reshape(nbs, N_GROUPS, SSM_STATE)
    x1s, h2s = _outproj(att_s.reshape(1, rows, a).astype(BF16), ssm_s, xs, mod_s,
                        ln1_g[0], ln1_b[0], w_o_b, tm=rows)
    y_s = _ffn(h2s, x1s, mod_s, ln2_g[0], ln2_b[0], w_up_b, w_down_b, tm=rows)

    hd = (N_HEADS, HEAD_DIM)
    return (y_p, y_s.reshape(nbs, n_q, D_MODEL),
            k.reshape(1, nbp, s_len, *hd), v.reshape(1, nbp, s_len, *hd), lfc[None],
            hp_re[None], hp_im[None],
            ks.reshape(1, nbs, n_q, *hd), vs.reshape(1, nbs, n_q, *hd), lfcs.reshape(1, nbs, n_q, N_HEADS),
            hs_re[None], hs_im[None])
```

```python
import functools
import math

import jax
import jax.numpy as jnp
from jax import lax
from jax.experimental import pallas as pl
from jax.experimental.pallas import tpu as pltpu

F32 = jnp.float32
BF16 = jnp.bfloat16

LANE = 128
D_MODEL = 2048
ATT_WIDTH = 1024
SSM_WIDTH = 1024
HEAD_DIM = 128
N_HEADS = 8
SSM_GROUP = 16
N_GROUPS = 64
SSM_STATE = 64
D_FF = 8192
N_MOD = 6
PAGE = 128
DEPTH = 1
ALPHA = (2 * DEPTH) ** 0.25
LN_EPS = 1e-5
LOG2E = math.log2(math.e)
QSCALE = HEAD_DIM ** -0.5 * LOG2E

GROUPS_PER_BLOCK = 8
N_GBLK = N_GROUPS // GROUPS_PER_BLOCK
ST_LANES = GROUPS_PER_BLOCK * SSM_STATE
CH_LANES = GROUPS_PER_BLOCK * SSM_GROUP

VMEM_LIMIT = 56 * 1024 * 1024
ROW_TILE = 512
BF16_SUBLANES = 16
ADA_ROWS = 16


def _cparams(sem):
    return pltpu.CompilerParams(dimension_semantics=sem, vmem_limit_bytes=VMEM_LIMIT)


def _ln(x):
    mu = jnp.mean(x, axis=-1, keepdims=True)
    xc = x - mu
    var = jnp.mean(xc * xc, axis=-1, keepdims=True)
    return xc * lax.rsqrt(var + LN_EPS)


def _log_sigmoid(x):
    return jnp.minimum(x, 0.0) - jnp.log1p(jnp.exp(-jnp.abs(x)))


def _nt_dot(a, b):
    return lax.dot_general(a, b, (((1,), (1,)), ((), ())), preferred_element_type=F32)


def _softmax_weights(m, l, s):
    m_new = jnp.maximum(m, jnp.max(s, axis=1, keepdims=True))
    alpha = jnp.exp2(m - m_new)
    p = jnp.exp2(s - m_new)
    return m_new, alpha, p.astype(BF16), alpha * l + jnp.sum(p, axis=1, keepdims=True)


def _dot_row_halves(a, b):
    half = a.shape[0] // 2
    return jnp.concatenate([jnp.dot(a[:half], b, preferred_element_type=F32),
                            jnp.dot(a[half:], b, preferred_element_type=F32)], axis=0)


def _dot_depth_halves(a, b):
    half = a.shape[1] // 2
    return (jnp.dot(a[:, :half], b[:half], preferred_element_type=F32)
            + jnp.dot(a[:, half:], b[half:], preferred_element_type=F32))


def _softmax_step(carry, s, v):
    m, l, acc = carry
    m_new, alpha, p, l_new = _softmax_weights(m, l, s)
    return m_new, l_new, alpha * acc + jnp.dot(p, v, preferred_element_type=F32)


def _ada_kernel(c_ref, w_ref, b_ref, o_ref):
    c = c_ref[...]
    s = c * jax.nn.sigmoid(c)
    o_ref[...] = jnp.dot(s, w_ref[...], preferred_element_type=F32) + b_ref[...]


def _ada(c_all, w_ada, b_ada):
    rows = c_all.shape[0]
    n = w_ada.shape[1]
    tn = 1024
    return pl.pallas_call(
        _ada_kernel,
        grid=(n // tn,),
        in_specs=[pl.BlockSpec((rows, D_MODEL), lambda j: (0, 0)),
                  pl.BlockSpec((D_MODEL, tn), lambda j: (0, j)),
                  pl.BlockSpec((1, tn), lambda j: (0, j))],
        out_specs=pl.BlockSpec((rows, tn), lambda j: (0, j)),
        out_shape=jax.ShapeDtypeStruct((rows, n), F32),
        compiler_params=_cparams(("arbitrary",)),
        name="ada",
    )(c_all, w_ada, b_ada.reshape(1, n))


SHIFT1, SCALE1, GATE1, SHIFT2, SCALE2, GATE2 = range(N_MOD)


def _mod_pack(mod_rows, repeat):
    n_seq = mod_rows.shape[0]
    m = jnp.swapaxes(mod_rows, 0, 1)
    if repeat == 1:
        return m.reshape(N_MOD * n_seq, 1, D_MODEL)
    return jnp.repeat(m, repeat, axis=1)


def _mod_spec(mod, chunk, nb):
    r = mod.shape[1]
    return pl.BlockSpec((1, r, D_MODEL), lambda b, i, *_: (chunk * nb + b, 0 if r == 1 else i, 0))


def _s5prep_kernel(are_ref, aim_ref, ldt_ref, arx_ref, aix_ref, ldx_ref, bre_ref, bim_ref,
                   lre_ref, lim_ref, bbre_ref, bbim_ref):
    def lam(a_re, a_im, log_dt):
        dt = jnp.exp(log_dt)
        mag = jnp.exp(a_re * dt)
        return mag * jnp.cos(a_im * dt), mag * jnp.sin(a_im * dt)

    l_re, l_im = lam(are_ref[...], aim_ref[...], ldt_ref[...])
    lre_ref[...] = l_re
    lim_ref[...] = l_im
    a_re, a_im = arx_ref[...], aix_ref[...]
    x_re, x_im = lam(a_re, a_im, ldx_ref[...])
    den = a_re * a_re + a_im * a_im
    n_re = x_re - 1.0
    k_re = (n_re * a_re + x_im * a_im) / den
    k_im = (x_im * a_re - n_re * a_im) / den
    b_re, b_im = bre_ref[...], bim_ref[...]
    bbre_ref[...] = k_re * b_re - k_im * b_im
    bbim_ref[...] = k_re * b_im + k_im * b_re


def _s5prep(a_re, a_im, log_dt, b_re, b_im):
    g, p, c = b_re.shape
    ldt = jnp.broadcast_to(log_dt[:, None], (g, p))
    ex = lambda a: jnp.broadcast_to(a[:, :, None], (g, p, c)).reshape(g, p * c)
    small = jax.ShapeDtypeStruct((g, p), F32)
    big = jax.ShapeDtypeStruct((g, p * c), F32)
    l_re, l_im, bb_re, bb_im = pl.pallas_call(
        _s5prep_kernel, out_shape=(small, small, big, big), name="s5prep",
    )(a_re, a_im, ldt, ex(a_re), ex(a_im), ex(ldt), b_re.reshape(g, p * c), b_im.reshape(g, p * c))
    return l_re, l_im, bb_re.reshape(g, p, c), bb_im.reshape(g, p, c)


def _blockdiag(w):
    g, a, b = w.shape
    w = w.reshape(N_GBLK, GROUPS_PER_BLOCK, a, b)
    eye = jnp.eye(GROUPS_PER_BLOCK, dtype=w.dtype)
    return jnp.einsum('xgab,gh->xgahb', w, eye).reshape(N_GBLK, GROUPS_PER_BLOCK * a, GROUPS_PER_BLOCK * b)


def _s5_weights(l_re, l_im, bb_re, bb_im, c_re, c_im, d_skip, w_glu, b_glu):
    wb = jnp.concatenate([_blockdiag(jnp.swapaxes(bb_re, 1, 2)),
                          _blockdiag(jnp.swapaxes(bb_im, 1, 2))], axis=2)
    wc = jnp.concatenate([_blockdiag(jnp.swapaxes(c_re, 1, 2)),
                          _blockdiag(-jnp.swapaxes(c_im, 1, 2))], axis=1)
    wg = jnp.concatenate([_blockdiag(w_glu[:, :, :SSM_GROUP]),
                          _blockdiag(w_glu[:, :, SSM_GROUP:])], axis=2)
    bg = jnp.concatenate([b_glu[:, :SSM_GROUP].reshape(N_GBLK, 1, CH_LANES),
                          b_glu[:, SSM_GROUP:].reshape(N_GBLK, 1, CH_LANES)], axis=2)
    dsk = d_skip.reshape(N_GBLK, 1, CH_LANES)
    lam = jnp.concatenate([l_re.reshape(N_GBLK, 1, ST_LANES), l_im.reshape(N_GBLK, 1, ST_LANES)], axis=2)
    return wb, wc, wg, bg, dsk, lam


def _inproj_kernel(x_ref, sh_ref, sc_ref, w_ref, wu_ref, wf_ref, wft_ref, bf_ref, bfr_ref,
                   q_ref, kf_ref, kb_ref, vf_ref, vb_ref, u_ref, lfc_ref, lfr_ref):
    hb = (_ln(x_ref[0]) * (1.0 + sc_ref[0]) + sh_ref[0]).astype(BF16)
    lfc_ref[0] = _log_sigmoid(_nt_dot(hb, wf_ref[...]) + bf_ref[...])[:, :N_HEADS]
    lfr_ref[0] = _log_sigmoid(_nt_dot(wft_ref[...], hb) + bfr_ref[...])

    def proj(w, group):
        return _nt_dot(hb, w[group * ATT_WIDTH:(group + 1) * ATT_WIDTH, :])

    q_ref[0] = (proj(w_ref, 0) * QSCALE).astype(BF16)
    k = proj(w_ref, 1)
    kf_ref[0] = k
    kb_ref[0] = k.astype(BF16)
    v = proj(w_ref, 2)
    vf_ref[0] = v
    vb_ref[0] = v.astype(BF16)
    u_ref[0] = proj(wu_ref, 0)


def _inproj(x, mod, w_qkv, w_u, w_f, w_ft, b_f, tm):
    nb, l, _ = x.shape
    row_map = lambda b, i: (b, i, 0)
    const2 = lambda b, i: (0, 0)
    wide = lambda dt: jax.ShapeDtypeStruct((nb, l, ATT_WIDTH), dt)
    return pl.pallas_call(
        _inproj_kernel,
        grid=(nb, l // tm),
        in_specs=[pl.BlockSpec((1, tm, D_MODEL), row_map),
                  _mod_spec(mod, SHIFT1, nb),
                  _mod_spec(mod, SCALE1, nb),
                  pl.BlockSpec((3 * ATT_WIDTH, D_MODEL), const2, pipeline_mode=pl.Buffered(1)),
                  pl.BlockSpec((SSM_WIDTH, D_MODEL), const2, pipeline_mode=pl.Buffered(1)),
                  pl.BlockSpec((LANE, D_MODEL), const2),
                  pl.BlockSpec((N_HEADS, D_MODEL), const2),
                  pl.BlockSpec((1, LANE), const2),
                  pl.BlockSpec((N_HEADS, 1), const2)],
        out_specs=[pl.BlockSpec((1, tm, ATT_WIDTH), row_map)] * 6
                  + [pl.BlockSpec((1, tm, N_HEADS), row_map),
                     pl.BlockSpec((1, N_HEADS, tm), lambda b, i: (b, 0, i))],
        out_shape=(wide(BF16), wide(F32), wide(BF16), wide(F32), wide(BF16), wide(F32),
                   jax.ShapeDtypeStruct((nb, l, N_HEADS), F32),
                   jax.ShapeDtypeStruct((nb, N_HEADS, l), F32)),
        compiler_params=_cparams(("arbitrary", "arbitrary")),
        name="inproj",
    )(x, mod, mod, w_qkv, w_u, w_f, w_ft, jnp.pad(b_f, (0, LANE - N_HEADS)).reshape(1, LANE),
      b_f.reshape(N_HEADS, 1))


CUM_BLK = 512


def _cumsum_kernel(lfr_ref, fr_ref):
    s_len = lfr_ref.shape[2]
    r = lax.broadcasted_iota(jnp.int32, (CUM_BLK, CUM_BLK), 0)
    c = lax.broadcasted_iota(jnp.int32, (CUM_BLK, CUM_BLK), 1)
    tri_u = (r <= c).astype(F32)
    carry = jnp.zeros((N_HEADS, 1), F32)
    for blk in range(s_len // CUM_BLK):
        cols = slice(blk * CUM_BLK, (blk + 1) * CUM_BLK)
        fr = jnp.dot(lfr_ref[0, :, cols], tri_u, precision=lax.Precision.HIGHEST,
                     preferred_element_type=F32) + carry
        fr_ref[0, :, cols] = fr * LOG2E
        carry = fr[:, CUM_BLK - 1:CUM_BLK]


def _cumsum(lfr):
    nb, _, s_len = lfr.shape
    rspec = pl.BlockSpec((1, N_HEADS, s_len), lambda b: (b, 0, 0))
    return pl.pallas_call(
        _cumsum_kernel, grid=(nb,), in_specs=[rspec], out_specs=rspec,
        out_shape=jax.ShapeDtypeStruct(lfr.shape, F32),
        compiler_params=_cparams(("arbitrary",)), name="cumsum",
    )(lfr)


ATT_BLK = 1024


def _attn_kernel(q_ref, k_ref, v_ref, fr_ref, *refs):
    n_w = (len(refs) - 2) // 2
    w_refs, o_ref, wb_refs, s_scr = refs[:n_w], refs[n_w], refs[n_w + 1:2 * n_w + 1], refs[2 * n_w + 1]
    for w_ref, wb_ref in zip(w_refs, wb_refs):
        wb_ref[...] = w_ref[...].astype(BF16)
    h = pl.program_id(1)
    qi = pl.program_id(2)
    t = ATT_BLK
    q = q_ref[0]

    def scores(j):
        start = pl.multiple_of(j * t, t)
        fk = fr_ref[0, pl.ds(h, 1), pl.ds(start, t)]
        return _nt_dot(q, k_ref[0, pl.ds(start, t), :]) - fk

    def values(j):
        return v_ref[0, pl.ds(pl.multiple_of(j * t, t), t), :]

    def stage(j, carry, slot):
        s_scr[1 - slot] = scores(j + 1)
        return _softmax_step(carry, s_scr[slot], values(j))

    def pair(jj, carry):
        return stage(2 * jj + 1, stage(2 * jj, carry, 0), 1)

    def diagonal(carry, slot):
        row = lax.broadcasted_iota(jnp.int32, (t, t), 0)
        col = lax.broadcasted_iota(jnp.int32, (t, t), 1)
        return _softmax_step(carry, jnp.where(col <= row, s_scr[slot], -jnp.inf), values(qi))

    s_scr[0] = scores(0)
    init = (jnp.full((t, 1), -jnp.inf, F32), jnp.zeros((t, 1), F32), jnp.zeros((t, HEAD_DIM), F32))
    carry = lax.fori_loop(0, qi // 2, pair, init)
    m, l, acc = lax.cond(qi % 2 == 1,
                         lambda c: diagonal(stage(qi - 1, c, 0), 1),
                         lambda c: diagonal(c, 0), carry)
    o_ref[0] = (acc / l).astype(BF16)


def _attn(q, k, v, fr, weights):
    nb, s_len, _ = q.shape
    t = ATT_BLK
    n_q = s_len // t
    n_steps = nb * N_HEADS * n_q
    qspec = pl.BlockSpec((1, t, HEAD_DIM), lambda b, h, i: (b, i, h))
    kvspec = pl.BlockSpec((1, s_len, HEAD_DIM), lambda b, h, i: (b, 0, h))
    wspecs = [pl.BlockSpec((w.shape[0] // n_steps, w.shape[1]), lambda b, h, i: ((b * N_HEADS + h) * n_q + i, 0))
              for w in weights]
    assert all(w.shape[0] % (BF16_SUBLANES * n_steps) == 0 for w in weights)
    return pl.pallas_call(
        _attn_kernel,
        grid=(nb, N_HEADS, n_q),
        in_specs=[qspec, kvspec, kvspec,
                  pl.BlockSpec((1, N_HEADS, s_len), lambda b, h, i: (b, 0, 0))] + wspecs,
        out_specs=[qspec] + wspecs,
        out_shape=[jax.ShapeDtypeStruct((nb, s_len, ATT_WIDTH), BF16)]
                  + [jax.ShapeDtypeStruct(w.shape, BF16) for w in weights],
        scratch_shapes=[pltpu.VMEM((2, t, t), F32)],
        compiler_params=_cparams(("arbitrary", "arbitrary", "arbitrary")),
        name="attn",
    )(q, k, v, fr, *weights)


PAGE_LANES = PAGE * N_HEADS


def _decay_kernel(lf_ref, wt_ref):
    x = lf_ref[...]
    n = x.shape[1]
    lane = lax.broadcasted_iota(jnp.int32, x.shape, 1)
    suffix, total = x, x
    k = N_HEADS
    while k < n:
        suffix = suffix + jnp.where(lane < n - k, pltpu.roll(suffix, n - k, axis=1), 0.0)
        total = total + pltpu.roll(total, k, axis=1)
        k *= 2
    wt_ref[:, 0, :] = suffix - x
    wt_ref[:, 1, :] = total


def _decay(lf_flat):
    n_pool, n = lf_flat.shape
    rb = 256
    return pl.pallas_call(
        _decay_kernel,
        grid=(n_pool // rb,),
        in_specs=[pl.BlockSpec((rb, n), lambda i: (i, 0))],
        out_specs=pl.BlockSpec((rb, 2, n), lambda i: (i, 0, 0)),
        out_shape=jax.ShapeDtypeStruct((n_pool, 2, n), F32),
        compiler_params=_cparams(("arbitrary",)),
        name="decay",
    )(lf_flat)


def _attn_s_init(q, kn_ref, vn_ref, lfn_ref, bias_scr, m_scr, l_scr, acc_scr, carry_scr):
    rows = q.shape[0]
    f = jnp.broadcast_to(lfn_ref[0], (N_HEADS, LANE))
    lane8 = lax.broadcasted_iota(jnp.int32, f.shape, 1)
    k = N_HEADS
    while k < rows:
        f = f + jnp.where(lane8 >= k, pltpu.roll(f, k, axis=1), 0.0)
        k *= 2
    f_row = f[0:1, :] * LOG2E
    r = lax.broadcasted_iota(jnp.int32, (rows, LANE), 0)
    c = lax.broadcasted_iota(jnp.int32, (rows, LANE), 1)
    rr = lax.broadcasted_iota(jnp.int32, (rows, PAGE_LANES), 0)
    cc = lax.broadcasted_iota(jnp.int32, (rows, PAGE_LANES), 1)
    same_head = jnp.bitwise_and(rr, N_HEADS - 1) == jnp.bitwise_and(cc, N_HEADS - 1)
    bias_scr[...] = jnp.where(same_head, 0.0, -jnp.inf)
    pad = jnp.zeros((LANE - rows, HEAD_DIM), BF16)
    kn = jnp.concatenate([kn_ref[0].astype(BF16), pad], axis=0)
    vn = jnp.concatenate([vn_ref[0].astype(BF16), pad], axis=0)
    head_bits = N_HEADS.bit_length() - 1
    tok_r = jnp.right_shift(r, head_bits)
    tok_c = jnp.right_shift(c, head_bits)
    head_ok = jnp.bitwise_and(r, N_HEADS - 1) == jnp.bitwise_and(c, N_HEADS - 1)
    s = _nt_dot(q, kn) - f_row
    s = jnp.where(head_ok, jnp.where(tok_c <= tok_r, s, -jnp.inf), -jnp.inf)
    init = (jnp.full((rows, 1), -jnp.inf, F32), jnp.zeros((rows, 1), F32), jnp.zeros((rows, HEAD_DIM), F32))
    m_scr[...], l_scr[...], acc_scr[...] = _softmax_step(init, s, vn)
    carry_scr[...] = jnp.zeros_like(carry_scr)


def _attn_s_scores(q, kp_refs, wt_refs, bias_scr, carry_scr):
    carry = carry_scr[...]
    bias = bias_scr[...]
    scores = []
    for kp_ref, wt_ref in zip(kp_refs, wt_refs):
        wt = wt_ref[0]
        dec = (wt[0:1, :] + carry) * LOG2E
        carry = carry + wt[1:2, :]
        k = kp_ref[...].reshape(PAGE_LANES, HEAD_DIM).astype(BF16)
        scores.append(_nt_dot(q, k) + bias + dec)
    carry_scr[...] = carry
    return jnp.concatenate(scores, axis=1)


def _attn_s_values(vp_refs):
    return jnp.concatenate([vp_ref[...].reshape(PAGE_LANES, HEAD_DIM).astype(BF16) for vp_ref in vp_refs], axis=0)


N_SEG = 8
S5_TB = 64
S5_PITCH = 520


def _cmul(ar, ai, br, bi):
    return ar * br - ai * bi, ar * bi + ai * br


def _s5_tail(h, u, wc, dsk, wg, bg):
    y = _dot_row_halves(h.astype(BF16), wc) + dsk * u
    z = jax.nn.gelu(y)
    zz = _dot_row_halves(z.astype(BF16), wg) + bg
    return zz[:, :CH_LANES] * jax.nn.sigmoid(zz[:, CH_LANES:])


def _s5_kernel(u_ref, wb_ref, wc_ref, wg_ref, bg_ref, dsk_ref, lam_ref, y_ref, hfin_ref, ubuf, ybuf, bu_scr):
    s_len = u_ref.shape[1]
    seg = s_len // N_SEG
    n_tb = seg // S5_TB
    assert seg <= S5_PITCH and seg % S5_TB == 0
    for i in range(N_SEG):
        ubuf[i * S5_PITCH:i * S5_PITCH + seg, :] = u_ref[0, i * seg:(i + 1) * seg, :]
    wb = wb_ref[0].astype(BF16)
    lam = lam_ref[0]
    l_re = jnp.broadcast_to(lam[:, :ST_LANES], (N_SEG, ST_LANES))
    l_im = jnp.broadcast_to(lam[:, ST_LANES:], (N_SEG, ST_LANES))

    def load_u(tb):
        t0 = tb * S5_TB
        return jnp.concatenate([ubuf[pl.ds(t0 + j, N_SEG, stride=S5_PITCH), :] for j in range(S5_TB)], axis=0)

    def project(tb, slot):
        u = load_u(jnp.minimum(tb, n_tb - 1))
        bu_scr[slot] = jnp.dot(u.astype(BF16), wb, preferred_element_type=F32)

    def scan_block(slot, h_re, h_im, keep):
        hs = []
        for j in range(S5_TB):
            n_re, n_im = _cmul(l_re, l_im, h_re, h_im)
            h_re = n_re + bu_scr[slot, j * N_SEG:(j + 1) * N_SEG, :ST_LANES]
            h_im = n_im + bu_scr[slot, j * N_SEG:(j + 1) * N_SEG, ST_LANES:]
            if keep:
                hs.append(jnp.concatenate([h_re, h_im], axis=1))
        return h_re, h_im, hs

    def two_blocks(stage):
        assert n_tb % 2 == 0
        return lambda jj, carry: stage(2 * jj + 1, 1, stage(2 * jj, 0, carry))

    def stage1(tb, slot, carry):
        project(tb + 1, 1 - slot)
        h_re, h_im, _ = scan_block(slot, *carry, keep=False)
        return h_re, h_im

    zero = jnp.zeros((N_SEG, ST_LANES), F32)
    project(0, 0)
    f_re, f_im = lax.fori_loop(0, n_tb // 2, two_blocks(stage1), (zero, zero))
    p_re, p_im = lam[:, :ST_LANES], lam[:, ST_LANES:]
    n_sq = seg.bit_length() - 1
    assert 1 << n_sq == seg
    for _ in range(n_sq):
        p_re, p_im = _cmul(p_re, p_im, p_re, p_im)
    c_re, c_im = [jnp.zeros((1, ST_LANES), F32)], [jnp.zeros((1, ST_LANES), F32)]
    for i in range(1, N_SEG):
        d_re, d_im = _cmul(p_re, p_im, c_re[-1], c_im[-1])
        c_re.append(f_re[i - 1:i, :] + d_re)
        c_im.append(f_im[i - 1:i, :] + d_im)
    wc, wg, bg, dsk = wc_ref[0].astype(BF16), wg_ref[0].astype(BF16), bg_ref[0], dsk_ref[0]

    def stage2(tb, slot, carry):
        project(tb + 1, 1 - slot)
        h_re, h_im, hs = scan_block(slot, *carry, keep=True)
        y = _s5_tail(jnp.concatenate(hs, axis=0), load_u(tb), wc, dsk, wg, bg)
        t0 = tb * S5_TB
        for j in range(S5_TB):
            ybuf[pl.ds(t0 + j, N_SEG, stride=S5_PITCH), :] = y[j * N_SEG:(j + 1) * N_SEG, :]
        return h_re, h_im

    project(0, 0)
    h_re, h_im = lax.fori_loop(0, n_tb // 2, two_blocks(stage2),
                               (jnp.concatenate(c_re, axis=0), jnp.concatenate(c_im, axis=0)))
    hfin_ref[0, 0] = jnp.concatenate([h_re[N_SEG - 1:, :], h_im[N_SEG - 1:, :]], axis=1)
    for i in range(N_SEG):
        y_ref[0, i * seg:(i + 1) * seg, :] = ybuf[i * S5_PITCH:i * S5_PITCH + seg, :].astype(BF16)


def _s5(u, wb, wc, wg, bg, dsk, lam):
    nb, s_len, _ = u.shape
    blk = lambda shape: pl.BlockSpec((1,) + shape, lambda b, g: (g, 0, 0))
    y, hfin = pl.pallas_call(
        _s5_kernel,
        grid=(nb, N_GBLK),
        in_specs=[pl.BlockSpec((1, s_len, CH_LANES), lambda b, g: (b, 0, g)),
                  blk((CH_LANES, 2 * ST_LANES)), blk((2 * ST_LANES, CH_LANES)),
                  blk((CH_LANES, 2 * CH_LANES)), blk((1, 2 * CH_LANES)), blk((1, CH_LANES)),
                  blk((1, 2 * ST_LANES))],
        out_specs=[pl.BlockSpec((1, s_len, CH_LANES), lambda b, g: (b, 0, g)),
                   pl.BlockSpec((1, 1, 1, 2 * ST_LANES), lambda b, g: (b, g, 0, 0))],
        out_shape=(jax.ShapeDtypeStruct((nb, s_len, SSM_WIDTH), BF16),
                   jax.ShapeDtypeStruct((nb, N_GBLK, 1, 2 * ST_LANES), F32)),
        scratch_shapes=[pltpu.VMEM((N_SEG * S5_PITCH, CH_LANES), F32),
                        pltpu.VMEM((N_SEG * S5_PITCH, CH_LANES), F32),
                        pltpu.VMEM((2, S5_TB * N_SEG, 2 * ST_LANES), F32)],
        compiler_params=_cparams(("arbitrary", "arbitrary")),
        name="s5",
    )(u, wb, wc, wg, bg, dsk, lam)
    hfin = hfin.reshape(nb, N_GBLK, 2, GROUPS_PER_BLOCK, SSM_STATE)
    return y, hfin[:, :, 0].reshape(nb, N_GROUPS, SSM_STATE), hfin[:, :, 1].reshape(nb, N_GROUPS, SSM_STATE)


def _s5s_kernel(u_ref, h0_ref, wb_ref, wc_ref, wg_ref, bg_ref, dsk_ref, lam_ref, y_ref, hfin_ref, *, nb):
    hi = lax.Precision.HIGHEST
    u = u_ref[...]
    n_t = u.shape[0] // nb
    bu = jnp.dot(u, wb_ref[0], precision=hi, preferred_element_type=F32)
    lam = lam_ref[0]
    l_re = jnp.broadcast_to(lam[:, :ST_LANES], (nb, ST_LANES))
    l_im = jnp.broadcast_to(lam[:, ST_LANES:], (nb, ST_LANES))
    h0 = h0_ref[0]
    h_re, h_im = h0[:, :ST_LANES], h0[:, ST_LANES:]
    hs = []
    for t in range(n_t):
        n_re, n_im = _cmul(l_re, l_im, h_re, h_im)
        h_re = n_re + bu[t * nb:(t + 1) * nb, :ST_LANES]
        h_im = n_im + bu[t * nb:(t + 1) * nb, ST_LANES:]
        hs.append(jnp.concatenate([h_re, h_im], axis=1))
    hfin_ref[0] = hs[-1]
    h = jnp.concatenate(hs, axis=0)
    y = jnp.dot(h, wc_ref[0], precision=hi, preferred_element_type=F32) + dsk_ref[0] * u
    z = jax.nn.gelu(y)
    zz = jnp.dot(z, wg_ref[0], precision=hi, preferred_element_type=F32) + bg_ref[0]
    y_ref[...] = (zz[:, :CH_LANES] * jax.nn.sigmoid(zz[:, CH_LANES:])).astype(BF16)


def _s5s(u_tm, h0, wb, wc, wg, bg, dsk, lam, nb):
    rows = u_tm.shape[0]
    blk = lambda shape: pl.BlockSpec((1,) + shape, lambda g: (g, 0, 0))
    return pl.pallas_call(
        functools.partial(_s5s_kernel, nb=nb),
        grid=(N_GBLK,),
        in_specs=[pl.BlockSpec((rows, CH_LANES), lambda g: (0, g)),
                  blk((nb, 2 * ST_LANES)),
                  blk((CH_LANES, 2 * ST_LANES)), blk((2 * ST_LANES, CH_LANES)),
                  blk((CH_LANES, 2 * CH_LANES)), blk((1, 2 * CH_LANES)), blk((1, CH_LANES)),
                  blk((1, 2 * ST_LANES))],
        out_specs=[pl.BlockSpec((rows, CH_LANES), lambda g: (0, g)), blk((nb, 2 * ST_LANES))],
        out_shape=(jax.ShapeDtypeStruct((rows, SSM_WIDTH), BF16),
                   jax.ShapeDtypeStruct((N_GBLK, nb, 2 * ST_LANES), F32)),
        compiler_params=_cparams(("arbitrary",)),
        name="s5s",
    )(u_tm, h0, wb, wc, wg, bg, dsk, lam)


def _outproj_kernel(att_ref, ssm_ref, x_ref, g1_ref, sc2_ref, sh2_ref, lng_ref, lnb_ref, wo_ref,
                    x1_ref, h2_ref):
    tm = x_ref.shape[1]
    n_sub = 2 if tm >= 512 else 1
    sub = tm // n_sub

    def mod_rows(ref, rows):
        return ref[0] if ref.shape[1] == 1 else ref[0, rows, :]

    for c in range(n_sub):
        rows = slice(c * sub, (c + 1) * sub)
        mix = (jnp.dot(att_ref[0, rows, :], wo_ref[:ATT_WIDTH, :], preferred_element_type=F32)
               + jnp.dot(ssm_ref[0, rows, :], wo_ref[ATT_WIDTH:, :], preferred_element_type=F32))
        x1 = (_ln(ALPHA * x_ref[0, rows, :] + (1.0 + mod_rows(g1_ref, rows)) * mix) * lng_ref[...]
              + lnb_ref[...])
        x1_ref[0, rows, :] = x1
        h2_ref[0, rows, :] = (_ln(x1) * (1.0 + mod_rows(sc2_ref, rows)) + mod_rows(sh2_ref, rows)).astype(BF16)


def _outproj(att, ssm, x, mod, ln_g, ln_b, w_o, tm):
    nb, l, _ = x.shape
    row_map = lambda b, i: (b, i, 0)
    const2 = lambda b, i: (0, 0)
    half = pl.BlockSpec((1, tm, ATT_WIDTH), row_map)
    full = pl.BlockSpec((1, tm, D_MODEL), row_map)
    vec = pl.BlockSpec((1, D_MODEL), const2)
    return pl.pallas_call(
        _outproj_kernel,
        grid=(nb, l // tm),
        in_specs=[half, half, full,
                  _mod_spec(mod, GATE1, nb), _mod_spec(mod, SCALE2, nb), _mod_spec(mod, SHIFT2, nb), vec, vec,
                  pl.BlockSpec((D_MODEL, D_MODEL), const2)],
        out_specs=[full, full],
        out_shape=(jax.ShapeDtypeStruct((nb, l, D_MODEL), F32),
                   jax.ShapeDtypeStruct((nb, l, D_MODEL), BF16)),
        compiler_params=_cparams(("arbitrary", "arbitrary")),
        name="outproj",
    )(att, ssm, x, mod, mod, mod, ln_g.reshape(1, D_MODEL), ln_b.reshape(1, D_MODEL), w_o)


FFN_TF = 1024
FUSED_TF = 512


def _ffn_zero(f, acc_scr):
    @pl.when(f == 0)
    def _():
        acc_scr[...] = jnp.zeros_like(acc_scr)


def _ffn_accumulate(h2_ref, wu_ref, wd_ref, acc_scr):
    up = jnp.dot(h2_ref[0], wu_ref[...], preferred_element_type=F32)
    act = jnp.square(jnp.maximum(up, 0.0)).astype(BF16)
    acc_scr[...] += jnp.dot(act, wd_ref[...], preferred_element_type=F32)


def _ffn_finish(f, x1_ref, g2_ref, lng_ref, lnb_ref, y_ref, acc_scr):
    @pl.when(f == pl.num_programs(2) - 1)
    def _():
        y_ref[0] = _ln(ALPHA * x1_ref[0] + (1.0 + g2_ref[0]) * acc_scr[...]) * lng_ref[...] + lnb_ref[...]


def _ffn_kernel(h2_ref, x1_ref, g2_ref, lng_ref, lnb_ref, wu_ref, wd_ref, y_ref, acc_scr):
    f = pl.program_id(2)
    _ffn_zero(f, acc_scr)
    _ffn_accumulate(h2_ref, wu_ref, wd_ref, acc_scr)
    _ffn_finish(f, x1_ref, g2_ref, lng_ref, lnb_ref, y_ref, acc_scr)


def _ffn_specs(mod, nb, tm, tf):
    wrap = lambda fn: (lambda b, i, f, *_: fn(b, i, f))
    row_map = wrap(lambda b, i, f: (b, i, 0))
    full = pl.BlockSpec((1, tm, D_MODEL), row_map)
    vec = pl.BlockSpec((1, D_MODEL), wrap(lambda b, i, f: (0, 0)))
    in_specs = [full, full, _mod_spec(mod, GATE2, nb), vec, vec,
                pl.BlockSpec((D_MODEL, tf), wrap(lambda b, i, f: (0, f))),
                pl.BlockSpec((tf, D_MODEL), wrap(lambda b, i, f: (f, 0)))]
    return in_specs, full


def _ffn(h2, x1, mod, ln_g, ln_b, w_up, w_down, tm):
    nb, l, _ = x1.shape
    in_specs, out_spec = _ffn_specs(mod, nb, tm, FFN_TF)
    return pl.pallas_call(
        _ffn_kernel,
        grid=(nb, l // tm, D_FF // FFN_TF),
        in_specs=in_specs,
        out_specs=out_spec,
        out_shape=jax.ShapeDtypeStruct((nb, l, D_MODEL), F32),
        scratch_shapes=[pltpu.VMEM((tm, D_MODEL), F32)],
        compiler_params=_cparams(("arbitrary", "arbitrary", "arbitrary")),
        name="ffn",
    )(h2, x1, mod, ln_g.reshape(1, D_MODEL), ln_b.reshape(1, D_MODEL), w_up, w_down)


def _ffn_attn_kernel(pt_ref, h2_ref, x1_ref, g2_ref, lng_ref, lnb_ref, wu_ref, wd_ref,
                     q_ref, kn_ref, vn_ref, lfn_ref, *refs, n_pg, steps_per_seq):
    kp_refs, vp_refs, wt_refs = refs[:n_pg], refs[n_pg:2 * n_pg], refs[2 * n_pg:3 * n_pg]
    y_ref, o_ref, acc_scr, bias_scr, m_scr, l_scr, a_scr, carry_scr = refs[3 * n_pg:]
    f = pl.program_id(2)
    t = (pl.program_id(0) * pl.num_programs(1) + pl.program_id(1)) * pl.num_programs(2) + f
    si = lax.rem(t, steps_per_seq)
    q = q_ref[0]

    @pl.when(si == 0)
    def _():
        _attn_s_init(q, kn_ref, vn_ref, lfn_ref, bias_scr, m_scr, l_scr, a_scr, carry_scr)

    _ffn_zero(f, acc_scr)
    s = _attn_s_scores(q, kp_refs, wt_refs, bias_scr, carry_scr)
    up = jnp.dot(h2_ref[0], wu_ref[...], preferred_element_type=F32)
    m_new, alpha, p, l_new = _softmax_weights(m_scr[...], l_scr[...], s)
    act = jnp.square(jnp.maximum(up, 0.0)).astype(BF16)
    acc_scr[...] += jnp.dot(act, wd_ref[...], preferred_element_type=F32)
    a_new = alpha * a_scr[...] + _dot_depth_halves(p, _attn_s_values(vp_refs))
    m_scr[...], l_scr[...], a_scr[...] = m_new, l_new, a_new

    @pl.when(si == steps_per_seq - 1)
    def _():
        o_ref[0] = a_new / l_new

    _ffn_finish(f, x1_ref, g2_ref, lng_ref, lnb_ref, y_ref, acc_scr)


def _ffn_attn(h2, x1, mod, ln_g, ln_b, w_up, w_down, tm,
              page_table, q, k_new, v_new, lf_new, cache_k, cache_v, decay_wt):
    nb, l, _ = x1.shape
    nbs, rows, _ = q.shape
    n_pages = page_table.shape[1]
    n_i, n_f = l // tm, D_FF // FUSED_TF
    n_steps = nb * n_i * n_f
    n_pg = nbs * n_pages // n_steps
    steps_per_seq = n_pages // n_pg
    assert n_pg * n_steps == nbs * n_pages and steps_per_seq * n_pg == n_pages

    def step(b, i, f):
        t = (b * n_i + i) * n_f + f
        return lax.div(t, steps_per_seq), lax.rem(t, steps_per_seq)

    seq = lambda b, i, f, pt: (step(b, i, f)[0], 0, 0)

    def page_spec(block, n_lead, pg):
        def idx(b, i, f, pt):
            sq, si = step(b, i, f)
            return (0,) * n_lead + (pt[sq, n_pages - 1 - (si * n_pg + pg)],) + (0,) * (len(block) - n_lead - 1)
        return pl.BlockSpec(block, idx)

    ffn_specs, y_spec = _ffn_specs(mod, nb, tm, FUSED_TF)
    kv_block = (None, None, PAGE, N_HEADS, HEAD_DIM)
    tok_spec = pl.BlockSpec((1, rows, HEAD_DIM), seq)
    grid_spec = pltpu.PrefetchScalarGridSpec(
        num_scalar_prefetch=1,
        grid=(nb, n_i, n_f),
        in_specs=ffn_specs + [tok_spec, tok_spec, tok_spec, pl.BlockSpec((1, 1, LANE), seq)]
                 + [page_spec(kv_block, 1, pg) for pg in range(n_pg)]
                 + [page_spec(kv_block, 1, pg) for pg in range(n_pg)]
                 + [page_spec((1, 2, PAGE_LANES), 0, pg) for pg in range(n_pg)],
        out_specs=[y_spec, tok_spec],
        scratch_shapes=[pltpu.VMEM((tm, D_MODEL), F32),
                        pltpu.VMEM((rows, PAGE_LANES), F32),
                        pltpu.VMEM((rows, 1), F32),
                        pltpu.VMEM((rows, 1), F32),
                        pltpu.VMEM((rows, HEAD_DIM), F32),
                        pltpu.VMEM((1, PAGE_LANES), F32)])
    return pl.pallas_call(
        functools.partial(_ffn_attn_kernel, n_pg=n_pg, steps_per_seq=steps_per_seq),
        grid_spec=grid_spec,
        out_shape=(jax.ShapeDtypeStruct((nb, l, D_MODEL), F32),
                   jax.ShapeDtypeStruct((nbs, rows, HEAD_DIM), F32)),
        compiler_params=_cparams(("arbitrary", "arbitrary", "arbitrary")),
        name="ffn_attn",
    )(page_table, h2, x1, mod, ln_g.reshape(1, D_MODEL), ln_b.reshape(1, D_MODEL), w_up, w_down,
      q, k_new, v_new, lf_new, *([cache_k] * n_pg), *([cache_v] * n_pg), *([decay_wt] * n_pg))


def kernel(x_prompt, x_sample, c_prompt, c_sample, cache_k, cache_v, cache_logf, state_ssm_re,
           state_ssm_im, page_table, w_ada, b_ada, w_in, b_f, w_o, a_re, a_im, log_dt, b_re, b_im,
           c_re, c_im, d_skip, w_glu, b_glu, ln1_g, ln1_b, w_up, w_down, ln2_g, ln2_b):
    assert w_ada.shape[0] == DEPTH == 1
    nbp, s_len, _ = x_prompt.shape
    nbs, n_q, _ = x_sample.shape
    n_seq = nbp + nbs

    c_all = jnp.concatenate([c_prompt, c_sample, jnp.zeros((ADA_ROWS - n_seq, D_MODEL), F32)], axis=0)
    mod = _ada(c_all, w_ada[0], b_ada[0]).reshape(ADA_ROWS, N_MOD, D_MODEL)
    mod_p = _mod_pack(mod[:nbp], 1)
    mod_s = _mod_pack(mod[nbp:n_seq], n_q)

    a = ATT_WIDTH
    w_in_t = w_in[0].T
    w_qkv = w_in_t[:3 * a].astype(BF16)
    w_u = w_in_t[3 * a + N_HEADS:].astype(BF16)
    w_ft = w_in_t[3 * a:3 * a + N_HEADS].astype(BF16)
    w_f = jnp.pad(w_ft, ((0, LANE - N_HEADS), (0, 0)))

    l_re, l_im, bb_re, bb_im = _s5prep(a_re[0], a_im[0], log_dt[0], b_re[0], b_im[0])
    wb, wc, wg, bg, dsk, lam = _s5_weights(l_re, l_im, bb_re, bb_im, c_re[0], c_im[0], d_skip[0],
                                           w_glu[0], b_glu[0])

    rows = nbs * n_q
    xs = x_sample.reshape(1, rows, D_MODEL)
    qs, ks, _, vs, _, us, lfcs, _ = _inproj(xs, mod_s, w_qkv, w_u, w_f, w_ft, b_f[0], tm=rows)
    n_pool = cache_k.shape[1]
    decay_wt = _decay(cache_logf[0].reshape(n_pool, PAGE_LANES))
    per_head = lambda t: t.reshape(nbs, n_q * N_HEADS, HEAD_DIM)
    lf_new = jnp.pad(lfcs.reshape(nbs, 1, n_q * N_HEADS), ((0, 0), (0, 0), (0, LANE - n_q * N_HEADS)))

    q, k, k_b, v, v_b, u, lfc, lfr = _inproj(x_prompt, mod_p, w_qkv, w_u, w_f, w_ft, b_f[0], tm=ROW_TILE)
    att, w_up_b, w_down_b, w_o_b = _attn(q, k_b, v_b, _cumsum(lfr), (w_up[0], w_down[0], w_o[0]))
    ssm, hp_re, hp_im = _s5(u, wb, wc, wg, bg, dsk, lam)
    x1, h2 = _outproj(att, ssm, x_prompt, mod_p, ln1_g[0], ln1_b[0], w_o_b, tm=ROW_TILE)
    y_p, att_s = _ffn_attn(h2, x1, mod_p, ln2_g[0], ln2_b[0], w_up_b, w_down_b, ROW_TILE,
                           page_table, per_head(qs), per_head(ks), per_head(vs), lf_new, cache_k, cache_v,
                           decay_wt)

    u_tm = us.reshape(nbs, n_q, SSM_WIDTH).transpose(1, 0, 2).reshape(rows, SSM_WIDTH)
    h0 = jnp.concatenate([state_ssm_re[0].reshape(nbs, N_GBLK, ST_LANES),
                          state_ssm_im[0].reshape(nbs, N_GBLK, ST_LANES)], axis=2).transpose(1, 0, 2)
    ssm_tm, hs_fin = _s5s(u_tm, h0, wb, wc, wg, bg, dsk, lam, nbs)
    ssm_s = ssm_tm.reshape(n_q, nbs, SSM_WIDTH).transpose(1, 0, 2).reshape(1, rows, SSM_WIDTH)
    hs_fin = hs_fin.transpose(1, 0, 2).reshape(nbs, N_GBLK, 2, GROUPS_PER_BLOCK, SSM_STATE)
    hs_re = hs_fin[:, :, 0].reshape(nbs, N_GROUPS, SSM_STATE)
    hs_im = hs_fin[:, :, 1].reshape(nbs, N_GROUPS, SSM_STATE)
    x1s, h2s = _outproj(att_s.reshape(1, rows, a).astype(BF16), ssm_s, xs, mod_s,
                        ln1_g[0], ln1_b[0], w_o_b, tm=rows)
    y_s = _ffn(h2s, x1s, mod_s, ln2_g[0], ln2_b[0], w_up_b, w_down_b, tm=rows)

    hd = (N_HEADS, HEAD_DIM)
    return (y_p, y_s.reshape(nbs, n_q, D_MODEL),
            k.reshape(1, nbp, s_len, *hd), v.reshape(1, nbp, s_len, *hd), lfc[None],
            hp_re[None], hp_im[None],
            ks.reshape(1, nbs, n_q, *hd), vs.reshape(1, nbs, n_q, *hd), lfcs.reshape(1, nbs, n_q, N_HEADS),
            hs_re[None], hs_im[None])
```

```python
import functools
import math

import jax
import jax.numpy as jnp
from jax import lax
from jax.experimental import pallas as pl
from jax.experimental.pallas import tpu as pltpu

F32 = jnp.float32
BF16 = jnp.bfloat16

LANE = 128
D_MODEL = 2048
ATT_WIDTH = 1024
SSM_WIDTH = 1024
HEAD_DIM = 128
N_HEADS = 8
SSM_GROUP = 16
N_GROUPS = 64
SSM_STATE = 64
D_FF = 8192
N_MOD = 6
PAGE = 128
DEPTH = 1
ALPHA = (2 * DEPTH) ** 0.25
LN_EPS = 1e-5
LOG2E = math.log2(math.e)
QSCALE = HEAD_DIM ** -0.5 * LOG2E

GROUPS_PER_BLOCK = 8
N_GBLK = N_GROUPS // GROUPS_PER_BLOCK
ST_LANES = GROUPS_PER_BLOCK * SSM_STATE
CH_LANES = GROUPS_PER_BLOCK * SSM_GROUP

VMEM_LIMIT = 56 * 1024 * 1024
ROW_TILE = 512
BF16_SUBLANES = 16
ADA_ROWS = 16


def _cparams(sem):
    return pltpu.CompilerParams(dimension_semantics=sem, vmem_limit_bytes=VMEM_LIMIT)


def _ln(x):
    mu = jnp.mean(x, axis=-1, keepdims=True)
    xc = x - mu
    var = jnp.mean(xc * xc, axis=-1, keepdims=True)
    return xc * lax.rsqrt(var + LN_EPS)


def _log_sigmoid(x):
    return jnp.minimum(x, 0.0) - jnp.log1p(jnp.exp(-jnp.abs(x)))


def _nt_dot(a, b):
    return lax.dot_general(a, b, (((1,), (1,)), ((), ())), preferred_element_type=F32)


def _softmax_weights(m, l, s):
    m_new = jnp.maximum(m, jnp.max(s, axis=1, keepdims=True))
    alpha = jnp.exp2(m - m_new)
    p = jnp.exp2(s - m_new)
    return m_new, alpha, p.astype(BF16), alpha * l + jnp.sum(p, axis=1, keepdims=True)


def _dot_row_halves(a, b):
    half = a.shape[0] // 2
    return jnp.concatenate([jnp.dot(a[:half], b, preferred_element_type=F32),
                            jnp.dot(a[half:], b, preferred_element_type=F32)], axis=0)


def _dot_depth_halves(a, b):
    half = a.shape[1] // 2
    return (jnp.dot(a[:, :half], b[:half], preferred_element_type=F32)
            + jnp.dot(a[:, half:], b[half:], preferred_element_type=F32))


def _softmax_step(carry, s, v):
    m, l, acc = carry
    m_new, alpha, p, l_new = _softmax_weights(m, l, s)
    return m_new, l_new, alpha * acc + jnp.dot(p, v, preferred_element_type=F32)


def _ada_kernel(c_ref, w_ref, b_ref, o_ref):
    c = c_ref[...]
    s = c * jax.nn.sigmoid(c)
    o_ref[...] = jnp.dot(s, w_ref[...], preferred_element_type=F32) + b_ref[...]


def _ada(c_all, w_ada, b_ada):
    rows = c_all.shape[0]
    n = w_ada.shape[1]
    tn = 1024
    return pl.pallas_call(
        _ada_kernel,
        grid=(n // tn,),
        in_specs=[pl.BlockSpec((rows, D_MODEL), lambda j: (0, 0)),
                  pl.BlockSpec((D_MODEL, tn), lambda j: (0, j)),
                  pl.BlockSpec((1, tn), lambda j: (0, j))],
        out_specs=pl.BlockSpec((rows, tn), lambda j: (0, j)),
        out_shape=jax.ShapeDtypeStruct((rows, n), F32),
        compiler_params=_cparams(("arbitrary",)),
        name="ada",
    )(c_all, w_ada, b_ada.reshape(1, n))


SHIFT1, SCALE1, GATE1, SHIFT2, SCALE2, GATE2 = range(N_MOD)


def _mod_pack(mod_rows, repeat):
    n_seq = mod_rows.shape[0]
    m = jnp.swapaxes(mod_rows, 0, 1)
    if repeat == 1:
        return m.reshape(N_MOD * n_seq, 1, D_MODEL)
    return jnp.repeat(m, repeat, axis=1)


def _mod_spec(mod, chunk, nb):
    r = mod.shape[1]
    return pl.BlockSpec((1, r, D_MODEL), lambda b, i, *_: (chunk * nb + b, 0 if r == 1 else i, 0))


def _s5prep_kernel(are_ref, aim_ref, ldt_ref, arx_ref, aix_ref, ldx_ref, bre_ref, bim_ref,
                   lre_ref, lim_ref, bbre_ref, bbim_ref):
    def lam(a_re, a_im, log_dt):
        dt = jnp.exp(log_dt)
        mag = jnp.exp(a_re * dt)
        return mag * jnp.cos(a_im * dt), mag * jnp.sin(a_im * dt)

    l_re, l_im = lam(are_ref[...], aim_ref[...], ldt_ref[...])
    lre_ref[...] = l_re
    lim_ref[...] = l_im
    a_re, a_im = arx_ref[...], aix_ref[...]
    x_re, x_im = lam(a_re, a_im, ldx_ref[...])
    den = a_re * a_re + a_im * a_im
    n_re = x_re - 1.0
    k_re = (n_re * a_re + x_im * a_im) / den
    k_im = (x_im * a_re - n_re * a_im) / den
    b_re, b_im = bre_ref[...], bim_ref[...]
    bbre_ref[...] = k_re * b_re - k_im * b_im
    bbim_ref[...] = k_re * b_im + k_im * b_re


def _s5prep(a_re, a_im, log_dt, b_re, b_im):
    g, p, c = b_re.shape
    ldt = jnp.broadcast_to(log_dt[:, None], (g, p))
    ex = lambda a: jnp.broadcast_to(a[:, :, None], (g, p, c)).reshape(g, p * c)
    small = jax.ShapeDtypeStruct((g, p), F32)
    big = jax.ShapeDtypeStruct((g, p * c), F32)
    l_re, l_im, bb_re, bb_im = pl.pallas_call(
        _s5prep_kernel, out_shape=(small, small, big, big), name="s5prep",
    )(a_re, a_im, ldt, ex(a_re), ex(a_im), ex(ldt), b_re.reshape(g, p * c), b_im.reshape(g, p * c))
    return l_re, l_im, bb_re.reshape(g, p, c), bb_im.reshape(g, p, c)


def _blockdiag(w):
    g, a, b = w.shape
    w = w.reshape(N_GBLK, GROUPS_PER_BLOCK, a, b)
    eye = jnp.eye(GROUPS_PER_BLOCK, dtype=w.dtype)
    return jnp.einsum('xgab,gh->xgahb', w, eye).reshape(N_GBLK, GROUPS_PER_BLOCK * a, GROUPS_PER_BLOCK * b)


def _s5_weights(l_re, l_im, bb_re, bb_im, c_re, c_im, d_skip, w_glu, b_glu):
    wb = jnp.concatenate([_blockdiag(jnp.swapaxes(bb_re, 1, 2)),
                          _blockdiag(jnp.swapaxes(bb_im, 1, 2))], axis=2)
    wc = jnp.concatenate([_blockdiag(jnp.swapaxes(c_re, 1, 2)),
                          _blockdiag(-jnp.swapaxes(c_im, 1, 2))], axis=1)
    wg = jnp.concatenate([_blockdiag(w_glu[:, :, :SSM_GROUP]),
                          _blockdiag(w_glu[:, :, SSM_GROUP:])], axis=2)
    bg = jnp.concatenate([b_glu[:, :SSM_GROUP].reshape(N_GBLK, 1, CH_LANES),
                          b_glu[:, SSM_GROUP:].reshape(N_GBLK, 1, CH_LANES)], axis=2)
    dsk = d_skip.reshape(N_GBLK, 1, CH_LANES)
    lam = jnp.concatenate([l_re.reshape(N_GBLK, 1, ST_LANES), l_im.reshape(N_GBLK, 1, ST_LANES)], axis=2)
    return wb, wc, wg, bg, dsk, lam


def _inproj_kernel(x_ref, sh_ref, sc_ref, w_ref, wu_ref, wf_ref, wft_ref, bf_ref, bfr_ref,
                   q_ref, kf_ref, kb_ref, vf_ref, vb_ref, u_ref, lfc_ref, lfr_ref):
    hb = (_ln(x_ref[0]) * (1.0 + sc_ref[0]) + sh_ref[0]).astype(BF16)
    lfc_ref[0] = _log_sigmoid(_nt_dot(hb, wf_ref[...]) + bf_ref[...])[:, :N_HEADS]
    lfr_ref[0] = _log_sigmoid(_nt_dot(wft_ref[...], hb) + bfr_ref[...])

    def proj(w, group):
        return _nt_dot(hb, w[group * ATT_WIDTH:(group + 1) * ATT_WIDTH, :])

    q_ref[0] = (proj(w_ref, 0) * QSCALE).astype(BF16)
    k = proj(w_ref, 1)
    kf_ref[0] = k
    kb_ref[0] = k.astype(BF16)
    v = proj(w_ref, 2)
    vf_ref[0] = v
    vb_ref[0] = v.astype(BF16)
    u_ref[0] = proj(wu_ref, 0)


def _inproj(x, mod, w_qkv, w_u, w_f, w_ft, b_f, tm):
    nb, l, _ = x.shape
    row_map = lambda b, i: (b, i, 0)
    const2 = lambda b, i: (0, 0)
    wide = lambda dt: jax.ShapeDtypeStruct((nb, l, ATT_WIDTH), dt)
    return pl.pallas_call(
        _inproj_kernel,
        grid=(nb, l // tm),
        in_specs=[pl.BlockSpec((1, tm, D_MODEL), row_map),
                  _mod_spec(mod, SHIFT1, nb),
                  _mod_spec(mod, SCALE1, nb),
                  pl.BlockSpec((3 * ATT_WIDTH, D_MODEL), const2, pipeline_mode=pl.Buffered(1)),
                  pl.BlockSpec((SSM_WIDTH, D_MODEL), const2, pipeline_mode=pl.Buffered(1)),
                  pl.BlockSpec((LANE, D_MODEL), const2),
                  pl.BlockSpec((N_HEADS, D_MODEL), const2),
                  pl.BlockSpec((1, LANE), const2),
                  pl.BlockSpec((N_HEADS, 1), const2)],
        out_specs=[pl.BlockSpec((1, tm, ATT_WIDTH), row_map)] * 6
                  + [pl.BlockSpec((1, tm, N_HEADS), row_map),
                     pl.BlockSpec((1, N_HEADS, tm), lambda b, i: (b, 0, i))],
        out_shape=(wide(BF16), wide(F32), wide(BF16), wide(F32), wide(BF16), wide(F32),
                   jax.ShapeDtypeStruct((nb, l, N_HEADS), F32),
                   jax.ShapeDtypeStruct((nb, N_HEADS, l), F32)),
        compiler_params=_cparams(("arbitrary", "arbitrary")),
        name="inproj",
    )(x, mod, mod, w_qkv, w_u, w_f, w_ft, jnp.pad(b_f, (0, LANE - N_HEADS)).reshape(1, LANE),
      b_f.reshape(N_HEADS, 1))


CUM_BLK = 512


def _cumsum_kernel(lfr_ref, fr_ref):
    s_len = lfr_ref.shape[2]
    r = lax.broadcasted_iota(jnp.int32, (CUM_BLK, CUM_BLK), 0)
    c = lax.broadcasted_iota(jnp.int32, (CUM_BLK, CUM_BLK), 1)
    tri_u = (r <= c).astype(F32)
    carry = jnp.zeros((N_HEADS, 1), F32)
    for blk in range(s_len // CUM_BLK):
        cols = slice(blk * CUM_BLK, (blk + 1) * CUM_BLK)
        fr = jnp.dot(lfr_ref[0, :, cols], tri_u, precision=lax.Precision.HIGHEST,
                     preferred_element_type=F32) + carry
        fr_ref[0, :, cols] = fr * LOG2E
        carry = fr[:, CUM_BLK - 1:CUM_BLK]


def _cumsum(lfr):
    nb, _, s_len = lfr.shape
    rspec = pl.BlockSpec((1, N_HEADS, s_len), lambda b: (b, 0, 0))
    return pl.pallas_call(
        _cumsum_kernel, grid=(nb,), in_specs=[rspec], out_specs=rspec,
        out_shape=jax.ShapeDtypeStruct(lfr.shape, F32),
        compiler_params=_cparams(("arbitrary",)), name="cumsum",
    )(lfr)


ATT_BLK = 1024


def _attn_kernel(q_ref, k_ref, v_ref, fr_ref, *refs):
    n_w = (len(refs) - 2) // 2
    w_refs, o_ref, wb_refs, s_scr = refs[:n_w], refs[n_w], refs[n_w + 1:2 * n_w + 1], refs[2 * n_w + 1]
    for w_ref, wb_ref in zip(w_refs, wb_refs):
        wb_ref[...] = w_ref[...].astype(BF16)
    h = pl.program_id(1)
    qi = pl.program_id(2)
    t = ATT_BLK
    q = q_ref[0]

    def scores(j):
        start = pl.multiple_of(j * t, t)
        fk = fr_ref[0, pl.ds(h, 1), pl.ds(start, t)]
        return _nt_dot(q, k_ref[0, pl.ds(start, t), :]) - fk

    def values(j):
        return v_ref[0, pl.ds(pl.multiple_of(j * t, t), t), :]

    def stage(j, carry, slot):
        s_scr[1 - slot] = scores(j + 1)
        return _softmax_step(carry, s_scr[slot], values(j))

    def pair(jj, carry):
        return stage(2 * jj + 1, stage(2 * jj, carry, 0), 1)

    def diagonal(carry, slot):
        row = lax.broadcasted_iota(jnp.int32, (t, t), 0)
        col = lax.broadcasted_iota(jnp.int32, (t, t), 1)
        return _softmax_step(carry, jnp.where(col <= row, s_scr[slot], -jnp.inf), values(qi))

    s_scr[0] = scores(0)
    init = (jnp.full((t, 1), -jnp.inf, F32), jnp.zeros((t, 1), F32), jnp.zeros((t, HEAD_DIM), F32))
    carry = lax.fori_loop(0, qi // 2, pair, init)
    m, l, acc = lax.cond(qi % 2 == 1,
                         lambda c: diagonal(stage(qi - 1, c, 0), 1),
                         lambda c: diagonal(c, 0), carry)
    o_ref[0] = (acc / l).astype(BF16)


def _attn(q, k, v, fr, weights):
    nb, s_len, _ = q.shape
    t = ATT_BLK
    n_q = s_len // t
    n_steps = nb * N_HEADS * n_q
    qspec = pl.BlockSpec((1, t, HEAD_DIM), lambda b, h, i: (b, i, h))
    kvspec = pl.BlockSpec((1, s_len, HEAD_DIM), lambda b, h, i: (b, 0, h))
    wspecs = [pl.BlockSpec((w.shape[0] // n_steps, w.shape[1]), lambda b, h, i: ((b * N_HEADS + h) * n_q + i, 0))
              for w in weights]
    assert all(w.shape[0] % (BF16_SUBLANES * n_steps) == 0 for w in weights)
    return pl.pallas_call(
        _attn_kernel,
        grid=(nb, N_HEADS, n_q),
        in_specs=[qspec, kvspec, kvspec,
                  pl.BlockSpec((1, N_HEADS, s_len), lambda b, h, i: (b, 0, 0))] + wspecs,
        out_specs=[qspec] + wspecs,
        out_shape=[jax.ShapeDtypeStruct((nb, s_len, ATT_WIDTH), BF16)]
                  + [jax.ShapeDtypeStruct(w.shape, BF16) for w in weights],
        scratch_shapes=[pltpu.VMEM((2, t, t), F32)],
        compiler_params=_cparams(("arbitrary", "arbitrary", "arbitrary")),
        name="attn",
    )(q, k, v, fr, *weights)


PAGE_LANES = PAGE * N_HEADS


def _decay_kernel(lf_ref, wt_ref):
    x = lf_ref[...]
    n = x.shape[1]
    lane = lax.broadcasted_iota(jnp.int32, x.shape, 1)
    suffix, total = x, x
    k = N_HEADS
    while k < n:
        suffix = suffix + jnp.where(lane < n - k, pltpu.roll(suffix, n - k, axis=1), 0.0)
        total = total + pltpu.roll(total, k, axis=1)
        k *= 2
    wt_ref[:, 0, :] = suffix - x
    wt_ref[:, 1, :] = total


def _decay(lf_flat):
    n_pool, n = lf_flat.shape
    rb = 256
    return pl.pallas_call(
        _decay_kernel,
        grid=(n_pool // rb,),
        in_specs=[pl.BlockSpec((rb, n), lambda i: (i, 0))],
        out_specs=pl.BlockSpec((rb, 2, n), lambda i: (i, 0, 0)),
        out_shape=jax.ShapeDtypeStruct((n_pool, 2, n), F32),
        compiler_params=_cparams(("arbitrary",)),
        name="decay",
    )(lf_flat)


def _attn_s_init(q, kn_ref, vn_ref, lfn_ref, bias_scr, m_scr, l_scr, acc_scr, carry_scr):
    rows = q.shape[0]
    f = jnp.broadcast_to(lfn_ref[0], (N_HEADS, LANE))
    lane8 = lax.broadcasted_iota(jnp.int32, f.shape, 1)
    k = N_HEADS
    while k < rows:
        f = f + jnp.where(lane8 >= k, pltpu.roll(f, k, axis=1), 0.0)
        k *= 2
    f_row = f[0:1, :] * LOG2E
    r = lax.broadcasted_iota(jnp.int32, (rows, LANE), 0)
    c = lax.broadcasted_iota(jnp.int32, (rows, LANE), 1)
    rr = lax.broadcasted_iota(jnp.int32, (rows, PAGE_LANES), 0)
    cc = lax.broadcasted_iota(jnp.int32, (rows, PAGE_LANES), 1)
    same_head = jnp.bitwise_and(rr, N_HEADS - 1) == jnp.bitwise_and(cc, N_HEADS - 1)
    bias_scr[...] = jnp.where(same_head, 0.0, -jnp.inf)
    pad = jnp.zeros((LANE - rows, HEAD_DIM), BF16)
    kn = jnp.concatenate([kn_ref[0].astype(BF16), pad], axis=0)
    vn = jnp.concatenate([vn_ref[0].astype(BF16), pad], axis=0)
    head_bits = N_HEADS.bit_length() - 1
    tok_r = jnp.right_shift(r, head_bits)
    tok_c = jnp.right_shift(c, head_bits)
    head_ok = jnp.bitwise_and(r, N_HEADS - 1) == jnp.bitwise_and(c, N_HEADS - 1)
    s = _nt_dot(q, kn) - f_row
    s = jnp.where(head_ok, jnp.where(tok_c <= tok_r, s, -jnp.inf), -jnp.inf)
    init = (jnp.full((rows, 1), -jnp.inf, F32), jnp.zeros((rows, 1), F32), jnp.zeros((rows, HEAD_DIM), F32))
    m_scr[...], l_scr[...], acc_scr[...] = _softmax_step(init, s, vn)
    carry_scr[...] = jnp.zeros_like(carry_scr)


def _attn_s_scores(q, kp_refs, wt_refs, bias_scr, carry_scr):
    carry = carry_scr[...]
    bias = bias_scr[...]
    scores = []
    for kp_ref, wt_ref in zip(kp_refs, wt_refs):
        wt = wt_ref[0]
        dec = (wt[0:1, :] + carry) * LOG2E
        carry = carry + wt[1:2, :]
        k = kp_ref[...].reshape(PAGE_LANES, HEAD_DIM).astype(BF16)
        scores.append(_nt_dot(q, k) + bias + dec)
    carry_scr[...] = carry
    return jnp.concatenate(scores, axis=1)


def _attn_s_values(vp_refs):
    return jnp.concatenate([vp_ref[...].reshape(PAGE_LANES, HEAD_DIM).astype(BF16) for vp_ref in vp_refs], axis=0)


N_SEG = 8
S5_TB = 64
S5_PITCH = 520


def _cmul(ar, ai, br, bi):
    return ar * br - ai * bi, ar * bi + ai * br


def _s5_tail(h, u, wc, dsk, wg, bg):
    y = _dot_row_halves(h.astype(BF16), wc) + dsk * u
    z = jax.nn.gelu(y)
    zz = _dot_row_halves(z.astype(BF16), wg) + bg
    return zz[:, :CH_LANES] * jax.nn.sigmoid(zz[:, CH_LANES:])


def _s5_kernel(u_ref, wb_ref, wc_ref, wg_ref, bg_ref, dsk_ref, lam_ref, y_ref, hfin_ref, ubuf, ybuf, bu_scr):
    s_len = u_ref.shape[1]
    seg = s_len // N_SEG
    n_tb = seg // S5_TB
    assert seg <= S5_PITCH and seg % S5_TB == 0
    for i in range(N_SEG):
        ubuf[i * S5_PITCH:i * S5_PITCH + seg, :] = u_ref[0, i * seg:(i + 1) * seg, :]
    wb = wb_ref[0].astype(BF16)
    lam = lam_ref[0]
    l_re = jnp.broadcast_to(lam[:, :ST_LANES], (N_SEG, ST_LANES))
    l_im = jnp.broadcast_to(lam[:, ST_LANES:], (N_SEG, ST_LANES))

    def load_u(tb):
        t0 = tb * S5_TB
        return jnp.concatenate([ubuf[pl.ds(t0 + j, N_SEG, stride=S5_PITCH), :] for j in range(S5_TB)], axis=0)

    def project(tb):
        bu_scr[tb] = jnp.dot(load_u(tb).astype(BF16), wb, preferred_element_type=F32)

    def scan_block(tb, h_re, h_im, keep):
        hs = []
        for j in range(S5_TB):
            n_re, n_im = _cmul(l_re, l_im, h_re, h_im)
            h_re = n_re + bu_scr[tb, j * N_SEG:(j + 1) * N_SEG, :ST_LANES]
            h_im = n_im + bu_scr[tb, j * N_SEG:(j + 1) * N_SEG, ST_LANES:]
            if keep:
                hs.append(jnp.concatenate([h_re, h_im], axis=1))
        return h_re, h_im, hs

    f_re = f_im = jnp.zeros((N_SEG, ST_LANES), F32)
    project(0)
    for tb in range(n_tb):
        if tb + 1 < n_tb:
            project(tb + 1)
        f_re, f_im, _ = scan_block(tb, f_re, f_im, keep=False)
    p_re, p_im = lam[:, :ST_LANES], lam[:, ST_LANES:]
    n_sq = seg.bit_length() - 1
    assert 1 << n_sq == seg
    for _ in range(n_sq):
        p_re, p_im = _cmul(p_re, p_im, p_re, p_im)
    c_re, c_im = [jnp.zeros((1, ST_LANES), F32)], [jnp.zeros((1, ST_LANES), F32)]
    for i in range(1, N_SEG):
        d_re, d_im = _cmul(p_re, p_im, c_re[-1], c_im[-1])
        c_re.append(f_re[i - 1:i, :] + d_re)
        c_im.append(f_im[i - 1:i, :] + d_im)
    wc, wg, bg, dsk = wc_ref[0].astype(BF16), wg_ref[0].astype(BF16), bg_ref[0], dsk_ref[0]

    h_re, h_im = jnp.concatenate(c_re, axis=0), jnp.concatenate(c_im, axis=0)
    for tb in range(n_tb):
        h_re, h_im, hs = scan_block(tb, h_re, h_im, keep=True)
        y = _s5_tail(jnp.concatenate(hs, axis=0), load_u(tb), wc, dsk, wg, bg)
        for j in range(S5_TB):
            ybuf[pl.ds(tb * S5_TB + j, N_SEG, stride=S5_PITCH), :] = y[j * N_SEG:(j + 1) * N_SEG, :]
    hfin_ref[0, 0] = jnp.concatenate([h_re[N_SEG - 1:, :], h_im[N_SEG - 1:, :]], axis=1)
    for i in range(N_SEG):
        y_ref[0, i * seg:(i + 1) * seg, :] = ybuf[i * S5_PITCH:i * S5_PITCH + seg, :].astype(BF16)


def _s5(u, wb, wc, wg, bg, dsk, lam):
    nb, s_len, _ = u.shape
    blk = lambda shape: pl.BlockSpec((1,) + shape, lambda b, g: (g, 0, 0))
    y, hfin = pl.pallas_call(
        _s5_kernel,
        grid=(nb, N_GBLK),
        in_specs=[pl.BlockSpec((1, s_len, CH_LANES), lambda b, g: (b, 0, g)),
                  blk((CH_LANES, 2 * ST_LANES)), blk((2 * ST_LANES, CH_LANES)),
                  blk((CH_LANES, 2 * CH_LANES)), blk((1, 2 * CH_LANES)), blk((1, CH_LANES)),
                  blk((1, 2 * ST_LANES))],
        out_specs=[pl.BlockSpec((1, s_len, CH_LANES), lambda b, g: (b, 0, g)),
                   pl.BlockSpec((1, 1, 1, 2 * ST_LANES), lambda b, g: (b, g, 0, 0))],
        out_shape=(jax.ShapeDtypeStruct((nb, s_len, SSM_WIDTH), BF16),
                   jax.ShapeDtypeStruct((nb, N_GBLK, 1, 2 * ST_LANES), F32)),
        scratch_shapes=[pltpu.VMEM((N_SEG * S5_PITCH, CH_LANES), F32),
                        pltpu.VMEM((N_SEG * S5_PITCH, CH_LANES), F32),
                        pltpu.VMEM((s_len // (S5_TB * N_SEG), S5_TB * N_SEG, 2 * ST_LANES), F32)],
        compiler_params=_cparams(("arbitrary", "arbitrary")),
        name="s5",
    )(u, wb, wc, wg, bg, dsk, lam)
    hfin = hfin.reshape(nb, N_GBLK, 2, GROUPS_PER_BLOCK, SSM_STATE)
    return y, hfin[:, :, 0].reshape(nb, N_GROUPS, SSM_STATE), hfin[:, :, 1].reshape(nb, N_GROUPS, SSM_STATE)


def _s5s_kernel(u_ref, h0_ref, wb_ref, wc_ref, wg_ref, bg_ref, dsk_ref, lam_ref, y_ref, hfin_ref, *, nb):
    hi = lax.Precision.HIGHEST
    u = u_ref[...]
    n_t = u.shape[0] // nb
    bu = jnp.dot(u, wb_ref[0], precision=hi, preferred_element_type=F32)
    lam = lam_ref[0]
    l_re = jnp.broadcast_to(lam[:, :ST_LANES], (nb, ST_LANES))
    l_im = jnp.broadcast_to(lam[:, ST_LANES:], (nb, ST_LANES))
    h0 = h0_ref[0]
    h_re, h_im = h0[:, :ST_LANES], h0[:, ST_LANES:]
    hs = []
    for t in range(n_t):
        n_re, n_im = _cmul(l_re, l_im, h_re, h_im)
        h_re = n_re + bu[t * nb:(t + 1) * nb, :ST_LANES]
        h_im = n_im + bu[t * nb:(t + 1) * nb, ST_LANES:]
        hs.append(jnp.concatenate([h_re, h_im], axis=1))
    hfin_ref[0] = hs[-1]
    h = jnp.concatenate(hs, axis=0)
    y = jnp.dot(h, wc_ref[0], precision=hi, preferred_element_type=F32) + dsk_ref[0] * u
    z = jax.nn.gelu(y)
    zz = jnp.dot(z, wg_ref[0], precision=hi, preferred_element_type=F32) + bg_ref[0]
    y_ref[...] = (zz[:, :CH_LANES] * jax.nn.sigmoid(zz[:, CH_LANES:])).astype(BF16)


def _s5s(u_tm, h0, wb, wc, wg, bg, dsk, lam, nb):
    rows = u_tm.shape[0]
    blk = lambda shape: pl.BlockSpec((1,) + shape, lambda g: (g, 0, 0))
    return pl.pallas_call(
        functools.partial(_s5s_kernel, nb=nb),
        grid=(N_GBLK,),
        in_specs=[pl.BlockSpec((rows, CH_LANES), lambda g: (0, g)),
                  blk((nb, 2 * ST_LANES)),
                  blk((CH_LANES, 2 * ST_LANES)), blk((2 * ST_LANES, CH_LANES)),
                  blk((CH_LANES, 2 * CH_LANES)), blk((1, 2 * CH_LANES)), blk((1, CH_LANES)),
                  blk((1, 2 * ST_LANES))],
        out_specs=[pl.BlockSpec((rows, CH_LANES), lambda g: (0, g)), blk((nb, 2 * ST_LANES))],
        out_shape=(jax.ShapeDtypeStruct((rows, SSM_WIDTH), BF16),
                   jax.ShapeDtypeStruct((N_GBLK, nb, 2 * ST_LANES), F32)),
        compiler_params=_cparams(("arbitrary",)),
        name="s5s",
    )(u_tm, h0, wb, wc, wg, bg, dsk, lam)


def _outproj_kernel(att_ref, ssm_ref, x_ref, g1_ref, sc2_ref, sh2_ref, lng_ref, lnb_ref, wo_ref,
                    x1_ref, h2_ref):
    tm = x_ref.shape[1]
    n_sub = 2 if tm >= 512 else 1
    sub = tm // n_sub

    def mod_rows(ref, rows):
        return ref[0] if ref.shape[1] == 1 else ref[0, rows, :]

    for c in range(n_sub):
        rows = slice(c * sub, (c + 1) * sub)
        mix = (jnp.dot(att_ref[0, rows, :], wo_ref[:ATT_WIDTH, :], preferred_element_type=F32)
               + jnp.dot(ssm_ref[0, rows, :], wo_ref[ATT_WIDTH:, :], preferred_element_type=F32))
        x1 = (_ln(ALPHA * x_ref[0, rows, :] + (1.0 + mod_rows(g1_ref, rows)) * mix) * lng_ref[...]
              + lnb_ref[...])
        x1_ref[0, rows, :] = x1
        h2_ref[0, rows, :] = (_ln(x1) * (1.0 + mod_rows(sc2_ref, rows)) + mod_rows(sh2_ref, rows)).astype(BF16)


def _outproj(att, ssm, x, mod, ln_g, ln_b, w_o, tm):
    nb, l, _ = x.shape
    row_map = lambda b, i: (b, i, 0)
    const2 = lambda b, i: (0, 0)
    half = pl.BlockSpec((1, tm, ATT_WIDTH), row_map)
    full = pl.BlockSpec((1, tm, D_MODEL), row_map)
    vec = pl.BlockSpec((1, D_MODEL), const2)
    return pl.pallas_call(
        _outproj_kernel,
        grid=(nb, l // tm),
        in_specs=[half, half, full,
                  _mod_spec(mod, GATE1, nb), _mod_spec(mod, SCALE2, nb), _mod_spec(mod, SHIFT2, nb), vec, vec,
                  pl.BlockSpec((D_MODEL, D_MODEL), const2)],
        out_specs=[full, full],
        out_shape=(jax.ShapeDtypeStruct((nb, l, D_MODEL), F32),
                   jax.ShapeDtypeStruct((nb, l, D_MODEL), BF16)),
        compiler_params=_cparams(("arbitrary", "arbitrary")),
        name="outproj",
    )(att, ssm, x, mod, mod, mod, ln_g.reshape(1, D_MODEL), ln_b.reshape(1, D_MODEL), w_o)


FFN_TF = 1024
FUSED_TF = 512


def _ffn_zero(f, acc_scr):
    @pl.when(f == 0)
    def _():
        acc_scr[...] = jnp.zeros_like(acc_scr)


def _ffn_accumulate(h2_ref, wu_ref, wd_ref, acc_scr):
    up = jnp.dot(h2_ref[0], wu_ref[...], preferred_element_type=F32)
    act = jnp.square(jnp.maximum(up, 0.0)).astype(BF16)
    acc_scr[...] += jnp.dot(act, wd_ref[...], preferred_element_type=F32)


def _ffn_finish(f, x1_ref, g2_ref, lng_ref, lnb_ref, y_ref, acc_scr):
    @pl.when(f == pl.num_programs(2) - 1)
    def _():
        y_ref[0] = _ln(ALPHA * x1_ref[0] + (1.0 + g2_ref[0]) * acc_scr[...]) * lng_ref[...] + lnb_ref[...]


def _ffn_kernel(h2_ref, x1_ref, g2_ref, lng_ref, lnb_ref, wu_ref, wd_ref, y_ref, acc_scr):
    f = pl.program_id(2)
    _ffn_zero(f, acc_scr)
    _ffn_accumulate(h2_ref, wu_ref, wd_ref, acc_scr)
    _ffn_finish(f, x1_ref, g2_ref, lng_ref, lnb_ref, y_ref, acc_scr)


def _ffn_specs(mod, nb, tm, tf):
    wrap = lambda fn: (lambda b, i, f, *_: fn(b, i, f))
    row_map = wrap(lambda b, i, f: (b, i, 0))
    full = pl.BlockSpec((1, tm, D_MODEL), row_map)
    vec = pl.BlockSpec((1, D_MODEL), wrap(lambda b, i, f: (0, 0)))
    in_specs = [full, full, _mod_spec(mod, GATE2, nb), vec, vec,
                pl.BlockSpec((D_MODEL, tf), wrap(lambda b, i, f: (0, f))),
                pl.BlockSpec((tf, D_MODEL), wrap(lambda b, i, f: (f, 0)))]
    return in_specs, full


def _ffn(h2, x1, mod, ln_g, ln_b, w_up, w_down, tm):
    nb, l, _ = x1.shape
    in_specs, out_spec = _ffn_specs(mod, nb, tm, FFN_TF)
    return pl.pallas_call(
        _ffn_kernel,
        grid=(nb, l // tm, D_FF // FFN_TF),
        in_specs=in_specs,
        out_specs=out_spec,
        out_shape=jax.ShapeDtypeStruct((nb, l, D_MODEL), F32),
        scratch_shapes=[pltpu.VMEM((tm, D_MODEL), F32)],
        compiler_params=_cparams(("arbitrary", "arbitrary", "arbitrary")),
        name="ffn",
    )(h2, x1, mod, ln_g.reshape(1, D_MODEL), ln_b.reshape(1, D_MODEL), w_up, w_down)


def _ffn_attn_kernel(pt_ref, h2_ref, x1_ref, g2_ref, lng_ref, lnb_ref, wu_ref, wd_ref,
                     q_ref, kn_ref, vn_ref, lfn_ref, *refs, n_pg, steps_per_seq):
    kp_refs, vp_refs, wt_refs = refs[:n_pg], refs[n_pg:2 * n_pg], refs[2 * n_pg:3 * n_pg]
    y_ref, o_ref, acc_scr, bias_scr, m_scr, l_scr, a_scr, carry_scr = refs[3 * n_pg:]
    f = pl.program_id(2)
    t = (pl.program_id(0) * pl.num_programs(1) + pl.program_id(1)) * pl.num_programs(2) + f
    si = lax.rem(t, steps_per_seq)
    q = q_ref[0]

    @pl.when(si == 0)
    def _():
        _attn_s_init(q, kn_ref, vn_ref, lfn_ref, bias_scr, m_scr, l_scr, a_scr, carry_scr)

    _ffn_zero(f, acc_scr)
    s = _attn_s_scores(q, kp_refs, wt_refs, bias_scr, carry_scr)
    up = jnp.dot(h2_ref[0], wu_ref[...], preferred_element_type=F32)
    m_new, alpha, p, l_new = _softmax_weights(m_scr[...], l_scr[...], s)
    act = jnp.square(jnp.maximum(up, 0.0)).astype(BF16)
    acc_scr[...] += jnp.dot(act, wd_ref[...], preferred_element_type=F32)
    a_new = alpha * a_scr[...] + _dot_depth_halves(p, _attn_s_values(vp_refs))
    m_scr[...], l_scr[...], a_scr[...] = m_new, l_new, a_new

    @pl.when(si == steps_per_seq - 1)
    def _():
        o_ref[0] = a_new / l_new

    _ffn_finish(f, x1_ref, g2_ref, lng_ref, lnb_ref, y_ref, acc_scr)


def _ffn_attn(h2, x1, mod, ln_g, ln_b, w_up, w_down, tm,
              page_table, q, k_new, v_new, lf_new, cache_k, cache_v, decay_wt):
    nb, l, _ = x1.shape
    nbs, rows, _ = q.shape
    n_pages = page_table.shape[1]
    n_i, n_f = l // tm, D_FF // FUSED_TF
    n_steps = nb * n_i * n_f
    n_pg = nbs * n_pages // n_steps
    steps_per_seq = n_pages // n_pg
    assert n_pg * n_steps == nbs * n_pages and steps_per_seq * n_pg == n_pages

    def step(b, i, f):
        t = (b * n_i + i) * n_f + f
        return lax.div(t, steps_per_seq), lax.rem(t, steps_per_seq)

    seq = lambda b, i, f, pt: (step(b, i, f)[0], 0, 0)

    def page_spec(block, n_lead, pg):
        def idx(b, i, f, pt):
            sq, si = step(b, i, f)
            return (0,) * n_lead + (pt[sq, n_pages - 1 - (si * n_pg + pg)],) + (0,) * (len(block) - n_lead - 1)
        return pl.BlockSpec(block, idx)

    ffn_specs, y_spec = _ffn_specs(mod, nb, tm, FUSED_TF)
    kv_block = (None, None, PAGE, N_HEADS, HEAD_DIM)
    tok_spec = pl.BlockSpec((1, rows, HEAD_DIM), seq)
    grid_spec = pltpu.PrefetchScalarGridSpec(
        num_scalar_prefetch=1,
        grid=(nb, n_i, n_f),
        in_specs=ffn_specs + [tok_spec, tok_spec, tok_spec, pl.BlockSpec((1, 1, LANE), seq)]
                 + [page_spec(kv_block, 1, pg) for pg in range(n_pg)]
                 + [page_spec(kv_block, 1, pg) for pg in range(n_pg)]
                 + [page_spec((1, 2, PAGE_LANES), 0, pg) for pg in range(n_pg)],
        out_specs=[y_spec, tok_spec],
        scratch_shapes=[pltpu.VMEM((tm, D_MODEL), F32),
                        pltpu.VMEM((rows, PAGE_LANES), F32),
                        pltpu.VMEM((rows, 1), F32),
                        pltpu.VMEM((rows, 1), F32),
                        pltpu.VMEM((rows, HEAD_DIM), F32),
                        pltpu.VMEM((1, PAGE_LANES), F32)])
    return pl.pallas_call(
        functools.partial(_ffn_attn_kernel, n_pg=n_pg, steps_per_seq=steps_per_seq),
        grid_spec=grid_spec,
        out_shape=(jax.ShapeDtypeStruct((nb, l, D_MODEL), F32),
                   jax.ShapeDtypeStruct((nbs, rows, HEAD_DIM), F32)),
        compiler_params=_cparams(("arbitrary", "arbitrary", "arbitrary")),
        name="ffn_attn",
    )(page_table, h2, x1, mod, ln_g.reshape(1, D_MODEL), ln_b.reshape(1, D_MODEL), w_up, w_down,
      q, k_new, v_new, lf_new, *([cache_k] * n_pg), *([cache_v] * n_pg), *([decay_wt] * n_pg))


def kernel(x_prompt, x_sample, c_prompt, c_sample, cache_k, cache_v, cache_logf, state_ssm_re,
           state_ssm_im, page_table, w_ada, b_ada, w_in, b_f, w_o, a_re, a_im, log_dt, b_re, b_im,
           c_re, c_im, d_skip, w_glu, b_glu, ln1_g, ln1_b, w_up, w_down, ln2_g, ln2_b):
    assert w_ada.shape[0] == DEPTH == 1
    nbp, s_len, _ = x_prompt.shape
    nbs, n_q, _ = x_sample.shape
    n_seq = nbp + nbs

    c_all = jnp.concatenate([c_prompt, c_sample, jnp.zeros((ADA_ROWS - n_seq, D_MODEL), F32)], axis=0)
    mod = _ada(c_all, w_ada[0], b_ada[0]).reshape(ADA_ROWS, N_MOD, D_MODEL)
    mod_p = _mod_pack(mod[:nbp], 1)
    mod_s = _mod_pack(mod[nbp:n_seq], n_q)

    a = ATT_WIDTH
    w_in_t = w_in[0].T
    w_qkv = w_in_t[:3 * a].astype(BF16)
    w_u = w_in_t[3 * a + N_HEADS:].astype(BF16)
    w_ft = w_in_t[3 * a:3 * a + N_HEADS].astype(BF16)
    w_f = jnp.pad(w_ft, ((0, LANE - N_HEADS), (0, 0)))

    l_re, l_im, bb_re, bb_im = _s5prep(a_re[0], a_im[0], log_dt[0], b_re[0], b_im[0])
    wb, wc, wg, bg, dsk, lam = _s5_weights(l_re, l_im, bb_re, bb_im, c_re[0], c_im[0], d_skip[0],
                                           w_glu[0], b_glu[0])

    rows = nbs * n_q
    xs = x_sample.reshape(1, rows, D_MODEL)
    qs, ks, _, vs, _, us, lfcs, _ = _inproj(xs, mod_s, w_qkv, w_u, w_f, w_ft, b_f[0], tm=rows)
    n_pool = cache_k.shape[1]
    decay_wt = _decay(cache_logf[0].reshape(n_pool, PAGE_LANES))
    per_head = lambda t: t.reshape(nbs, n_q * N_HEADS, HEAD_DIM)
    lf_new = jnp.pad(lfcs.reshape(nbs, 1, n_q * N_HEADS), ((0, 0), (0, 0), (0, LANE - n_q * N_HEADS)))

    q, k, k_b, v, v_b, u, lfc, lfr = _inproj(x_prompt, mod_p, w_qkv, w_u, w_f, w_ft, b_f[0], tm=ROW_TILE)
    att, w_up_b, w_down_b, w_o_b = _attn(q, k_b, v_b, _cumsum(lfr), (w_up[0], w_down[0], w_o[0]))
    ssm, hp_re, hp_im = _s5(u, wb, wc, wg, bg, dsk, lam)
    x1, h2 = _outproj(att, ssm, x_prompt, mod_p, ln1_g[0], ln1_b[0], w_o_b, tm=ROW_TILE)
    y_p, att_s = _ffn_attn(h2, x1, mod_p, ln2_g[0], ln2_b[0], w_up_b, w_down_b, ROW_TILE,
                           page_table, per_head(qs), per_head(ks), per_head(vs), lf_new, cache_k, cache_v,
                           decay_wt)

    u_tm = us.reshape(nbs, n_q, SSM_WIDTH).transpose(1, 0, 2).reshape(rows, SSM_WIDTH)
    h0 = jnp.concatenate([state_ssm_re[0].reshape(nbs, N_GBLK, ST_LANES),
                          state_ssm_im[0].reshape(nbs, N_GBLK, ST_LANES)], axis=2).transpose(1, 0, 2)
    ssm_tm, hs_fin = _s5s(u_tm, h0, wb, wc, wg, bg, dsk, lam, nbs)
    ssm_s = ssm_tm.reshape(n_q, nbs, SSM_WIDTH).transpose(1, 0, 2).reshape(1, rows, SSM_WIDTH)
    hs_fin = hs_fin.transpose(1, 0, 2).reshape(nbs, N_GBLK, 2, GROUPS_PER_BLOCK, SSM_STATE)
    hs_re = hs_fin[:, :, 0].reshape(nbs, N_GROUPS, SSM_STATE)
    hs_im = hs_fin[:, :, 1].reshape(nbs, N_GROUPS, SSM_STATE)
    x1s, h2s = _outproj(att_s.reshape(1, rows, a).astype(BF16), ssm_s, xs, mod_s,
                        ln1_g[0], ln1_b[0], w_o_b, tm=rows)
    y_s = _ffn(h2s, x1s, mod_s, ln2_g[0], ln2_b[0], w_up_b, w_down_b, tm=rows)

    hd = (N_HEADS, HEAD_DIM)
    return (y_p, y_s.reshape(nbs, n_q, D_MODEL),
            k.reshape(1, nbp, s_len, *hd), v.reshape(1, nbp, s_len, *hd), lfc[None],
            hp_re[None], hp_im[None],
            ks.reshape(1, nbs, n_q, *hd), vs.reshape(1, nbs, n_q, *hd), lfcs.reshape(1, nbs, n_q, N_HEADS),
            hs_re[None], hs_im[None])
```
